```python
import jax, jax.numpy as jnp
from jax import lax
import numpy as np

D_MODEL = 2048
BATCH = 4
SEQ = 4096
DEPTH = 2

GRID_W = 64
CTX_LEN = 256
ROPE_BASE = 10000.0
NORM_EPS = 1e-6
NEG_INF = -1e30

SWA_HEADS = 8
SWA_KV_HEADS = 2
SWA_HEAD_DIM = 128
SWA_WINDOW = 128
SWA_BLOCK = 128

MLA_HEADS = 8
MLA_Q_RANK = 512
MLA_KV_RANK = 256
MLA_NOPE_DIM = 128
MLA_ROPE_DIM = 64
MLA_V_DIM = 128
MLA_Q_BLOCK = 128

_AB_SIZES = (SWA_HEADS * SWA_HEAD_DIM, SWA_KV_HEADS * SWA_HEAD_DIM, SWA_KV_HEADS * SWA_HEAD_DIM,
             MLA_Q_RANK, MLA_KV_RANK, MLA_ROPE_DIM)
AB_IN = sum(_AB_SIZES)
AB_SPLITS = tuple(int(s) for s in np.cumsum(_AB_SIZES)[:-1])
AB_OUT = SWA_HEADS * SWA_HEAD_DIM + MLA_HEADS * MLA_V_DIM

RET_HEADS = 8
RET_QK_DIM = D_MODEL // RET_HEADS
RET_V_DIM = 2 * RET_QK_DIM
RET_CHUNK = 128
_RET_SIZES = (RET_HEADS * RET_QK_DIM, RET_HEADS * RET_QK_DIM, RET_HEADS * RET_V_DIM, RET_HEADS * RET_V_DIM)
RET_IN = sum(_RET_SIZES)
RET_SPLITS = tuple(int(s) for s in np.cumsum(_RET_SIZES)[:-1])
RET_OUT = RET_HEADS * RET_V_DIM

FFN_HIDDEN = ((8 * D_MODEL + 3 * 256 - 1) // (3 * 256)) * 256

N_AB = (DEPTH + 1) // 2
N_RET = DEPTH // 2

kernel_name = "hybrid_swa_mla_retention_prefix_dit"


def _rmsnorm(x, g):
    x32 = x.astype(jnp.float32)
    y = x32 * lax.rsqrt(jnp.mean(x32 * x32, axis=-1, keepdims=True) + NORM_EPS)
    return (y * g.astype(jnp.float32)).astype(x.dtype)


def _modulation(cond, w, b):
    return jnp.split(jax.nn.silu(cond) @ w + b, 6, axis=-1)


def _axial_rope_tables(row, col, rot_dim, dtype):
    n_freq = rot_dim // 4
    inv = ROPE_BASE ** (-jnp.arange(n_freq, dtype=jnp.float32) / n_freq)
    ang = jnp.concatenate([row[:, None] * inv, col[:, None] * inv], axis=-1)
    return jnp.cos(ang).astype(dtype), jnp.sin(ang).astype(dtype)


def _apply_rope(x, cos, sin):
    x1, x2 = jnp.split(x, 2, axis=-1)
    c = cos[None, :, None, :]
    s = sin[None, :, None, :]
    return jnp.concatenate([x1 * c - x2 * s, x1 * s + x2 * c], axis=-1)


def _swa_sink_attention(q, k, v, q_c, k_c, v_c, sink):
    B, S, H, d = q.shape
    KV = k.shape[2]
    G = H // KV
    Cn = k_c.shape[1]
    W = SWA_BLOCK
    nb = S // W
    scale = d ** -0.5
    sink_g = sink.reshape(KV, G).astype(jnp.float32)

    qc = q_c.reshape(B, Cn, KV, G, d)
    s_cc = jnp.einsum('bikgd,bjkd->bkgij', qc, k_c).astype(jnp.float32) * scale
    s_cc = jnp.concatenate([s_cc, jnp.broadcast_to(sink_g[None, :, :, None, None], s_cc.shape[:-1] + (1,))], axis=-1)
    p_cc = jax.nn.softmax(s_cc, axis=-1)[..., :Cn].astype(v_c.dtype)
    o_c = jnp.einsum('bkgij,bjkd->bikgd', p_cc, v_c).reshape(B, Cn, H, d)

    pad = ((0, 0), (W, W), (0, 0), (0, 0))
    kp = jnp.pad(k, pad).reshape(B, nb + 2, W, KV, d)
    vp = jnp.pad(v, pad).reshape(B, nb + 2, W, KV, d)
    kb = jnp.concatenate([kp[:, :-2], kp[:, 1:-1], kp[:, 2:]], axis=2)
    vb = jnp.concatenate([vp[:, :-2], vp[:, 1:-1], vp[:, 2:]], axis=2)
    qb = q.reshape(B, nb, W, KV, G, d)
    s_band = jnp.einsum('bnikgd,bnjkd->bnkgij', qb, kb).astype(jnp.float32) * scale
    qi = jnp.arange(W)
    kj = jnp.arange(3 * W)
    blk = jnp.arange(nb)
    rel = kj[None, :] - W - qi[:, None]
    kpos = blk[:, None] * W - W + kj[None, :]
    valid = (jnp.abs(rel) <= SWA_WINDOW)[None] & ((kpos >= 0) & (kpos < S))[:, None, :]
    s_band = jnp.where(valid[None, :, None, None], s_band, NEG_INF)
    s_lc = jnp.einsum('bnikgd,bjkd->bnkgij', qb, k_c).astype(jnp.float32) * scale
    sink_b = jnp.broadcast_to(sink_g[None, None, :, :, None, None], s_band.shape[:-1] + (1,))
    p = jax.nn.softmax(jnp.concatenate([s_band, s_lc, sink_b], axis=-1), axis=-1)
    p_band = p[..., :3 * W].astype(v.dtype)
    p_lc = p[..., 3 * W:3 * W + Cn].astype(v.dtype)
    o = jnp.einsum('bnkgij,bnjkd->bnikgd', p_band, vb) + jnp.einsum('bnkgij,bjkd->bnikgd', p_lc, v_c)
    return o.reshape(B, S, H, d), o_c


def _mla_attend(q_nope, q_rope, k_nope, k_rope, v):
    scale = (MLA_NOPE_DIM + MLA_ROPE_DIM) ** -0.5
    s = (jnp.einsum('bihd,bjhd->bhij', q_nope, k_nope)
         + jnp.einsum('bihd,bjd->bhij', q_rope, k_rope)).astype(jnp.float32) * scale
    p = jax.nn.softmax(s, axis=-1).astype(v.dtype)
    return jnp.einsum('bhij,bjhd->bihd', p, v)


def _ab_mixer(h, hc, w_in, w_out, sink, q_norm_g, w_q_b, kv_norm_g, w_kv_b, cos_a, sin_a, cos_b, sin_b):
    def project(z):
        B, L, _ = z.shape
        qa, ka, va, q_lat, kv_lat, k_r = jnp.split(z @ w_in, AB_SPLITS, axis=-1)
        qa = qa.reshape(B, L, SWA_HEADS, SWA_HEAD_DIM)
        ka = ka.reshape(B, L, SWA_KV_HEADS, SWA_HEAD_DIM)
        va = va.reshape(B, L, SWA_KV_HEADS, SWA_HEAD_DIM)
        qm = (_rmsnorm(q_lat, q_norm_g) @ w_q_b).reshape(B, L, MLA_HEADS, MLA_NOPE_DIM + MLA_ROPE_DIM)
        kvm = (_rmsnorm(kv_lat, kv_norm_g) @ w_kv_b).reshape(B, L, MLA_HEADS, MLA_NOPE_DIM + MLA_V_DIM)
        return (qa, ka, va, qm[..., :MLA_NOPE_DIM], qm[..., MLA_NOPE_DIM:],
                kvm[..., :MLA_NOPE_DIM], k_r, kvm[..., MLA_NOPE_DIM:])

    B, S, _ = h.shape
    qa, ka, va, qn, qr, kn, kr, vm = project(h)
    qa = _apply_rope(qa, cos_a, sin_a)
    ka = _apply_rope(ka, cos_a, sin_a)
    qr = _apply_rope(qr, cos_b, sin_b)
    kr = _apply_rope(kr[:, :, None, :], cos_b, sin_b)[:, :, 0, :]
    qa_c, ka_c, va_c, qn_c, qr_c, kn_c, kr_c, vm_c = project(hc)

    oa, oa_c = _swa_sink_attention(qa, ka, va, qa_c, ka_c, va_c, sink)

    ob_c = _mla_attend(qn_c, qr_c, kn_c, kr_c, vm_c)
    kn_all = jnp.concatenate([kn_c, kn], axis=1)
    kr_all = jnp.concatenate([kr_c, kr], axis=1)
    v_all = jnp.concatenate([vm_c, vm], axis=1)
    nb = S // MLA_Q_BLOCK
    qn_b = qn.reshape(B, nb, MLA_Q_BLOCK, MLA_HEADS, MLA_NOPE_DIM).swapaxes(0, 1)
    qr_b = qr.reshape(B, nb, MLA_Q_BLOCK, MLA_HEADS, MLA_ROPE_DIM).swapaxes(0, 1)
    ob = lax.map(lambda qq: _mla_attend(qq[0], qq[1], kn_all, kr_all, v_all), (qn_b, qr_b))
    ob = ob.swapaxes(0, 1).reshape(B, S, MLA_HEADS * MLA_V_DIM)

    Cn = hc.shape[1]
    out = jnp.concatenate([oa.reshape(B, S, -1), ob], axis=-1) @ w_out
    out_c = jnp.concatenate([oa_c.reshape(B, Cn, -1), ob_c.reshape(B, Cn, -1)], axis=-1) @ w_out
    return out, out_c


def _retention_scan(q, k, v, log_g, s0):
    B, L, H, _ = q.shape
    dv = v.shape[-1]
    C = RET_CHUNK
    n = L // C
    pos = jnp.arange(C, dtype=jnp.float32)
    diff = pos[:, None] - pos[None, :]
    lg = log_g.astype(jnp.float32)
    d_intra = jnp.where(diff[None] >= 0, jnp.exp(lg[:, None, None] * jnp.maximum(diff, 0.0)[None]), 0.0)
    q_dec = jnp.exp(lg[:, None] * (pos + 1.0)[None])[..., None]
    k_dec = jnp.exp(lg[:, None] * (C - 1.0 - pos)[None])[..., None]
    c_dec = jnp.exp(lg * C)[:, None, None]

    def chunks(z):
        return z.astype(jnp.float32).reshape(B, n, C, H, z.shape[-1]).transpose(1, 0, 3, 2, 4)

    def step(s, qkv):
        qi, ki, vi = qkv
        a = jnp.einsum('bhid,bhjd->bhij', qi, ki) * d_intra
        o = jnp.einsum('bhij,bhjv->bhiv', a, vi) + jnp.einsum('bhid,bhdv->bhiv', qi * q_dec, s)
        s = s * c_dec + jnp.einsum('bhjd,bhjv->bhdv', ki * k_dec, vi)
        return s, o

    s_fin, o = lax.scan(step, s0, (chunks(q), chunks(k), chunks(v)))
    o = o.transpose(1, 0, 3, 2, 4).reshape(B, L, H, dv)
    return o.astype(v.dtype), s_fin


def _retention_mixer(h, hc, w_in, logit_f, logit_b, gn_g, w_out, cos_r, sin_r):
    H, dk, dv = RET_HEADS, RET_QK_DIM, RET_V_DIM

    def project(z):
        B, L, _ = z.shape
        q, k, v, g = jnp.split(z @ w_in, RET_SPLITS, axis=-1)
        return q.reshape(B, L, H, dk), k.reshape(B, L, H, dk) * (dk ** -0.5), v.reshape(B, L, H, dv), g

    q, k, v, g = project(h)
    q = _apply_rope(q, cos_r, sin_r)
    k = _apply_rope(k, cos_r, sin_r)
    qc, kc, vc, gc = project(hc)
    lg_f = jax.nn.log_sigmoid(logit_f.astype(jnp.float32))
    lg_b = jax.nn.log_sigmoid(logit_b.astype(jnp.float32))
    s0 = jnp.zeros((h.shape[0], H, dk, dv), jnp.float32)

    def flip(z):
        return jnp.flip(z, axis=1)

    oc_f, s_f = _retention_scan(qc, kc, vc, lg_f, s0)
    oc_b, s_b = _retention_scan(flip(qc), flip(kc), flip(vc), lg_b, s0)
    o_f, _ = _retention_scan(q, k, v, lg_f, s_f)
    o_b, _ = _retention_scan(flip(q), flip(k), flip(v), lg_b, s_b)

    def finish(o, gate):
        B, L = o.shape[:2]
        o32 = o.astype(jnp.float32)
        mu = jnp.mean(o32, axis=-1, keepdims=True)
        var = jnp.mean(jnp.square(o32 - mu), axis=-1, keepdims=True)
        y = ((o32 - mu) * lax.rsqrt(var + NORM_EPS)).reshape(B, L, H * dv) * gn_g.astype(jnp.float32)
        return (jax.nn.silu(gate) * y.astype(gate.dtype)) @ w_out

    return finish(o_f + flip(o_b), g), finish(oc_f + flip(oc_b), gc)


def _swiglu(h, wg, wu, wd):
    return (jax.nn.silu(h @ wg) * (h @ wu)) @ wd


def setup_inputs(seed: int = 0) -> dict:
    key = jax.random.key(seed)
    ks = jax.random.split(key, 32)
    f32 = jnp.float32

    def nrm(i, shape, scale):
        return jax.random.normal(ks[i], shape, f32) * scale

    decay_base = jnp.log(2.0 ** (5.0 + jnp.arange(RET_HEADS, dtype=f32)) - 1.0)
    return {
        "x": nrm(0, (BATCH, SEQ, D_MODEL), 1.0),
        "c": nrm(1, (BATCH, D_MODEL), 1.0),
        "ctx": nrm(2, (BATCH, CTX_LEN, D_MODEL), 1.0),
        "c_ctx": nrm(3, (D_MODEL,), 1.0),
        "mod_w": nrm(4, (DEPTH, D_MODEL, 6 * D_MODEL), 0.5 * D_MODEL ** -0.5),
        "mod_b": nrm(5, (DEPTH, 6 * D_MODEL), 0.02),
        "norm_mix_g": 1.0 + nrm(6, (DEPTH, D_MODEL), 0.02),
        "norm_ffn_g": 1.0 + nrm(7, (DEPTH, D_MODEL), 0.02),
        "ffn_w_gate": nrm(8, (DEPTH, D_MODEL, FFN_HIDDEN), D_MODEL ** -0.5),
        "ffn_w_up": nrm(9, (DEPTH, D_MODEL, FFN_HIDDEN), D_MODEL ** -0.5),
        "ffn_w_down": nrm(10, (DEPTH, FFN_HIDDEN, D_MODEL), FFN_HIDDEN ** -0.5),
        "ab_w_in": nrm(11, (N_AB, D_MODEL, AB_IN), D_MODEL ** -0.5),
        "ab_w_out": nrm(12, (N_AB, AB_OUT, D_MODEL), AB_OUT ** -0.5),
        "swa_sink": nrm(13, (N_AB, SWA_HEADS), 0.5),
        "mla_q_norm_g": 1.0 + nrm(14, (N_AB, MLA_Q_RANK), 0.02),
        "mla_w_q_b": nrm(15, (N_AB, MLA_Q_RANK, MLA_HEADS * (MLA_NOPE_DIM + MLA_ROPE_DIM)), MLA_Q_RANK ** -0.5),
        "mla_kv_norm_g": 1.0 + nrm(16, (N_AB, MLA_KV_RANK), 0.02),
        "mla_w_kv_b": nrm(17, (N_AB, MLA_KV_RANK, MLA_HEADS * (MLA_NOPE_DIM + MLA_V_DIM)), MLA_KV_RANK ** -0.5),
        "ret_w_in": nrm(18, (N_RET, D_MODEL, RET_IN), D_MODEL ** -0.5),
        "ret_decay_logit_fwd": decay_base + nrm(19, (N_RET, RET_HEADS), 0.1),
        "ret_decay_logit_bwd": decay_base + nrm(20, (N_RET, RET_HEADS), 0.1),
        "ret_gn_g": 1.0 + nrm(21, (N_RET, RET_OUT), 0.02),
        "ret_w_out": nrm(22, (N_RET, RET_OUT, D_MODEL), RET_OUT ** -0.5),
        "final_norm_g": 1.0 + nrm(23, (D_MODEL,), 0.02),
    }


def reference(x, c, ctx, c_ctx, mod_w, mod_b, norm_mix_g, norm_ffn_g, ffn_w_gate, ffn_w_up, ffn_w_down,
              ab_w_in, ab_w_out, swa_sink, mla_q_norm_g, mla_w_q_b, mla_kv_norm_g, mla_w_kv_b,
              ret_w_in, ret_decay_logit_fwd, ret_decay_logit_bwd, ret_gn_g, ret_w_out, final_norm_g):
    n_tok = x.shape[1]
    rows = n_tok // GRID_W
    row = jnp.repeat(jnp.arange(rows, dtype=jnp.float32), GRID_W)
    col = (jnp.arange(n_tok) % GRID_W).astype(jnp.float32)
    cos_a, sin_a = _axial_rope_tables(row, col, SWA_HEAD_DIM, x.dtype)
    cos_b, sin_b = _axial_rope_tables(row, col, MLA_ROPE_DIM, x.dtype)
    cos_r, sin_r = _axial_rope_tables(row, col, RET_QK_DIM, x.dtype)

    xc = ctx
    for l in range(DEPTH):
        last = l == DEPTH - 1
        sh1, sc1, g1, sh2, sc2, g2 = [t[:, None, :] for t in _modulation(c, mod_w[l], mod_b[l])]
        csh1, csc1, cg1, csh2, csc2, cg2 = _modulation(c_ctx, mod_w[l], mod_b[l])
        h = _rmsnorm(x, norm_mix_g[l]) * (1.0 + sc1) + sh1
        hc = _rmsnorm(xc, norm_mix_g[l]) * (1.0 + csc1) + csh1
        i = l // 2
        if l % 2 == 0:
            o, oc = _ab_mixer(h, hc, ab_w_in[i], ab_w_out[i], swa_sink[i], mla_q_norm_g[i], mla_w_q_b[i],
                              mla_kv_norm_g[i], mla_w_kv_b[i], cos_a, sin_a, cos_b, sin_b)
        else:
            o, oc = _retention_mixer(h, hc, ret_w_in[i], ret_decay_logit_fwd[i], ret_decay_logit_bwd[i],
                                     ret_gn_g[i], ret_w_out[i], cos_r, sin_r)
        x = x + g1 * o
        h = _rmsnorm(x, norm_ffn_g[l]) * (1.0 + sc2) + sh2
        x = x + g2 * _swiglu(h, ffn_w_gate[l], ffn_w_up[l], ffn_w_down[l])
        if not last:
            xc = xc + cg1 * oc
            hc = _rmsnorm(xc, norm_ffn_g[l]) * (1.0 + csc2) + csh2
            xc = xc + cg2 * _swiglu(hc, ffn_w_gate[l], ffn_w_up[l], ffn_w_down[l])
    return _rmsnorm(x, final_norm_g)
```

```python
import functools

import jax
import jax.numpy as jnp
from jax import lax
from jax.experimental import pallas as pl
from jax.experimental.pallas import tpu as pltpu

GRID_W = 64
ROPE_BASE = 10000.0
NORM_EPS = 1e-6
NEG_INF = -1e30

SWA_HEADS = 8
SWA_KV_HEADS = 2
SWA_HEAD_DIM = 128
SWA_WINDOW = 128

MLA_HEADS = 8
MLA_Q_RANK = 512
MLA_KV_RANK = 256
MLA_NOPE_DIM = 128
MLA_ROPE_DIM = 64
MLA_V_DIM = 128
MLA_HEAD_PAD = 256

RET_HEADS = 8
RET_CHUNK = 128

LANES = 128
V7X_VMEM_BYTES = 64 * 1024 * 1024
VMEM_LIMIT = V7X_VMEM_BYTES - 8 * 1024 * 1024

F32 = jnp.float32
BF16 = jnp.bfloat16


def _cparams(sem):
    return pltpu.CompilerParams(dimension_semantics=sem, vmem_limit_bytes=VMEM_LIMIT)


def _dot(a, b):
    return jnp.dot(a, b, preferred_element_type=F32)


def _dot_nt(a, b):
    return lax.dot_general(a, b, (((1,), (1,)), ((), ())), preferred_element_type=F32)


def _dot_tn(a, b):
    return lax.dot_general(a, b, (((0,), (0,)), ((), ())), preferred_element_type=F32)


def _silu(x):
    return x * (1.0 / (1.0 + jnp.exp(-x)))


def _norm_mod(x, g, shift, scale):
    y = x * lax.rsqrt(jnp.mean(x * x, axis=-1, keepdims=True) + NORM_EPS)
    return (y * g) * (1.0 + scale) + shift


def _mod_kernel(c_ref, w_ref, b_ref, o_ref):
    a = _silu(c_ref[...]).astype(BF16)
    o_ref[0] = _dot(a, w_ref[0].astype(BF16)) + b_ref[0]


def _modulation(cond8, mod_w, mod_b):
    depth, d, n = mod_w.shape
    tn = 1024
    return pl.pallas_call(
        _mod_kernel,
        grid=(depth, n // tn),
        in_specs=[
            pl.BlockSpec((8, d), lambda l, j: (0, 0)),
            pl.BlockSpec((1, d, tn), lambda l, j: (l, 0, j)),
            pl.BlockSpec((1, 1, tn), lambda l, j: (l, 0, j)),
        ],
        out_specs=pl.BlockSpec((1, 8, tn), lambda l, j: (l, 0, j)),
        out_shape=jax.ShapeDtypeStruct((depth, 8, n), F32),
        compiler_params=_cparams(("arbitrary", "arbitrary")),
    )(cond8, mod_w, mod_b.reshape(depth, 1, n))


def _rope128(z, cosf, sins):
    return z * cosf + pltpu.roll(z, 64, axis=1) * sins


def _ab_in_kernel(x_ref, g_ref, mod_ref, w_ref, cos_ref, sin_ref, a_ref, l_ref, h_ref):
    j = pl.program_id(1)

    @pl.when(j == 0)
    def _():
        h = _norm_mod(x_ref[...], g_ref[...], mod_ref[0, 0:1, :], mod_ref[0, 1:2, :])
        h_ref[...] = h.astype(BF16)

    z = _dot(h_ref[...], w_ref[...])
    cosf = cos_ref[...]
    sins = sin_ref[...]

    @pl.when(j <= 1)
    def _():
        for k in range(4):
            sl = slice(k * LANES, (k + 1) * LANES)
            a_ref[:, sl] = _rope128(z[:, sl], cosf, sins).astype(BF16)

    @pl.when(j == 2)
    def _():
        for k in range(2):
            sl = slice(k * LANES, (k + 1) * LANES)
            a_ref[:, sl] = _rope128(z[:, sl], cosf, sins).astype(BF16)
        a_ref[:, 2 * LANES:] = z[:, 2 * LANES:].astype(BF16)

    @pl.when(j >= 3)
    def _():
        l_ref[...] = z


def _ab_in_proj(xt, norm_g, mod, w_in, cos_a, sin_a, mrow, tm):
    nt, d = xt.shape
    tn = 512
    return pl.pallas_call(
        _ab_in_kernel,
        grid=(nt // tm, 5),
        in_specs=[
            pl.BlockSpec((tm, d), lambda i, j: (i, 0)),
            pl.BlockSpec((1, d), lambda i, j: (0, 0)),
            pl.BlockSpec((1, 6, d), lambda i, j: (mrow(i), 0, 0)),
            pl.BlockSpec((d, tn), lambda i, j: (0, j)),
            pl.BlockSpec((tm, LANES), lambda i, j: (i, 0)),
            pl.BlockSpec((tm, LANES), lambda i, j: (i, 0)),
        ],
        out_specs=[
            pl.BlockSpec((tm, tn), lambda i, j: (i, jnp.minimum(j, 2))),
            pl.BlockSpec((tm, tn), lambda i, j: (i, jnp.maximum(j - 3, 0))),
        ],
        out_shape=[
            jax.ShapeDtypeStruct((nt, 3 * tn), BF16),
            jax.ShapeDtypeStruct((nt, 2 * tn), F32),
        ],
        scratch_shapes=[pltpu.VMEM((tm, d), BF16)],
        compiler_params=_cparams(("arbitrary", "arbitrary")),
    )(xt, norm_g, mod, w_in, cos_a, sin_a)


def _rope64(r, cosp, sinp):
    lane = lax.broadcasted_iota(jnp.int32, r.shape, 1)
    partner = jnp.where(lane < 32, pltpu.roll(r, 96, axis=1), pltpu.roll(r, 32, axis=1))
    return r * cosp + partner * sinp


def _mla_proj_kernel(l_ref, qg_ref, kvg_ref, wq_ref, wkv_ref, cos_ref, sin_ref, qm_ref, kc_ref, vm_ref):
    cosp = cos_ref[...]
    sinp = sin_ref[...]

    def rms(x, g):
        return x * lax.rsqrt(jnp.mean(x * x, axis=-1, keepdims=True) + NORM_EPS) * g

    qn = rms(l_ref[:, :MLA_Q_RANK], qg_ref[...]).astype(BF16)
    qm = _dot(qn, wq_ref[...])
    kvn = rms(l_ref[:, MLA_Q_RANK:MLA_Q_RANK + MLA_KV_RANK], kvg_ref[...]).astype(BF16)
    kv = _dot(kvn, wkv_ref[...])
    kr = l_ref[:, MLA_Q_RANK + MLA_KV_RANK:MLA_Q_RANK + MLA_KV_RANK + LANES]
    krr = _rope64(kr, cosp, sinp).astype(BF16)
    for h in range(MLA_HEADS):
        o = h * MLA_HEAD_PAD
        qm_ref[:, o:o + LANES] = qm[:, o:o + LANES].astype(BF16)
        qm_ref[:, o + LANES:o + 2 * LANES] = _rope64(qm[:, o + LANES:o + 2 * LANES], cosp, sinp).astype(BF16)
        kc_ref[:, o:o + LANES] = kv[:, o:o + LANES].astype(BF16)
        kc_ref[:, o + LANES:o + 2 * LANES] = krr
        vm_ref[:, h * LANES:(h + 1) * LANES] = kv[:, o + LANES:o + 2 * LANES].astype(BF16)


def _mla_proj(lat, q_norm_g, kv_norm_g, wq, wkv, cos_b, sin_b, tm):
    nt = lat.shape[0]
    hp = MLA_HEADS * MLA_HEAD_PAD
    return pl.pallas_call(
        _mla_proj_kernel,
        grid=(nt // tm,),
        in_specs=[
            pl.BlockSpec((tm, lat.shape[1]), lambda i: (i, 0)),
            pl.BlockSpec((1, MLA_Q_RANK), lambda i: (0, 0)),
            pl.BlockSpec((1, MLA_KV_RANK), lambda i: (0, 0)),
            pl.BlockSpec(wq.shape, lambda i: (0, 0)),
            pl.BlockSpec(wkv.shape, lambda i: (0, 0)),
            pl.BlockSpec((tm, LANES), lambda i: (i, 0)),
            pl.BlockSpec((tm, LANES), lambda i: (i, 0)),
        ],
        out_specs=[
            pl.BlockSpec((tm, hp), lambda i: (i, 0)),
            pl.BlockSpec((tm, hp), lambda i: (i, 0)),
            pl.BlockSpec((tm, MLA_HEADS * MLA_V_DIM), lambda i: (i, 0)),
        ],
        out_shape=[
            jax.ShapeDtypeStruct((nt, hp), BF16),
            jax.ShapeDtypeStruct((nt, hp), BF16),
            jax.ShapeDtypeStruct((nt, MLA_HEADS * MLA_V_DIM), BF16),
        ],
        compiler_params=_cparams(("arbitrary",)),
    )(lat, q_norm_g, kv_norm_g, wq, wkv, cos_b, sin_b)


def _swa_kernel(sink_ref, q_ref, *refs, tq, s_len, latent):
    kvh = pl.program_id(1)
    d = SWA_HEAD_DIM
    g_heads = SWA_HEADS // SWA_KV_HEADS
    scale = d ** -0.5
    if latent:
        kp, km, kn, kc, vp, vm, vn, vc, o_ref = refs
        i = pl.program_id(2)
        k_all = jnp.concatenate([kp[...], km[...], kn[...], kc[...]], axis=0)
        v_all = jnp.concatenate([vp[...], vm[...], vn[...], vc[...]], axis=0)
        nb = tq + 2 * SWA_WINDOW
        shape = (tq, k_all.shape[0])
        row = lax.broadcasted_iota(jnp.int32, shape, 0)
        col = lax.broadcasted_iota(jnp.int32, shape, 1)
        rel = col - SWA_WINDOW - row
        kpos = i * tq - SWA_WINDOW + col
        valid = (col >= nb) | ((jnp.abs(rel) <= SWA_WINDOW) & (kpos >= 0) & (kpos < s_len))
    else:
        kc, vc, o_ref = refs
        k_all = kc[...]
        v_all = vc[...]
        valid = None
    for g in range(g_heads):
        sk = sink_ref[kvh * g_heads + g]
        s = _dot_nt(q_ref[:, g * d:(g + 1) * d], k_all) * scale
        if valid is not None:
            s = jnp.where(valid, s, NEG_INF)
        m = jnp.maximum(jnp.max(s, axis=-1, keepdims=True), sk)
        p = jnp.exp(s - m)
        den = jnp.sum(p, axis=-1, keepdims=True) + jnp.exp(sk - m)
        o = _dot(p.astype(BF16), v_all) / den
        o_ref[:, g * d:(g + 1) * d] = o.astype(BF16)


def _swa_attention(a, sink, b_sz, cn, s_len, tq):
    nt = a.shape[0]
    d = SWA_HEAD_DIM
    w = SWA_WINDOW
    gw = (SWA_HEADS // SWA_KV_HEADS) * d
    kcol = SWA_HEADS
    vcol = SWA_HEADS + SWA_KV_HEADS
    nq = s_len // tq
    r = tq // w
    c0 = (b_sz * s_len) // cn
    out_shape = jax.ShapeDtypeStruct((nt, SWA_HEADS * d), BF16)
    smem = pl.BlockSpec(memory_space=pltpu.SMEM)

    def main(b, h, i):
        return b * nq + i

    def prev(b, h, i):
        return b * (s_len // w) + jnp.maximum(i * r - 1, 0)

    def nxt(b, h, i):
        return b * (s_len // w) + jnp.minimum((i + 1) * r, s_len // w - 1)

    o_lat = pl.pallas_call(
        functools.partial(_swa_kernel, tq=tq, s_len=s_len, latent=True),
        grid=(b_sz, SWA_KV_HEADS, nq),
        in_specs=[
            smem,
            pl.BlockSpec((tq, gw), lambda b, h, i: (main(b, h, i), h)),
            pl.BlockSpec((w, d), lambda b, h, i: (prev(b, h, i), kcol + h)),
            pl.BlockSpec((tq, d), lambda b, h, i: (main(b, h, i), kcol + h)),
            pl.BlockSpec((w, d), lambda b, h, i: (nxt(b, h, i), kcol + h)),
            pl.BlockSpec((cn, d), lambda b, h, i: (c0 + b, kcol + h)),
            pl.BlockSpec((w, d), lambda b, h, i: (prev(b, h, i), vcol + h)),
            pl.BlockSpec((tq, d), lambda b, h, i: (main(b, h, i), vcol + h)),
            pl.BlockSpec((w, d), lambda b, h, i: (nxt(b, h, i), vcol + h)),
            pl.BlockSpec((cn, d), lambda b, h, i: (c0 + b, vcol + h)),
        ],
        out_specs=pl.BlockSpec((tq, gw), lambda b, h, i: (main(b, h, i), h)),
        out_shape=out_shape,
        compiler_params=_cparams(("arbitrary", "arbitrary", "arbitrary")),
    )(sink, a, a, a, a, a, a, a, a, a)

    o_all = pl.pallas_call(
        functools.partial(_swa_ctx_kernel, tq=cn, s_len=s_len),
        grid=(b_sz, SWA_KV_HEADS),
        in_specs=[
            smem,
            pl.BlockSpec((cn, gw), lambda b, h: (c0 + b, h)),
            pl.BlockSpec((cn, d), lambda b, h: (c0 + b, kcol + h)),
            pl.BlockSpec((cn, d), lambda b, h: (c0 + b, vcol + h)),
            pl.BlockSpec(memory_space=pl.ANY),
        ],
        out_specs=pl.BlockSpec((cn, gw), lambda b, h: (c0 + b, h)),
        out_shape=out_shape,
        input_output_aliases={4: 0},
        compiler_params=_cparams(("arbitrary", "arbitrary")),
    )(sink, a, a, a, o_lat)
    return o_all


def _swa_ctx_kernel(sink_ref, q_ref, kc, vc, prev_ref, o_ref, *, tq, s_len):
    del prev_ref
    _swa_kernel(sink_ref, q_ref, kc, vc, o_ref, tq=tq, s_len=s_len, latent=False)


def _mla_kernel(q_ref, kc_ref, vc_ref, *refs, nk, latent):
    scale = (MLA_NOPE_DIM + MLA_ROPE_DIM) ** -0.5
    if latent:
        kl_ref, vl_ref, o_ref, m_ref, l_ref, acc_ref = refs
        k = pl.program_id(3)
    else:
        o_ref, m_ref, l_ref, acc_ref = refs
        k = 0
    q = q_ref[...]

    def first():
        s = _dot_nt(q, kc_ref[...]) * scale
        m = jnp.max(s, axis=-1, keepdims=True)
        p = jnp.exp(s - m)
        m_ref[...] = m
        l_ref[...] = jnp.sum(p, axis=-1, keepdims=True)
        acc_ref[...] = _dot(p.astype(BF16), vc_ref[...])

    if latent:
        pl.when(k == 0)(first)
        s = _dot_nt(q, kl_ref[...]) * scale
        m_prev = m_ref[...]
        m_new = jnp.maximum(m_prev, jnp.max(s, axis=-1, keepdims=True))
        alpha = jnp.exp(m_prev - m_new)
        p = jnp.exp(s - m_new)
        l_ref[...] = alpha * l_ref[...] + jnp.sum(p, axis=-1, keepdims=True)
        acc_ref[...] = alpha * acc_ref[...] + _dot(p.astype(BF16), vl_ref[...])
        m_ref[...] = m_new

        @pl.when(k == nk - 1)
        def _():
            o_ref[...] = (acc_ref[...] / l_ref[...]).astype(BF16)
    else:
        first()
        o_ref[...] = (acc_ref[...] / l_ref[...]).astype(BF16)


def _mla_ctx_kernel(q_ref, kc_ref, vc_ref, prev_ref, o_ref, m_ref, l_ref, acc_ref):
    del prev_ref
    _mla_kernel(q_ref, kc_ref, vc_ref, o_ref, m_ref, l_ref, acc_ref, nk=1, latent=False)


def _mla_attention(qm, kcat, vm, b_sz, cn, s_len, tq, tk):
    nt = qm.shape[0]
    hp = MLA_HEAD_PAD
    dv = MLA_V_DIM
    nq = s_len // tq
    nk = s_len // tk
    c0 = (b_sz * s_len) // cn
    out_shape = jax.ShapeDtypeStruct((nt, MLA_HEADS * dv), BF16)

    def scratch(rows):
        return [pltpu.VMEM((rows, 1), F32), pltpu.VMEM((rows, 1), F32), pltpu.VMEM((rows, dv), F32)]

    o_lat = pl.pallas_call(
        functools.partial(_mla_kernel, nk=nk, latent=True),
        grid=(b_sz, MLA_HEADS, nq, nk),
        in_specs=[
            pl.BlockSpec((tq, hp), lambda b, h, i, k: (b * nq + i, h)),
            pl.BlockSpec((cn, hp), lambda b, h, i, k: (c0 + b, h)),
            pl.BlockSpec((cn, dv), lambda b, h, i, k: (c0 + b, h)),
            pl.BlockSpec((tk, hp), lambda b, h, i, k: (b * nk + k, h)),
            pl.BlockSpec((tk, dv), lambda b, h, i, k: (b * nk + k, h)),
        ],
        out_specs=pl.BlockSpec((tq, dv), lambda b, h, i, k: (b * nq + i, h)),
        out_shape=out_shape,
        scratch_shapes=scratch(tq),
        compiler_params=_cparams(("arbitrary",) * 4),
    )(qm, kcat, vm, kcat, vm)

    return pl.pallas_call(
        _mla_ctx_kernel,
        grid=(b_sz, MLA_HEADS),
        in_specs=[
            pl.BlockSpec((cn, hp), lambda b, h: (c0 + b, h)),
            pl.BlockSpec((cn, hp), lambda b, h: (c0 + b, h)),
            pl.BlockSpec((cn, dv), lambda b, h: (c0 + b, h)),
            pl.BlockSpec(memory_space=pl.ANY),
        ],
        out_specs=pl.BlockSpec((cn, dv), lambda b, h: (c0 + b, h)),
        out_shape=out_shape,
        scratch_shapes=scratch(cn),
        input_output_aliases={3: 0},
        compiler_params=_cparams(("arbitrary", "arbitrary")),
    )(qm, kcat, vm, o_lat)


def _out_proj_kernel(*refs, n_parts):
    parts = refs[:n_parts]
    w_ref, x_ref, mod_ref, o_ref = refs[n_parts:]
    acc = None
    k0 = 0
    for p in parts:
        kw = p.shape[1]
        t = _dot(p[...], w_ref[k0:k0 + kw, :])
        acc = t if acc is None else acc + t
        k0 += kw
    o_ref[...] = x_ref[...] + mod_ref[0, 2:3, :] * acc


def _out_proj(parts, w, xt, mod, mrow, tm, tn):
    rows = parts[0].shape[0]
    nt, d = xt.shape
    kdim = w.shape[0]
    part_specs = [pl.BlockSpec((tm, p.shape[1]), lambda i, j: (i, 0)) for p in parts]
    return pl.pallas_call(
        functools.partial(_out_proj_kernel, n_parts=len(parts)),
        grid=(rows // tm, d // tn),
        in_specs=part_specs + [
            pl.BlockSpec((kdim, tn), lambda i, j: (0, j)),
            pl.BlockSpec((tm, tn), lambda i, j: (i, j)),
            pl.BlockSpec((1, 6, tn), lambda i, j: (mrow(i), 0, j)),
        ],
        out_specs=pl.BlockSpec((tm, tn), lambda i, j: (i, j)),
        out_shape=jax.ShapeDtypeStruct((nt, d), F32),
        input_output_aliases={len(parts) + 1: 0},
        compiler_params=_cparams(("arbitrary", "arbitrary")),
    )(*parts, w, xt, mod)


def _ffn_kernel(x_ref, g_ref, mod_ref, wg_ref, wu_ref, wd_ref, fg_ref, o_ref, h_ref, acc_ref, *, nf, final):
    j = pl.program_id(1)

    @pl.when(j == 0)
    def _():
        h = _norm_mod(x_ref[...], g_ref[...], mod_ref[0, 3:4, :], mod_ref[0, 4:5, :])
        h_ref[...] = h.astype(BF16)

    h = h_ref[...]
    a = (_silu(_dot(h, wg_ref[...])) * _dot(h, wu_ref[...])).astype(BF16)
    part = _dot(a, wd_ref[...])

    @pl.when(j == 0)
    def _():
        acc_ref[...] = part

    @pl.when(j > 0)
    def _():
        acc_ref[...] += part

    @pl.when(j == nf - 1)
    def _():
        y = x_ref[...] + mod_ref[0, 5:6, :] * acc_ref[...]
        if final:
            y = y * lax.rsqrt(jnp.mean(y * y, axis=-1, keepdims=True) + NORM_EPS) * fg_ref[...]
        o_ref[...] = y


def _ffn(xt, norm_g, mod, wg, wu, wd, final_g, mrow, n_rows, tm, tf, final):
    nt, d = xt.shape
    f = wg.shape[1]
    nf = f // tf
    assert final or n_rows == nt
    kwargs = {} if final else {"input_output_aliases": {0: 0}}
    return pl.pallas_call(
        functools.partial(_ffn_kernel, nf=nf, final=final),
        grid=(n_rows // tm, nf),
        in_specs=[
            pl.BlockSpec((tm, d), lambda i, j: (i, 0)),
            pl.BlockSpec((1, d), lambda i, j: (0, 0)),
            pl.BlockSpec((1, 6, d), lambda i, j: (mrow(i), 0, 0)),
            pl.BlockSpec((d, tf), lambda i, j: (0, j)),
            pl.BlockSpec((d, tf), lambda i, j: (0, j)),
            pl.BlockSpec((tf, d), lambda i, j: (j, 0)),
            pl.BlockSpec((1, d), lambda i, j: (0, 0)),
        ],
        out_specs=pl.BlockSpec((tm, d), lambda i, j: (i, 0)),
        out_shape=jax.ShapeDtypeStruct((n_rows, d), F32),
        scratch_shapes=[pltpu.VMEM((tm, d), BF16), pltpu.VMEM((tm, d), F32)],
        compiler_params=_cparams(("arbitrary", "arbitrary")),
        **kwargs,
    )(xt, norm_g, mod, wg, wu, wd, final_g)


def _ret_in_kernel(x_ref, g_ref, mod_ref, w_ref, cos_ref, sin_ref, o_ref, h_ref, *, n_qk, k_scale, tn):
    j = pl.program_id(1)

    @pl.when(j == 0)
    def _():
        h = _norm_mod(x_ref[...], g_ref[...], mod_ref[0, 0:1, :], mod_ref[0, 1:2, :])
        h_ref[...] = h.astype(BF16)

    z = _dot(h_ref[...], w_ref[...])

    def rope(scale):
        c = cos_ref[...]
        s = sin_ref[...]
        for k in range(tn // (2 * LANES)):
            o = 2 * k * LANES
            x1 = z[:, o:o + LANES]
            x2 = z[:, o + LANES:o + 2 * LANES]
            o_ref[:, o:o + LANES] = ((x1 * c - x2 * s) * scale).astype(BF16)
            o_ref[:, o + LANES:o + 2 * LANES] = ((x1 * s + x2 * c) * scale).astype(BF16)

    @pl.when(j < n_qk)
    def _():
        rope(1.0)

    @pl.when((j >= n_qk) & (j < 2 * n_qk))
    def _():
        rope(k_scale)

    @pl.when(j >= 2 * n_qk)
    def _():
        o_ref[...] = z.astype(BF16)


def _ret_in_proj(xt, norm_g, mod, w_in, cos_r, sin_r, mrow, tm, tn):
    nt, d = xt.shape
    n = w_in.shape[1]
    dk = d // RET_HEADS
    return pl.pallas_call(
        functools.partial(_ret_in_kernel, n_qk=d // tn, k_scale=dk ** -0.5, tn=tn),
        grid=(nt // tm, n // tn),
        in_specs=[
            pl.BlockSpec((tm, d), lambda i, j: (i, 0)),
            pl.BlockSpec((1, d), lambda i, j: (0, 0)),
            pl.BlockSpec((1, 6, d), lambda i, j: (mrow(i), 0, 0)),
            pl.BlockSpec((d, tn), lambda i, j: (0, j)),
            pl.BlockSpec((tm, LANES), lambda i, j: (i, 0)),
            pl.BlockSpec((tm, LANES), lambda i, j: (i, 0)),
        ],
        out_specs=pl.BlockSpec((tm, tn), lambda i, j: (i, j)),
        out_shape=jax.ShapeDtypeStruct((nt, n), BF16),
        scratch_shapes=[pltpu.VMEM((tm, d), BF16)],
        compiler_params=_cparams(("arbitrary", "arbitrary")),
    )(xt, norm_g, mod, w_in, cos_r, sin_r)


def _log_sigmoid(x):
    return jnp.minimum(x, 0.0) - jnp.log1p(jnp.exp(-jnp.abs(x)))


def _ret_kernel(lf_ref, lb_ref, gn_ref, qc_ref, kc_ref, vc_ref, q_ref, k_ref, v_ref, g_ref, y_ref,
                of_ref, s_ref, *, n_ctx, n_lat):
    c = RET_CHUNK
    dk = q_ref.shape[1]
    dv = v_ref.shape[1]
    row = lax.broadcasted_iota(jnp.int32, (c, c), 0)
    col = lax.broadcasted_iota(jnp.int32, (c, c), 1)
    pos_k = lax.broadcasted_iota(jnp.int32, (c, dk), 0).astype(F32)

    def decays(l_ref, backward):
        lg = _log_sigmoid(l_ref[0])
        lg_c = jnp.broadcast_to(lg[:, 0:1], (1, c))
        lg_k = jnp.broadcast_to(lg[:, 0:1], (1, dk))
        lg_v = jnp.broadcast_to(lg[:, 0:1], (1, dv))
        diff = (col - row) if backward else (row - col)
        d_intra = jnp.where(diff >= 0, jnp.exp(lg_c * jnp.maximum(diff, 0).astype(F32)), 0.0)
        if backward:
            q_dec = jnp.exp(lg_k * (c - pos_k))
            k_dec = jnp.exp(lg_k * pos_k)
        else:
            q_dec = jnp.exp(lg_k * (pos_k + 1.0))
            k_dec = jnp.exp(lg_k * (c - 1.0 - pos_k))
        c_dec = jnp.exp(lg_v * c)
        return d_intra, q_dec, k_dec, c_dec

    def step(q, k, v, dec, want_out):
        d_intra, q_dec, k_dec, c_dec = dec
        s = s_ref[...]
        o = None
        if want_out:
            a = (_dot_nt(q, k) * d_intra).astype(BF16)
            qd = (q.astype(F32) * q_dec).astype(BF16)
            o = _dot(a, v) + _dot(qd, s.astype(BF16))
        kd = (k.astype(F32) * k_dec).astype(BF16)
        s_ref[...] = s * c_dec + _dot_tn(kd, v)
        return o

    def chunk(ref, t):
        return ref[pl.ds(pl.multiple_of(t * c, c), c), :]

    dec_f = decays(lf_ref, False)
    s_ref[...] = jnp.zeros_like(s_ref)
    for t in range(n_ctx):
        step(qc_ref[t * c:(t + 1) * c, :], kc_ref[t * c:(t + 1) * c, :], vc_ref[t * c:(t + 1) * c, :], dec_f, False)

    def fwd_body(t, carry):
        o = step(chunk(q_ref, t), chunk(k_ref, t), chunk(v_ref, t), dec_f, True)
        of_ref[pl.ds(pl.multiple_of(t * c, c), c), :] = o
        return carry

    lax.fori_loop(0, n_lat, fwd_body, 0)

    dec_b = decays(lb_ref, True)
    s_ref[...] = jnp.zeros_like(s_ref)
    for t in reversed(range(n_ctx)):
        step(qc_ref[t * c:(t + 1) * c, :], kc_ref[t * c:(t + 1) * c, :], vc_ref[t * c:(t + 1) * c, :], dec_b, False)
    gn = gn_ref[...]

    def bwd_body(tt, carry):
        t = n_lat - 1 - tt
        sl = pl.ds(pl.multiple_of(t * c, c), c)
        o = of_ref[sl, :] + step(chunk(q_ref, t), chunk(k_ref, t), chunk(v_ref, t), dec_b, True)
        mu = jnp.mean(o, axis=-1, keepdims=True)
        oc = o - mu
        var = jnp.mean(oc * oc, axis=-1, keepdims=True)
        yn = oc * lax.rsqrt(var + NORM_EPS) * gn
        y_ref[sl, :] = (_silu(g_ref[sl, :].astype(F32)) * yn).astype(BF16)
        return carry

    lax.fori_loop(0, n_lat, bwd_body, 0)


def _retention(r, lg_f, lg_b, gn_g, b_sz, cn, s_len, d):
    h_n = RET_HEADS
    dk = d // h_n
    dv = 2 * dk
    c0 = (b_sz * s_len) // cn
    kq, kk, kv, kg = 0, d // dk, (2 * d) // dv, (4 * d) // dv

    def lg_spec():
        return pl.BlockSpec((1, 1, LANES), lambda b, h: (h, 0, 0))

    return pl.pallas_call(
        functools.partial(_ret_kernel, n_ctx=cn // RET_CHUNK, n_lat=s_len // RET_CHUNK),
        grid=(b_sz, h_n),
        in_specs=[
            lg_spec(), lg_spec(),
            pl.BlockSpec((1, dv), lambda b, h: (0, h)),
            pl.BlockSpec((cn, dk), lambda b, h: (c0 + b, kq + h)),
            pl.BlockSpec((cn, dk), lambda b, h: (c0 + b, kk + h)),
            pl.BlockSpec((cn, dv), lambda b, h: (c0 + b, kv + h)),
            pl.BlockSpec((s_len, dk), lambda b, h: (b, kq + h)),
            pl.BlockSpec((s_len, dk), lambda b, h: (b, kk + h)),
            pl.BlockSpec((s_len, dv), lambda b, h: (b, kv + h)),
            pl.BlockSpec((s_len, dv), lambda b, h: (b, kg + h)),
        ],
        out_specs=pl.BlockSpec((s_len, dv), lambda b, h: (b, h)),
        out_shape=jax.ShapeDtypeStruct((b_sz * s_len, h_n * dv), BF16),
        scratch_shapes=[pltpu.VMEM((s_len, dv), F32), pltpu.VMEM((dk, dv), F32)],
        compiler_params=_cparams(("arbitrary", "arbitrary")),
    )(lg_f, lg_b, gn_g, r, r, r, r, r, r, r)


def _rope_angles(s_len, rot_dim):
    rows = s_len // GRID_W
    row = jnp.repeat(jnp.arange(rows, dtype=F32), GRID_W)
    col = (jnp.arange(s_len) % GRID_W).astype(F32)
    n_freq = rot_dim // 4
    inv = ROPE_BASE ** (-jnp.arange(n_freq, dtype=F32) / n_freq)
    ang = jnp.concatenate([row[:, None] * inv, col[:, None] * inv], axis=-1)
    return jnp.cos(ang), jnp.sin(ang)


def _token_table(lat_table, ctx_row, b_sz, n_ctx_rows):
    width = lat_table.shape[1]
    return jnp.concatenate([jnp.tile(lat_table, (b_sz, 1)),
                            jnp.broadcast_to(ctx_row[None, :], (n_ctx_rows, width))], axis=0)


def kernel(x, c, ctx, c_ctx, mod_w, mod_b, norm_mix_g, norm_ffn_g, ffn_w_gate, ffn_w_up, ffn_w_down, ab_w_in, ab_w_out, swa_sink, mla_q_norm_g, mla_w_q_b, mla_kv_norm_g, mla_w_kv_b, ret_w_in, ret_decay_logit_fwd, ret_decay_logit_bwd, ret_gn_g, ret_w_out, final_norm_g):
    b_sz, s_len, d = x.shape
    cn = ctx.shape[1]
    depth = mod_w.shape[0]
    assert depth == 2 and ab_w_in.shape[0] == 1 and ret_w_in.shape[0] == 1
    assert b_sz + 1 <= 8
    n_lat_rows = b_sz * s_len
    n_ctx_rows = b_sz * cn
    nt = n_lat_rows + n_ctx_rows

    tm = min(1024, n_ctx_rows, s_len)
    tm_ffn = min(512, tm)
    tq = min(512, s_len)
    tk = min(1024, s_len)
    assert n_ctx_rows % tm == 0 and s_len % tm == 0 and s_len % cn == 0 and s_len % tq == 0 and s_len % tk == 0

    def make_mrow(t):
        n_lat_tiles = n_lat_rows // t
        per_b = s_len // t
        return lambda i: jnp.where(i < n_lat_tiles, 1 + i // per_b, 0)

    mrow = make_mrow(tm)
    mrow_ffn = make_mrow(tm_ffn)

    xt = jnp.concatenate([x.reshape(n_lat_rows, d), ctx.reshape(n_ctx_rows, d)], axis=0)

    cond8 = jnp.zeros((8, d), F32).at[0].set(c_ctx).at[1:1 + b_sz].set(c)
    mod_all = _modulation(cond8, mod_w, mod_b).reshape(depth, 8, 6, d)

    cos_a, sin_a = _rope_angles(s_len, SWA_HEAD_DIM)
    cos_b, sin_b = _rope_angles(s_len, MLA_ROPE_DIM)
    cos_r, sin_r = _rope_angles(s_len, d // RET_HEADS)
    ones = jnp.ones((LANES,), F32)
    zeros = jnp.zeros((LANES,), F32)
    half = jnp.concatenate([jnp.ones((64,), F32), jnp.zeros((64,), F32)])
    z64 = jnp.zeros((s_len, 64), F32)
    table = functools.partial(_token_table, b_sz=b_sz, n_ctx_rows=n_ctx_rows)
    t_cos_a = table(jnp.concatenate([cos_a, cos_a], axis=1), ones)
    t_sin_a = table(jnp.concatenate([-sin_a, sin_a], axis=1), zeros)
    t_cos_b = table(jnp.concatenate([cos_b, cos_b, z64], axis=1), half)
    t_sin_b = table(jnp.concatenate([-sin_b, sin_b, z64], axis=1), zeros)
    t_cos_r = table(cos_r, ones)
    t_sin_r = table(sin_r, zeros)

    bf = lambda w: w.astype(BF16)
    w_ab_in = jnp.pad(ab_w_in[0], ((0, 0), (0, 2560 - ab_w_in.shape[2]))).astype(BF16)
    wq_b = jnp.pad(mla_w_q_b[0].reshape(MLA_Q_RANK, MLA_HEADS, MLA_NOPE_DIM + MLA_ROPE_DIM),
                   ((0, 0), (0, 0), (0, MLA_HEAD_PAD - MLA_NOPE_DIM - MLA_ROPE_DIM))
                   ).reshape(MLA_Q_RANK, MLA_HEADS * MLA_HEAD_PAD).astype(BF16)
    wkv_b = bf(mla_w_kv_b[0])

    mod0 = mod_all[0]
    a_qkv, lat = _ab_in_proj(xt, norm_mix_g[0:1], mod0, w_ab_in, t_cos_a, t_sin_a, mrow, tm)
    qm, kcat, vmla = _mla_proj(lat, mla_q_norm_g[0:1], mla_kv_norm_g[0:1], wq_b, wkv_b, t_cos_b, t_sin_b, tm)
    oa = _swa_attention(a_qkv, swa_sink[0], b_sz, cn, s_len, tq)
    ob = _mla_attention(qm, kcat, vmla, b_sz, cn, s_len, tq, tk)
    xt = _out_proj([oa, ob], bf(ab_w_out[0]), xt, mod0, mrow, tm, 512)
    xt = _ffn(xt, norm_ffn_g[0:1], mod0, bf(ffn_w_gate[0]), bf(ffn_w_up[0]), bf(ffn_w_down[0]),
              final_norm_g[None, :], mrow_ffn, nt, tm_ffn, 512, final=False)

    mod1 = mod_all[1]
    r = _ret_in_proj(xt, norm_mix_g[1:2], mod1, bf(ret_w_in[0]), t_cos_r, t_sin_r, mrow, tm, 512)
    lg_shape = (RET_HEADS, 1, LANES)
    lg_f = jnp.broadcast_to(ret_decay_logit_fwd[0].astype(F32)[:, None, None], lg_shape)
    lg_b = jnp.broadcast_to(ret_decay_logit_bwd[0].astype(F32)[:, None, None], lg_shape)
    y = _retention(r, lg_f, lg_b, ret_gn_g[0:1], b_sz, cn, s_len, d)
    xt = _out_proj([y], bf(ret_w_out[0]), xt, mod1, mrow, tm, 512)
    out = _ffn(xt, norm_ffn_g[1:2], mod1, bf(ffn_w_gate[1]), bf(ffn_w_up[1]), bf(ffn_w_down[1]),
               final_norm_g[None, :], mrow_ffn, n_lat_rows, tm_ffn, 512, final=True)
    return out.reshape(b_sz, s_len, d)
```

```python
import functools

import jax
import jax.numpy as jnp
from jax import lax
from jax.experimental import pallas as pl
from jax.experimental.pallas import tpu as pltpu

GRID_W = 64
ROPE_BASE = 10000.0
NORM_EPS = 1e-6
NEG_INF = -1e30
LOG2_E = 1.4426950408889634

SWA_HEADS = 8
SWA_KV_HEADS = 2
SWA_HEAD_DIM = 128
SWA_WINDOW = 128
SWA_QSCALE = SWA_HEAD_DIM ** -0.5 * LOG2_E

MLA_HEADS = 8
MLA_Q_RANK = 512
MLA_KV_RANK = 256
MLA_NOPE_DIM = 128
MLA_ROPE_DIM = 64
MLA_V_DIM = 128
MLA_HEAD_PAD = 256
MLA_QSCALE = (MLA_NOPE_DIM + MLA_ROPE_DIM) ** -0.5 * LOG2_E

RET_HEADS = 8
RET_CHUNK = 256

LANES = 128
V7X_VMEM_BYTES = 64 * 1024 * 1024
VMEM_LIMIT = V7X_VMEM_BYTES - 8 * 1024 * 1024

F32 = jnp.float32
BF16 = jnp.bfloat16


def _cparams(sem):
    return pltpu.CompilerParams(dimension_semantics=sem, vmem_limit_bytes=VMEM_LIMIT)


def _dot(a, b):
    return jnp.dot(a, b, preferred_element_type=F32)


def _dot_nt(a, b):
    return lax.dot_general(a, b, (((1,), (1,)), ((), ())), preferred_element_type=F32)


def _dot_tn(a, b):
    return lax.dot_general(a, b, (((0,), (0,)), ((), ())), preferred_element_type=F32)


def _silu(x):
    return x * (1.0 / (1.0 + jnp.exp(-x)))


def _norm_mod(x, g, shift, scale):
    y = x * lax.rsqrt(jnp.mean(x * x, axis=-1, keepdims=True) + NORM_EPS)
    return (y * g) * (1.0 + scale) + shift


def _split_rows(n_lat_tiles):
    lat = lambda i, j: (jnp.minimum(i, n_lat_tiles - 1), 0)
    ctx = lambda i, j: (jnp.maximum(i - n_lat_tiles, 0), 0)
    return lat, ctx


def _mod_kernel(c_ref, w_ref, b_ref, o_ref):
    a = _silu(c_ref[...]).astype(BF16)
    o_ref[0] = _dot(a, w_ref[0].astype(BF16)) + b_ref[0]


def _modulation(cond8, mod_w, mod_b):
    depth, d, n = mod_w.shape
    tn = 1024
    return pl.pallas_call(
        _mod_kernel,
        grid=(depth, n // tn),
        in_specs=[
            pl.BlockSpec((8, d), lambda l, j: (0, 0)),
            pl.BlockSpec((1, d, tn), lambda l, j: (l, 0, j)),
            pl.BlockSpec((1, 1, tn), lambda l, j: (l, 0, j)),
        ],
        out_specs=pl.BlockSpec((1, 8, tn), lambda l, j: (l, 0, j)),
        out_shape=jax.ShapeDtypeStruct((depth, 8, n), F32),
        name="modulation",
        compiler_params=_cparams(("arbitrary", "arbitrary")),
    )(cond8, mod_w, mod_b.reshape(depth, 1, n))


def _rope128(z, cosf, sins):
    return z * cosf + pltpu.roll(z, 64, axis=1) * sins


def _ab_in_kernel(x_ref, c_ref, g_ref, mod_ref, w_ref, cos_ref, sin_ref, a_ref, l_ref, h_ref, *, n_lat_tiles):
    i = pl.program_id(0)
    j = pl.program_id(1)

    def prologue(src_ref):
        h = _norm_mod(src_ref[...], g_ref[...], mod_ref[0, 0, 0:1, :], mod_ref[0, 0, 1:2, :])
        h_ref[...] = h.astype(BF16)

    pl.when((j == 0) & (i < n_lat_tiles))(lambda: prologue(x_ref))
    pl.when((j == 0) & (i >= n_lat_tiles))(lambda: prologue(c_ref))

    z = _dot(h_ref[...], w_ref[...])
    cosf = cos_ref[...]
    sins = sin_ref[...]

    @pl.when(j <= 1)
    def _():
        for k in range(4):
            sl = slice(k * LANES, (k + 1) * LANES)
            a_ref[:, sl] = (_rope128(z[:, sl], cosf, sins) * SWA_QSCALE).astype(BF16)

    @pl.when(j == 2)
    def _():
        for k in range(2):
            sl = slice(k * LANES, (k + 1) * LANES)
            a_ref[:, sl] = _rope128(z[:, sl], cosf, sins).astype(BF16)
        a_ref[:, 2 * LANES:] = z[:, 2 * LANES:].astype(BF16)

    @pl.when(j >= 3)
    def _():
        l_ref[...] = z


def _ab_in_proj(x2d, ctx2d, norm_g, mod_all, layer, w_in, cos_a, sin_a, mrow, tm):
    d = x2d.shape[1]
    n_lat_tiles = x2d.shape[0] // tm
    nt = x2d.shape[0] + ctx2d.shape[0]
    tn = 512
    lat_map, ctx_map = _split_rows(n_lat_tiles)
    return pl.pallas_call(
        functools.partial(_ab_in_kernel, n_lat_tiles=n_lat_tiles),
        grid=(nt // tm, 5),
        in_specs=[
            pl.BlockSpec((tm, d), lat_map),
            pl.BlockSpec((tm, d), ctx_map),
            pl.BlockSpec((1, d), lambda i, j: (0, 0)),
            pl.BlockSpec((1, 1, 6, d), lambda i, j: (layer, mrow(i), 0, 0)),
            pl.BlockSpec((d, tn), lambda i, j: (0, j)),
            pl.BlockSpec((tm, LANES), lambda i, j: (i, 0)),
            pl.BlockSpec((tm, LANES), lambda i, j: (i, 0)),
        ],
        out_specs=[
            pl.BlockSpec((tm, tn), lambda i, j: (i, jnp.minimum(j, 2))),
            pl.BlockSpec((tm, tn), lambda i, j: (i, jnp.maximum(j - 3, 0))),
        ],
        out_shape=[
            jax.ShapeDtypeStruct((nt, 3 * tn), BF16),
            jax.ShapeDtypeStruct((nt, 2 * tn), F32),
        ],
        scratch_shapes=[pltpu.VMEM((tm, d), BF16)],
        name="ab_in_proj",
        compiler_params=_cparams(("arbitrary", "arbitrary")),
    )(x2d, ctx2d, norm_g, mod_all, w_in, cos_a, sin_a)


def _rope64(r, cosp, sinp):
    lane = lax.broadcasted_iota(jnp.int32, r.shape, 1)
    partner = jnp.where(lane < 32, pltpu.roll(r, 96, axis=1), pltpu.roll(r, 32, axis=1))
    return r * cosp + partner * sinp


def _mla_proj_kernel(l_ref, qg_ref, kvg_ref, wq_ref, wkv_ref, cos_ref, sin_ref, qm_ref, kc_ref, vm_ref):
    cosp = cos_ref[...]
    sinp = sin_ref[...]

    def rms(x, g):
        return x * lax.rsqrt(jnp.mean(x * x, axis=-1, keepdims=True) + NORM_EPS) * g

    qn = rms(l_ref[:, :MLA_Q_RANK], qg_ref[...]).astype(BF16)
    qm = _dot(qn, wq_ref[...])
    kvn = rms(l_ref[:, MLA_Q_RANK:MLA_Q_RANK + MLA_KV_RANK], kvg_ref[...]).astype(BF16)
    kv = _dot(kvn, wkv_ref[...])
    kr = l_ref[:, MLA_Q_RANK + MLA_KV_RANK:MLA_Q_RANK + MLA_KV_RANK + LANES]
    krr = _rope64(kr, cosp, sinp).astype(BF16)
    ones = jnp.ones((l_ref.shape[0], LANES), BF16)
    for h in range(MLA_HEADS):
        o = h * MLA_HEAD_PAD
        qm_ref[:, o:o + LANES] = (qm[:, o:o + LANES] * MLA_QSCALE).astype(BF16)
        qm_ref[:, o + LANES:o + 2 * LANES] = (
            _rope64(qm[:, o + LANES:o + 2 * LANES], cosp, sinp) * MLA_QSCALE).astype(BF16)
        kc_ref[:, o:o + LANES] = kv[:, o:o + LANES].astype(BF16)
        kc_ref[:, o + LANES:o + 2 * LANES] = krr
        vm_ref[:, o:o + LANES] = kv[:, o + LANES:o + 2 * LANES].astype(BF16)
        vm_ref[:, o + LANES:o + 2 * LANES] = ones


def _mla_proj(lat, q_norm_g, kv_norm_g, wq, wkv, cos_b, sin_b, tm):
    nt = lat.shape[0]
    hp = MLA_HEADS * MLA_HEAD_PAD
    return pl.pallas_call(
        _mla_proj_kernel,
        grid=(nt // tm,),
        in_specs=[
            pl.BlockSpec((tm, lat.shape[1]), lambda i: (i, 0)),
            pl.BlockSpec((1, MLA_Q_RANK), lambda i: (0, 0)),
            pl.BlockSpec((1, MLA_KV_RANK), lambda i: (0, 0)),
            pl.BlockSpec(wq.shape, lambda i: (0, 0)),
            pl.BlockSpec(wkv.shape, lambda i: (0, 0)),
            pl.BlockSpec((tm, LANES), lambda i: (i, 0)),
            pl.BlockSpec((tm, LANES), lambda i: (i, 0)),
        ],
        out_specs=[
            pl.BlockSpec((tm, hp), lambda i: (i, 0)),
            pl.BlockSpec((tm, hp), lambda i: (i, 0)),
            pl.BlockSpec((tm, hp), lambda i: (i, 0)),
        ],
        out_shape=[
            jax.ShapeDtypeStruct((nt, hp), BF16),
            jax.ShapeDtypeStruct((nt, hp), BF16),
            jax.ShapeDtypeStruct((nt, hp), BF16),
        ],
        name="mla_proj",
        compiler_params=_cparams(("arbitrary",)),
    )(lat, q_norm_g, kv_norm_g, wq, wkv, cos_b, sin_b)


def _swa_kernel(sink_ref, q_ref, *refs, tq, s_len, latent):
    kvh = pl.program_id(1)
    d = SWA_HEAD_DIM
    g_heads = SWA_HEADS // SWA_KV_HEADS
    if latent:
        kp, km, kn, kc, vp, vm, vn, vc, o_ref = refs
        i = pl.program_id(2)
        k_all = jnp.concatenate([kp[...], km[...], kn[...], kc[...]], axis=0)
        v_all = jnp.concatenate([vp[...], vm[...], vn[...], vc[...]], axis=0)
        nb = tq + 2 * SWA_WINDOW
        shape = (tq, k_all.shape[0])
        row = lax.broadcasted_iota(jnp.int32, shape, 0)
        col = lax.broadcasted_iota(jnp.int32, shape, 1)
        rel = col - SWA_WINDOW - row
        kpos = i * tq - SWA_WINDOW + col
        valid = (col >= nb) | ((jnp.abs(rel) <= SWA_WINDOW) & (kpos >= 0) & (kpos < s_len))
    else:
        kc, vc, o_ref = refs
        k_all = kc[...]
        v_all = vc[...]
        valid = None
    v_ext = jnp.concatenate([v_all, jnp.ones_like(v_all)], axis=1)
    for g in range(g_heads):
        sk = sink_ref[kvh * g_heads + g] * LOG2_E
        s = _dot_nt(q_ref[:, g * d:(g + 1) * d], k_all)
        if valid is not None:
            s = jnp.where(valid, s, NEG_INF)
        m = jnp.maximum(jnp.max(s, axis=-1, keepdims=True), sk)
        o_ext = _dot(jnp.exp2(s - m).astype(BF16), v_ext)
        o = o_ext[:, :d] / (o_ext[:, d:] + jnp.exp2(sk - m))
        o_ref[:, g * d:(g + 1) * d] = o.astype(BF16)


def _swa_ctx_kernel(sink_ref, q_ref, kc, vc, prev_ref, o_ref, *, tq, s_len):
    del prev_ref
    _swa_kernel(sink_ref, q_ref, kc, vc, o_ref, tq=tq, s_len=s_len, latent=False)


def _swa_attention(a, sink, b_sz, cn, s_len, tq):
    nt = a.shape[0]
    d = SWA_HEAD_DIM
    w = SWA_WINDOW
    gw = (SWA_HEADS // SWA_KV_HEADS) * d
    kcol = SWA_HEADS
    vcol = SWA_HEADS + SWA_KV_HEADS
    nq = s_len // tq
    r = tq // w
    c0 = (b_sz * s_len) // cn
    out_shape = jax.ShapeDtypeStruct((nt, SWA_HEADS * d), BF16)
    smem = pl.BlockSpec(memory_space=pltpu.SMEM)

    def main(b, h, i):
        return b * nq + i

    def prev(b, h, i):
        return b * (s_len // w) + jnp.maximum(i * r - 1, 0)

    def nxt(b, h, i):
        return b * (s_len // w) + jnp.minimum((i + 1) * r, s_len // w - 1)

    o_lat = pl.pallas_call(
        functools.partial(_swa_kernel, tq=tq, s_len=s_len, latent=True),
        grid=(b_sz, SWA_KV_HEADS, nq),
        in_specs=[
            smem,
            pl.BlockSpec((tq, gw), lambda b, h, i: (main(b, h, i), h)),
            pl.BlockSpec((w, d), lambda b, h, i: (prev(b, h, i), kcol + h)),
            pl.BlockSpec((tq, d), lambda b, h, i: (main(b, h, i), kcol + h)),
            pl.BlockSpec((w, d), lambda b, h, i: (nxt(b, h, i), kcol + h)),
            pl.BlockSpec((cn, d), lambda b, h, i: (c0 + b, kcol + h)),
            pl.BlockSpec((w, d), lambda b, h, i: (prev(b, h, i), vcol + h)),
            pl.BlockSpec((tq, d), lambda b, h, i: (main(b, h, i), vcol + h)),
            pl.BlockSpec((w, d), lambda b, h, i: (nxt(b, h, i), vcol + h)),
            pl.BlockSpec((cn, d), lambda b, h, i: (c0 + b, vcol + h)),
        ],
        out_specs=pl.BlockSpec((tq, gw), lambda b, h, i: (main(b, h, i), h)),
        out_shape=out_shape,
        name="swa_latent",
        compiler_params=_cparams(("arbitrary", "arbitrary", "arbitrary")),
    )(sink, a, a, a, a, a, a, a, a, a)

    return pl.pallas_call(
        functools.partial(_swa_ctx_kernel, tq=cn, s_len=s_len),
        grid=(b_sz, SWA_KV_HEADS),
        in_specs=[
            smem,
            pl.BlockSpec((cn, gw), lambda b, h: (c0 + b, h)),
            pl.BlockSpec((cn, d), lambda b, h: (c0 + b, kcol + h)),
            pl.BlockSpec((cn, d), lambda b, h: (c0 + b, vcol + h)),
            pl.BlockSpec(memory_space=pl.ANY),
        ],
        out_specs=pl.BlockSpec((cn, gw), lambda b, h: (c0 + b, h)),
        out_shape=out_shape,
        input_output_aliases={4: 0},
        name="swa_context",
        compiler_params=_cparams(("arbitrary", "arbitrary")),
    )(sink, a, a, a, o_lat)


def _mla_kernel(q_ref, kc_ref, vc_ref, *refs, tk, latent):
    if latent:
        kl_ref, vl_ref, o_ref = refs
    else:
        (o_ref,) = refs
    dv = MLA_V_DIM
    q = q_ref[...]
    s = _dot_nt(q, kc_ref[...])
    m = jnp.max(s, axis=-1, keepdims=True)
    acc = _dot(jnp.exp2(s - m).astype(BF16), vc_ref[...])
    if latent:
        for j in range(kl_ref.shape[0] // tk):
            sl = slice(j * tk, (j + 1) * tk)
            s = _dot_nt(q, kl_ref[sl, :])
            m_new = jnp.maximum(m, jnp.max(s, axis=-1, keepdims=True))
            acc = jnp.exp2(m - m_new) * acc + _dot(jnp.exp2(s - m_new).astype(BF16), vl_ref[sl, :])
            m = m_new
    o_ref[...] = (acc[:, :dv] / acc[:, dv:]).astype(BF16)


def _mla_ctx_kernel(q_ref, kc_ref, vc_ref, prev_ref, o_ref):
    del prev_ref
    _mla_kernel(q_ref, kc_ref, vc_ref, o_ref, tk=0, latent=False)


def _mla_attention(qm, kcat, vext, b_sz, cn, s_len, tq, tk):
    nt = qm.shape[0]
    hp = MLA_HEAD_PAD
    dv = MLA_V_DIM
    nq = s_len // tq
    c0 = (b_sz * s_len) // cn
    out_shape = jax.ShapeDtypeStruct((nt, MLA_HEADS * dv), BF16)

    o_lat = pl.pallas_call(
        functools.partial(_mla_kernel, tk=tk, latent=True),
        grid=(b_sz, MLA_HEADS, nq),
        in_specs=[
            pl.BlockSpec((tq, hp), lambda b, h, i: (b * nq + i, h)),
            pl.BlockSpec((cn, hp), lambda b, h, i: (c0 + b, h)),
            pl.BlockSpec((cn, hp), lambda b, h, i: (c0 + b, h)),
            pl.BlockSpec((s_len, hp), lambda b, h, i: (b, h)),
            pl.BlockSpec((s_len, hp), lambda b, h, i: (b, h)),
        ],
        out_specs=pl.BlockSpec((tq, dv), lambda b, h, i: (b * nq + i, h)),
        out_shape=out_shape,
        name="mla_latent",
        compiler_params=_cparams(("arbitrary",) * 3),
    )(qm, kcat, vext, kcat, vext)

    return pl.pallas_call(
        _mla_ctx_kernel,
        grid=(b_sz, MLA_HEADS),
        in_specs=[
            pl.BlockSpec((cn, hp), lambda b, h: (c0 + b, h)),
            pl.BlockSpec((cn, hp), lambda b, h: (c0 + b, h)),
            pl.BlockSpec((cn, hp), lambda b, h: (c0 + b, h)),
            pl.BlockSpec(memory_space=pl.ANY),
        ],
        out_specs=pl.BlockSpec((cn, dv), lambda b, h: (c0 + b, h)),
        out_shape=out_shape,
        input_output_aliases={3: 0},
        name="mla_context",
        compiler_params=_cparams(("arbitrary", "arbitrary")),
    )(qm, kcat, vext, o_lat)


def _out_proj_kernel(*refs, n_parts, n_lat_tiles):
    parts = refs[:n_parts]
    w_ref, mod_ref, x_ref = refs[n_parts:n_parts + 3]
    o_ref = refs[-1]
    acc = None
    k0 = 0
    for p in parts:
        kw = p.shape[1]
        t = _dot(p[...], w_ref[k0:k0 + kw, :])
        acc = t if acc is None else acc + t
        k0 += kw
    upd = mod_ref[0, 0, 2:3, :] * acc
    if n_lat_tiles is None:
        o_ref[...] = x_ref[...] + upd
    else:
        c_ref = refs[n_parts + 3]
        i = pl.program_id(0)

        @pl.when(i < n_lat_tiles)
        def _():
            o_ref[...] = x_ref[...] + upd

        @pl.when(i >= n_lat_tiles)
        def _():
            o_ref[...] = c_ref[...] + upd


def _out_proj(parts, w, x2d, ctx2d, mod_all, layer, mrow, tm, tn):
    rows = parts[0].shape[0]
    d = x2d.shape[1]
    kdim = w.shape[0]
    part_specs = [pl.BlockSpec((tm, p.shape[1]), lambda i, j: (i, 0)) for p in parts]
    if ctx2d is None:
        n_lat_tiles = None
        res_specs = [pl.BlockSpec((tm, tn), lambda i, j: (i, j))]
        res = [x2d]
    else:
        n_lat_tiles = x2d.shape[0] // tm
        is_lat = lambda i: i < n_lat_tiles
        res_specs = [
            pl.BlockSpec((tm, tn), lambda i, j: (jnp.minimum(i, n_lat_tiles - 1), jnp.where(is_lat(i), j, 0))),
            pl.BlockSpec((tm, tn), lambda i, j: (jnp.maximum(i - n_lat_tiles, 0), jnp.where(is_lat(i), 0, j))),
        ]
        res = [x2d, ctx2d]
    return pl.pallas_call(
        functools.partial(_out_proj_kernel, n_parts=len(parts), n_lat_tiles=n_lat_tiles),
        grid=(rows // tm, d // tn),
        in_specs=part_specs + [
            pl.BlockSpec((kdim, tn), lambda i, j: (0, j)),
            pl.BlockSpec((1, 1, 6, tn), lambda i, j: (layer, mrow(i), 0, j)),
        ] + res_specs,
        out_specs=pl.BlockSpec((tm, tn), lambda i, j: (i, j)),
        out_shape=jax.ShapeDtypeStruct((rows, d), F32),
        name="out_proj",
        compiler_params=_cparams(("arbitrary", "arbitrary")),
    )(*parts, w, mod_all, *res)


def _ffn_kernel(x_ref, g_ref, mod_ref, wg_ref, wu_ref, wd_ref, fg_ref, o_ref, h_ref, acc_ref, *, nf, final):
    j = pl.program_id(1)

    @pl.when(j == 0)
    def _():
        h = _norm_mod(x_ref[...], g_ref[...], mod_ref[0, 0, 3:4, :], mod_ref[0, 0, 4:5, :])
        h_ref[...] = h.astype(BF16)

    h = h_ref[...]
    a = (_silu(_dot(h, wg_ref[0])) * _dot(h, wu_ref[0])).astype(BF16)
    part = _dot(a, wd_ref[0])

    @pl.when(j == 0)
    def _():
        acc_ref[...] = part

    @pl.when(j > 0)
    def _():
        acc_ref[...] += part

    @pl.when(j == nf - 1)
    def _():
        y = x_ref[...] + mod_ref[0, 0, 5:6, :] * acc_ref[...]
        if final:
            y = y * lax.rsqrt(jnp.mean(y * y, axis=-1, keepdims=True) + NORM_EPS) * fg_ref[...]
        o_ref[...] = y


def _ffn(xt, norm_g, mod_all, layer, wg, wu, wd, final_g, mrow, n_rows, tm, tf, final):
    d = xt.shape[1]
    f = wg.shape[2]
    nf = f // tf
    return pl.pallas_call(
        functools.partial(_ffn_kernel, nf=nf, final=final),
        grid=(n_rows // tm, nf),
        in_specs=[
            pl.BlockSpec((tm, d), lambda i, j: (i, 0)),
            pl.BlockSpec((1, d), lambda i, j: (0, 0)),
            pl.BlockSpec((1, 1, 6, d), lambda i, j: (layer, mrow(i), 0, 0)),
            pl.BlockSpec((1, d, tf), lambda i, j: (layer, 0, j)),
            pl.BlockSpec((1, d, tf), lambda i, j: (layer, 0, j)),
            pl.BlockSpec((1, tf, d), lambda i, j: (layer, j, 0)),
            pl.BlockSpec((1, d), lambda i, j: (0, 0)),
        ],
        out_specs=pl.BlockSpec((tm, d), lambda i, j: (i, 0)),
        out_shape=jax.ShapeDtypeStruct((n_rows, d), F32),
        scratch_shapes=[pltpu.VMEM((tm, d), BF16), pltpu.VMEM((tm, d), F32)],
        name="ffn_final" if final else "ffn",
        compiler_params=_cparams(("arbitrary", "arbitrary")),
    )(xt, norm_g, mod_all, wg, wu, wd, final_g)


def _ret_in_kernel(x_ref, g_ref, mod_ref, w_ref, cos_ref, sin_ref, o_ref, h_ref, *, n_qk, k_scale, tn):
    j = pl.program_id(1)

    @pl.when(j == 0)
    def _():
        h = _norm_mod(x_ref[...], g_ref[...], mod_ref[0, 0, 0:1, :], mod_ref[0, 0, 1:2, :])
        h_ref[...] = h.astype(BF16)

    z = _dot(h_ref[...], w_ref[...])

    def rope(scale):
        c = cos_ref[...]
        s = sin_ref[...]
        for k in range(tn // (2 * LANES)):
            o = 2 * k * LANES
            x1 = z[:, o:o + LANES]
            x2 = z[:, o + LANES:o + 2 * LANES]
            o_ref[:, o:o + LANES] = ((x1 * c - x2 * s) * scale).astype(BF16)
            o_ref[:, o + LANES:o + 2 * LANES] = ((x1 * s + x2 * c) * scale).astype(BF16)

    @pl.when(j < n_qk)
    def _():
        rope(1.0)

    @pl.when((j >= n_qk) & (j < 2 * n_qk))
    def _():
        rope(k_scale)

    @pl.when(j >= 2 * n_qk)
    def _():
        o_ref[...] = z.astype(BF16)


def _ret_in_proj(xt, norm_g, mod_all, layer, w_in, cos_r, sin_r, mrow, tm, tn):
    nt, d = xt.shape
    n = w_in.shape[1]
    dk = d // RET_HEADS
    return pl.pallas_call(
        functools.partial(_ret_in_kernel, n_qk=d // tn, k_scale=dk ** -0.5, tn=tn),
        grid=(nt // tm, n // tn),
        in_specs=[
            pl.BlockSpec((tm, d), lambda i, j: (i, 0)),
            pl.BlockSpec((1, d), lambda i, j: (0, 0)),
            pl.BlockSpec((1, 1, 6, d), lambda i, j: (layer, mrow(i), 0, 0)),
            pl.BlockSpec((d, tn), lambda i, j: (0, j)),
            pl.BlockSpec((tm, LANES), lambda i, j: (i, 0)),
            pl.BlockSpec((tm, LANES), lambda i, j: (i, 0)),
        ],
        out_specs=pl.BlockSpec((tm, tn), lambda i, j: (i, j)),
        out_shape=jax.ShapeDtypeStruct((nt, n), BF16),
        scratch_shapes=[pltpu.VMEM((tm, d), BF16)],
        name="ret_in_proj",
        compiler_params=_cparams(("arbitrary", "arbitrary")),
    )(xt, norm_g, mod_all, w_in, cos_r, sin_r)


def _log_sigmoid(x):
    return jnp.minimum(x, 0.0) - jnp.log1p(jnp.exp(-jnp.abs(x)))


def _ret_kernel(lf_ref, lb_ref, gn_ref, qc_ref, kc_ref, vc_ref, q_ref, k_ref, v_ref, g_ref, y_ref,
                o_ref, s_ref, dint_ref, qdec_ref, kdec_ref, *, n_ctx, n_lat):
    c = RET_CHUNK
    dk = q_ref.shape[1]
    dv = v_ref.shape[1]
    row = lax.broadcasted_iota(jnp.int32, (c, c), 0)
    col = lax.broadcasted_iota(jnp.int32, (c, c), 1)
    pos = lax.broadcasted_iota(jnp.int32, (c, LANES), 0).astype(F32)
    c_dec = []
    for drn, l_ref in enumerate((lf_ref, lb_ref)):
        lg = _log_sigmoid(l_ref[0])
        lg_c = jnp.broadcast_to(lg[:, 0:1], (1, c))
        diff = (col - row) if drn else (row - col)
        dint_ref[drn] = jnp.where(diff >= 0, jnp.exp(lg_c * jnp.maximum(diff, 0).astype(F32)), 0.0)
        if drn:
            qdec_ref[drn] = jnp.exp(lg * (c - pos))
            kdec_ref[drn] = jnp.exp(lg * pos)
        else:
            qdec_ref[drn] = jnp.exp(lg * (pos + 1.0))
            kdec_ref[drn] = jnp.exp(lg * (c - 1.0 - pos))
        c_dec.append(jnp.exp(jnp.broadcast_to(lg[:, 0:1], (1, dv)) * c))

    def scale_rows(x, dec):
        return jnp.concatenate([x[:, k * LANES:(k + 1) * LANES].astype(F32) * dec for k in range(dk // LANES)],
                               axis=1).astype(BF16)

    def step(drn, q, k, v, want_out):
        s = s_ref[drn]
        o = None
        if want_out:
            a = (_dot_nt(q, k) * dint_ref[drn]).astype(BF16)
            o = _dot(a, v) + _dot(scale_rows(q, qdec_ref[drn]), s.astype(BF16))
        s_ref[drn] = s * c_dec[drn] + _dot_tn(scale_rows(k, kdec_ref[drn]), v)
        return o

    def rows(t):
        return pl.ds(pl.multiple_of(t * c, c), c)

    def lat_step(drn, t):
        sl = rows(t)
        return step(drn, q_ref[sl, :], k_ref[sl, :], v_ref[sl, :], True)

    def finish(t, o):
        sl = rows(t)
        mu = jnp.mean(o, axis=-1, keepdims=True)
        oc = o - mu
        var = jnp.mean(oc * oc, axis=-1, keepdims=True)
        yn = oc * lax.rsqrt(var + NORM_EPS) * gn_ref[...]
        y_ref[sl, :] = (_silu(g_ref[sl, :].astype(F32)) * yn).astype(BF16)

    s_ref[...] = jnp.zeros_like(s_ref)
    for t in range(n_ctx):
        for drn, tt in ((0, t), (1, n_ctx - 1 - t)):
            sl = slice(tt * c, (tt + 1) * c)
            step(drn, qc_ref[sl, :], kc_ref[sl, :], vc_ref[sl, :], False)

    half = n_lat // 2

    def first_half(t, carry):
        tb = n_lat - 1 - t
        o_ref[rows(t), :] = lat_step(0, t)
        o_ref[rows(tb), :] = lat_step(1, tb)
        return carry

    def second_half(t, carry):
        tb = n_lat - 1 - t
        finish(t, o_ref[rows(t), :] + lat_step(0, t))
        finish(tb, o_ref[rows(tb), :] + lat_step(1, tb))
        return carry

    lax.fori_loop(0, half, first_half, 0)
    lax.fori_loop(half, n_lat, second_half, 0)


def _retention(r, lg_f, lg_b, gn_g, b_sz, cn, s_len, d):
    h_n = RET_HEADS
    dk = d // h_n
    dv = 2 * dk
    c = RET_CHUNK
    assert cn % c == 0 and s_len % (2 * c) == 0
    c0 = (b_sz * s_len) // cn
    kq, kk, kv, kg = 0, d // dk, (2 * d) // dv, (4 * d) // dv

    def lg_spec():
        return pl.BlockSpec((1, 1, LANES), lambda b, h: (h, 0, 0))

    return pl.pallas_call(
        functools.partial(_ret_kernel, n_ctx=cn // c, n_lat=s_len // c),
        grid=(b_sz, h_n),
        in_specs=[
            lg_spec(), lg_spec(),
            pl.BlockSpec((1, dv), lambda b, h: (0, h)),
            pl.BlockSpec((cn, dk), lambda b, h: (c0 + b, kq + h)),
            pl.BlockSpec((cn, dk), lambda b, h: (c0 + b, kk + h)),
            pl.BlockSpec((cn, dv), lambda b, h: (c0 + b, kv + h)),
            pl.BlockSpec((s_len, dk), lambda b, h: (b, kq + h)),
            pl.BlockSpec((s_len, dk), lambda b, h: (b, kk + h)),
            pl.BlockSpec((s_len, dv), lambda b, h: (b, kv + h)),
            pl.BlockSpec((s_len, dv), lambda b, h: (b, kg + h)),
        ],
        out_specs=pl.BlockSpec((s_len, dv), lambda b, h: (b, h)),
        out_shape=jax.ShapeDtypeStruct((b_sz * s_len, h_n * dv), BF16),
        scratch_shapes=[
            pltpu.VMEM((s_len, dv), F32),
            pltpu.VMEM((2, dk, dv), F32),
            pltpu.VMEM((2, c, c), F32),
            pltpu.VMEM((2, c, LANES), F32),
            pltpu.VMEM((2, c, LANES), F32),
        ],
        name="retention",
        compiler_params=_cparams(("arbitrary", "arbitrary")),
    )(lg_f, lg_b, gn_g, r, r, r, r, r, r, r)


def _rope_angles(s_len, rot_dim):
    rows = s_len // GRID_W
    row = jnp.repeat(jnp.arange(rows, dtype=F32), GRID_W)
    col = (jnp.arange(s_len) % GRID_W).astype(F32)
    n_freq = rot_dim // 4
    inv = ROPE_BASE ** (-jnp.arange(n_freq, dtype=F32) / n_freq)
    ang = jnp.concatenate([row[:, None] * inv, col[:, None] * inv], axis=-1)
    return jnp.cos(ang), jnp.sin(ang)


def _token_table(lat_table, ctx_row, b_sz, n_ctx_rows):
    width = lat_table.shape[1]
    return jnp.concatenate([jnp.tile(lat_table, (b_sz, 1)),
                            jnp.broadcast_to(ctx_row[None, :], (n_ctx_rows, width))], axis=0)


def kernel(x, c, ctx, c_ctx, mod_w, mod_b, norm_mix_g, norm_ffn_g, ffn_w_gate, ffn_w_up, ffn_w_down, ab_w_in, ab_w_out, swa_sink, mla_q_norm_g, mla_w_q_b, mla_kv_norm_g, mla_w_kv_b, ret_w_in, ret_decay_logit_fwd, ret_decay_logit_bwd, ret_gn_g, ret_w_out, final_norm_g):
    b_sz, s_len, d = x.shape
    cn = ctx.shape[1]
    depth = mod_w.shape[0]
    assert depth == 2 and ab_w_in.shape[0] == 1 and ret_w_in.shape[0] == 1
    assert b_sz + 1 <= 8
    n_lat_rows = b_sz * s_len
    n_ctx_rows = b_sz * cn
    nt = n_lat_rows + n_ctx_rows

    tm = min(1024, n_ctx_rows, s_len)
    tm_ffn = min(512, tm)
    tq = min(512, s_len)
    tk = min(1024, s_len)
    assert n_ctx_rows % tm == 0 and s_len % tm == 0 and s_len % cn == 0 and s_len % tq == 0 and s_len % tk == 0

    def make_mrow(t):
        n_lat_tiles = n_lat_rows // t
        per_b = s_len // t
        return lambda i: jnp.where(i < n_lat_tiles, 1 + i // per_b, 0)

    mrow = make_mrow(tm)
    mrow_ffn = make_mrow(tm_ffn)

    x2d = x.reshape(n_lat_rows, d)
    ctx2d = ctx.reshape(n_ctx_rows, d)

    cond8 = jnp.zeros((8, d), F32).at[0].set(c_ctx).at[1:1 + b_sz].set(c)
    mod_all = _modulation(cond8, mod_w, mod_b).reshape(depth, 8, 6, d)

    cos_a, sin_a = _rope_angles(s_len, SWA_HEAD_DIM)
    cos_b, sin_b = _rope_angles(s_len, MLA_ROPE_DIM)
    cos_r, sin_r = _rope_angles(s_len, d // RET_HEADS)
    ones = jnp.ones((LANES,), F32)
    zeros = jnp.zeros((LANES,), F32)
    half = jnp.concatenate([jnp.ones((64,), F32), jnp.zeros((64,), F32)])
    z64 = jnp.zeros((s_len, 64), F32)
    table = functools.partial(_token_table, b_sz=b_sz, n_ctx_rows=n_ctx_rows)
    t_cos_a = table(jnp.concatenate([cos_a, cos_a], axis=1), ones)
    t_sin_a = table(jnp.concatenate([-sin_a, sin_a], axis=1), zeros)
    t_cos_b = table(jnp.concatenate([cos_b, cos_b, z64], axis=1), half)
    t_sin_b = table(jnp.concatenate([-sin_b, sin_b, z64], axis=1), zeros)
    t_cos_r = table(cos_r, ones)
    t_sin_r = table(sin_r, zeros)

    bf = lambda w: w.astype(BF16)
    w_ab_in = jnp.pad(ab_w_in[0], ((0, 0), (0, 2560 - ab_w_in.shape[2]))).astype(BF16)
    wq_b = jnp.pad(mla_w_q_b[0].reshape(MLA_Q_RANK, MLA_HEADS, MLA_NOPE_DIM + MLA_ROPE_DIM),
                   ((0, 0), (0, 0), (0, MLA_HEAD_PAD - MLA_NOPE_DIM - MLA_ROPE_DIM))
                   ).reshape(MLA_Q_RANK, MLA_HEADS * MLA_HEAD_PAD).astype(BF16)
    wkv_b = bf(mla_w_kv_b[0])
    wg, wu, wd = bf(ffn_w_gate), bf(ffn_w_up), bf(ffn_w_down)
    final_g = final_norm_g[None, :]

    a_qkv, lat = _ab_in_proj(x2d, ctx2d, norm_mix_g[0:1], mod_all, 0, w_ab_in, t_cos_a, t_sin_a, mrow_ffn, tm_ffn)
    qm, kcat, vext = _mla_proj(lat, mla_q_norm_g[0:1], mla_kv_norm_g[0:1], wq_b, wkv_b, t_cos_b, t_sin_b, tm)
    oa = _swa_attention(a_qkv, swa_sink[0], b_sz, cn, s_len, tq)
    ob = _mla_attention(qm, kcat, vext, b_sz, cn, s_len, tq, tk)
    xt = _out_proj([oa, ob], bf(ab_w_out[0]), x2d, ctx2d, mod_all, 0, mrow, tm, 512)
    xt = _ffn(xt, norm_ffn_g[0:1], mod_all, 0, wg, wu, wd, final_g, mrow_ffn, nt, tm_ffn, 512, final=False)

    r = _ret_in_proj(xt, norm_mix_g[1:2], mod_all, 1, bf(ret_w_in[0]), t_cos_r, t_sin_r, mrow, tm, 512)
    lg_shape = (RET_HEADS, 1, LANES)
    lg_f = jnp.broadcast_to(ret_decay_logit_fwd[0].astype(F32)[:, None, None], lg_shape)
    lg_b = jnp.broadcast_to(ret_decay_logit_bwd[0].astype(F32)[:, None, None], lg_shape)
    y = _retention(r, lg_f, lg_b, ret_gn_g[0:1], b_sz, cn, s_len, d)
    xl = _out_proj([y], bf(ret_w_out[0]), xt, None, mod_all, 1, mrow, tm, 512)
    out = _ffn(xl, norm_ffn_g[1:2], mod_all, 1, wg, wu, wd, final_g, mrow_ffn, n_lat_rows, tm_ffn, 512, final=True)
    return out.reshape(b_sz, s_len, d)
```

```python
import functools

import jax
import jax.numpy as jnp
from jax import lax
from jax.experimental import pallas as pl
from jax.experimental.pallas import tpu as pltpu

GRID_W = 64
ROPE_BASE = 10000.0
NORM_EPS = 1e-6
NEG_INF = -1e30
LOG2_E = 1.4426950408889634

SWA_HEADS = 8
SWA_KV_HEADS = 2
SWA_HEAD_DIM = 128
SWA_WINDOW = 128
SWA_QSCALE = SWA_HEAD_DIM ** -0.5 * LOG2_E

MLA_HEADS = 8
MLA_Q_RANK = 512
MLA_KV_RANK = 256
MLA_NOPE_DIM = 128
MLA_ROPE_DIM = 64
MLA_V_DIM = 128
MLA_HEAD_PAD = 256
MLA_QSCALE = (MLA_NOPE_DIM + MLA_ROPE_DIM) ** -0.5 * LOG2_E

RET_HEADS = 8
RET_CHUNK = 256

LANES = 128
V7X_VMEM_BYTES = 64 * 1024 * 1024
VMEM_LIMIT = V7X_VMEM_BYTES - 8 * 1024 * 1024

F32 = jnp.float32
BF16 = jnp.bfloat16


def _cparams(sem):
    return pltpu.CompilerParams(dimension_semantics=sem, vmem_limit_bytes=VMEM_LIMIT)


def _dot(a, b):
    return jnp.dot(a, b, preferred_element_type=F32)


def _dot_nt(a, b):
    return lax.dot_general(a, b, (((1,), (1,)), ((), ())), preferred_element_type=F32)


def _dot_tn(a, b):
    return lax.dot_general(a, b, (((0,), (0,)), ((), ())), preferred_element_type=F32)


def _silu(x):
    return x * (1.0 / (1.0 + jnp.exp(-x)))


NORM_ROWS = 32
SUB_N = 256


def _norm_mod_into(h_ref, x_ref, g_ref, mod_ref, shift_row):
    shift = mod_ref[0, 0, shift_row:shift_row + 1, :]
    gain = g_ref[...] * (1.0 + mod_ref[0, 0, shift_row + 1:shift_row + 2, :])

    def body(t, carry):
        sl = pl.ds(pl.multiple_of(t * NORM_ROWS, NORM_ROWS), NORM_ROWS)
        x = x_ref[sl, :]
        inv = lax.rsqrt(jnp.mean(x * x, axis=-1, keepdims=True) + NORM_EPS)
        h_ref[sl, :] = ((x * inv) * gain + shift).astype(BF16)
        return carry

    lax.fori_loop(0, x_ref.shape[0] // NORM_ROWS, body, 0, unroll=4)


def _split_rows(n_lat_tiles):
    lat = lambda i, j: (jnp.minimum(i, n_lat_tiles - 1), 0)
    ctx = lambda i, j: (jnp.maximum(i - n_lat_tiles, 0), 0)
    return lat, ctx


def _mod_kernel(c_ref, w_ref, b_ref, o_ref):
    a = _silu(c_ref[...]).astype(BF16)
    o_ref[0] = _dot(a, w_ref[0].astype(BF16)) + b_ref[0]


def _modulation(cond8, mod_w, mod_b):
    depth, d, n = mod_w.shape
    tn = 1024
    return pl.pallas_call(
        _mod_kernel,
        grid=(depth, n // tn),
        in_specs=[
            pl.BlockSpec((8, d), lambda l, j: (0, 0)),
            pl.BlockSpec((1, d, tn), lambda l, j: (l, 0, j)),
            pl.BlockSpec((1, 1, tn), lambda l, j: (l, 0, j)),
        ],
        out_specs=pl.BlockSpec((1, 8, tn), lambda l, j: (l, 0, j)),
        out_shape=jax.ShapeDtypeStruct((depth, 8, n), F32),
        name="modulation",
        compiler_params=_cparams(("arbitrary", "arbitrary")),
    )(cond8, mod_w, mod_b.reshape(depth, 1, n))


def _rope128(z, cosf, sins):
    return z * cosf + pltpu.roll(z, 64, axis=1) * sins


def _ab_in_kernel(x_ref, c_ref, g_ref, mod_ref, w_ref, cos_ref, sin_ref, a_ref, l_ref, h_ref, *, n_lat_tiles):
    i = pl.program_id(0)
    j = pl.program_id(1)

    pl.when((j == 0) & (i < n_lat_tiles))(lambda: _norm_mod_into(h_ref, x_ref, g_ref, mod_ref, 0))
    pl.when((j == 0) & (i >= n_lat_tiles))(lambda: _norm_mod_into(h_ref, c_ref, g_ref, mod_ref, 0))

    def block(kinds):
        h = h_ref[...]
        for half, kind in enumerate(kinds):
            z = _dot(h, w_ref[:, half * SUB_N:(half + 1) * SUB_N])
            for k in range(SUB_N // LANES):
                sl = slice(half * SUB_N + k * LANES, half * SUB_N + (k + 1) * LANES)
                zk = z[:, k * LANES:(k + 1) * LANES]
                if kind == "q":
                    a_ref[:, sl] = (_rope128(zk, cos_ref[...], sin_ref[...]) * SWA_QSCALE).astype(BF16)
                elif kind == "k":
                    a_ref[:, sl] = _rope128(zk, cos_ref[...], sin_ref[...]).astype(BF16)
                elif kind == "v":
                    a_ref[:, sl] = zk.astype(BF16)
                else:
                    l_ref[:, sl] = zk

    pl.when(j <= 1)(lambda: block(("q", "q")))
    pl.when(j == 2)(lambda: block(("k", "v")))
    pl.when(j >= 3)(lambda: block(("f32", "f32")))


def _ab_in_proj(x2d, ctx2d, norm_g, mod_all, layer, w_in, cos_a, sin_a, mrow, tm):
    d = x2d.shape[1]
    n_lat_tiles = x2d.shape[0] // tm
    nt = x2d.shape[0] + ctx2d.shape[0]
    tn = 512
    lat_map, ctx_map = _split_rows(n_lat_tiles)
    return pl.pallas_call(
        functools.partial(_ab_in_kernel, n_lat_tiles=n_lat_tiles),
        grid=(nt // tm, 5),
        in_specs=[
            pl.BlockSpec((tm, d), lat_map),
            pl.BlockSpec((tm, d), ctx_map),
            pl.BlockSpec((1, d), lambda i, j: (0, 0)),
            pl.BlockSpec((1, 1, 6, d), lambda i, j: (layer, mrow(i), 0, 0)),
            pl.BlockSpec((d, tn), lambda i, j: (0, j)),
            pl.BlockSpec((tm, LANES), lambda i, j: (i, 0)),
            pl.BlockSpec((tm, LANES), lambda i, j: (i, 0)),
        ],
        out_specs=[
            pl.BlockSpec((tm, tn), lambda i, j: (i, jnp.minimum(j, 2))),
            pl.BlockSpec((tm, tn), lambda i, j: (i, jnp.maximum(j - 3, 0))),
        ],
        out_shape=[
            jax.ShapeDtypeStruct((nt, 3 * tn), BF16),
            jax.ShapeDtypeStruct((nt, 2 * tn), F32),
        ],
        scratch_shapes=[pltpu.VMEM((tm, d), BF16)],
        name="ab_in_proj",
        compiler_params=_cparams(("arbitrary", "arbitrary")),
    )(x2d, ctx2d, norm_g, mod_all, w_in, cos_a, sin_a)


def _rope64(r, cosp, sinp):
    lane = lax.broadcasted_iota(jnp.int32, r.shape, 1)
    partner = jnp.where(lane < 32, pltpu.roll(r, 96, axis=1), pltpu.roll(r, 32, axis=1))
    return r * cosp + partner * sinp


def _mla_proj_kernel(l_ref, qg_ref, kvg_ref, wq_ref, wkv_ref, cos_ref, sin_ref, qm_ref, kc_ref, vm_ref):
    cosp = cos_ref[...]
    sinp = sin_ref[...]

    def rms(x, g):
        return x * lax.rsqrt(jnp.mean(x * x, axis=-1, keepdims=True) + NORM_EPS) * g

    qn = rms(l_ref[:, :MLA_Q_RANK], qg_ref[...]).astype(BF16)
    qm = _dot(qn, wq_ref[...])
    kvn = rms(l_ref[:, MLA_Q_RANK:MLA_Q_RANK + MLA_KV_RANK], kvg_ref[...]).astype(BF16)
    kv = _dot(kvn, wkv_ref[...])
    kr = l_ref[:, MLA_Q_RANK + MLA_KV_RANK:MLA_Q_RANK + MLA_KV_RANK + LANES]
    krr = _rope64(kr, cosp, sinp).astype(BF16)
    ones = jnp.ones((l_ref.shape[0], LANES), BF16)
    for h in range(MLA_HEADS):
        o = h * MLA_HEAD_PAD
        qm_ref[:, o:o + LANES] = (qm[:, o:o + LANES] * MLA_QSCALE).astype(BF16)
        qm_ref[:, o + LANES:o + 2 * LANES] = (
            _rope64(qm[:, o + LANES:o + 2 * LANES], cosp, sinp) * MLA_QSCALE).astype(BF16)
        kc_ref[:, o:o + LANES] = kv[:, o:o + LANES].astype(BF16)
        kc_ref[:, o + LANES:o + 2 * LANES] = krr
        vm_ref[:, o:o + LANES] = kv[:, o + LANES:o + 2 * LANES].astype(BF16)
        vm_ref[:, o + LANES:o + 2 * LANES] = ones


def _mla_proj(lat, q_norm_g, kv_norm_g, wq, wkv, cos_b, sin_b, tm):
    nt = lat.shape[0]
    hp = MLA_HEADS * MLA_HEAD_PAD
    return pl.pallas_call(
        _mla_proj_kernel,
        grid=(nt // tm,),
        in_specs=[
            pl.BlockSpec((tm, lat.shape[1]), lambda i: (i, 0)),
            pl.BlockSpec((1, MLA_Q_RANK), lambda i: (0, 0)),
            pl.BlockSpec((1, MLA_KV_RANK), lambda i: (0, 0)),
            pl.BlockSpec(wq.shape, lambda i: (0, 0)),
            pl.BlockSpec(wkv.shape, lambda i: (0, 0)),
            pl.BlockSpec((tm, LANES), lambda i: (i, 0)),
            pl.BlockSpec((tm, LANES), lambda i: (i, 0)),
        ],
        out_specs=[
            pl.BlockSpec((tm, hp), lambda i: (i, 0)),
            pl.BlockSpec((tm, hp), lambda i: (i, 0)),
            pl.BlockSpec((tm, hp), lambda i: (i, 0)),
        ],
        out_shape=[
            jax.ShapeDtypeStruct((nt, hp), BF16),
            jax.ShapeDtypeStruct((nt, hp), BF16),
            jax.ShapeDtypeStruct((nt, hp), BF16),
        ],
        name="mla_proj",
        compiler_params=_cparams(("arbitrary",)),
    )(lat, q_norm_g, kv_norm_g, wq, wkv, cos_b, sin_b)


def _swa_kernel(sink_ref, q_ref, *refs, tq, s_len, latent):
    kvh = pl.program_id(1)
    d = SWA_HEAD_DIM
    g_heads = SWA_HEADS // SWA_KV_HEADS
    if latent:
        kp, km, kn, kc, vp, vm, vn, vc, o_ref = refs
        i = pl.program_id(2)
        k_all = jnp.concatenate([kp[...], km[...], kn[...], kc[...]], axis=0)
        v_all = jnp.concatenate([vp[...], vm[...], vn[...], vc[...]], axis=0)
        nb = tq + 2 * SWA_WINDOW
        shape = (tq, k_all.shape[0])
        row = lax.broadcasted_iota(jnp.int32, shape, 0)
        col = lax.broadcasted_iota(jnp.int32, shape, 1)
        rel = col - SWA_WINDOW - row
        kpos = i * tq - SWA_WINDOW + col
        valid = (col >= nb) | ((jnp.abs(rel) <= SWA_WINDOW) & (kpos >= 0) & (kpos < s_len))
    else:
        kc, vc, o_ref = refs
        k_all = kc[...]
        v_all = vc[...]
        valid = None
    v_ext = jnp.concatenate([v_all, jnp.ones_like(v_all)], axis=1)
    for g in range(g_heads):
        sk = sink_ref[kvh * g_heads + g] * LOG2_E
        s = _dot_nt(q_ref[:, g * d:(g + 1) * d], k_all)
        if valid is not None:
            s = jnp.where(valid, s, NEG_INF)
        m = jnp.maximum(jnp.max(s, axis=-1, keepdims=True), sk)
        o_ext = _dot(jnp.exp2(s - m).astype(BF16), v_ext)
        o = o_ext[:, :d] / (o_ext[:, d:] + jnp.exp2(sk - m))
        o_ref[:, g * d:(g + 1) * d] = o.astype(BF16)


def _swa_attention(a, sink, b_sz, cn, s_len, tq):
    d = SWA_HEAD_DIM
    w = SWA_WINDOW
    gw = (SWA_HEADS // SWA_KV_HEADS) * d
    kcol = SWA_HEADS
    vcol = SWA_HEADS + SWA_KV_HEADS
    nq = s_len // tq
    r = tq // w
    c0 = (b_sz * s_len) // cn
    smem = pl.BlockSpec(memory_space=pltpu.SMEM)

    def main(b, h, i):
        return b * nq + i

    def prev(b, h, i):
        return b * (s_len // w) + jnp.maximum(i * r - 1, 0)

    def nxt(b, h, i):
        return b * (s_len // w) + jnp.minimum((i + 1) * r, s_len // w - 1)

    o_lat = pl.pallas_call(
        functools.partial(_swa_kernel, tq=tq, s_len=s_len, latent=True),
        grid=(b_sz, SWA_KV_HEADS, nq),
        in_specs=[
            smem,
            pl.BlockSpec((tq, gw), lambda b, h, i: (main(b, h, i), h)),
            pl.BlockSpec((w, d), lambda b, h, i: (prev(b, h, i), kcol + h)),
            pl.BlockSpec((tq, d), lambda b, h, i: (main(b, h, i), kcol + h)),
            pl.BlockSpec((w, d), lambda b, h, i: (nxt(b, h, i), kcol + h)),
            pl.BlockSpec((cn, d), lambda b, h, i: (c0 + b, kcol + h)),
            pl.BlockSpec((w, d), lambda b, h, i: (prev(b, h, i), vcol + h)),
            pl.BlockSpec((tq, d), lambda b, h, i: (main(b, h, i), vcol + h)),
            pl.BlockSpec((w, d), lambda b, h, i: (nxt(b, h, i), vcol + h)),
            pl.BlockSpec((cn, d), lambda b, h, i: (c0 + b, vcol + h)),
        ],
        out_specs=pl.BlockSpec((tq, gw), lambda b, h, i: (main(b, h, i), h)),
        out_shape=jax.ShapeDtypeStruct((b_sz * s_len, SWA_HEADS * d), BF16),
        name="swa_latent",
        compiler_params=_cparams(("arbitrary", "arbitrary", "arbitrary")),
    )(sink, a, a, a, a, a, a, a, a, a)

    o_ctx = pl.pallas_call(
        functools.partial(_swa_kernel, tq=cn, s_len=s_len, latent=False),
        grid=(b_sz, SWA_KV_HEADS),
        in_specs=[
            smem,
            pl.BlockSpec((cn, gw), lambda b, h: (c0 + b, h)),
            pl.BlockSpec((cn, d), lambda b, h: (c0 + b, kcol + h)),
            pl.BlockSpec((cn, d), lambda b, h: (c0 + b, vcol + h)),
        ],
        out_specs=pl.BlockSpec((cn, gw), lambda b, h: (b, h)),
        out_shape=jax.ShapeDtypeStruct((b_sz * cn, SWA_HEADS * d), BF16),
        name="swa_context",
        compiler_params=_cparams(("arbitrary", "arbitrary")),
    )(sink, a, a, a)
    return o_lat, o_ctx


def _mla_kernel(q_ref, kc_ref, vc_ref, *refs, tk, latent):
    if latent:
        kl_ref, vl_ref, o_ref = refs
    else:
        (o_ref,) = refs
    dv = MLA_V_DIM
    q = q_ref[...]
    s = _dot_nt(q, kc_ref[...])
    m = jnp.max(s, axis=-1, keepdims=True)
    acc = _dot(jnp.exp2(s - m).astype(BF16), vc_ref[...])
    if latent:
        for j in range(kl_ref.shape[0] // tk):
            sl = slice(j * tk, (j + 1) * tk)
            s = _dot_nt(q, kl_ref[sl, :])
            m_new = jnp.maximum(m, jnp.max(s, axis=-1, keepdims=True))
            acc = jnp.exp2(m - m_new) * acc + _dot(jnp.exp2(s - m_new).astype(BF16), vl_ref[sl, :])
            m = m_new
    o_ref[...] = (acc[:, :dv] / acc[:, dv:]).astype(BF16)


def _mla_attention(qm, kcat, vext, b_sz, cn, s_len, tq, tk):
    hp = MLA_HEAD_PAD
    dv = MLA_V_DIM
    nq = s_len // tq
    c0 = (b_sz * s_len) // cn

    o_lat = pl.pallas_call(
        functools.partial(_mla_kernel, tk=tk, latent=True),
        grid=(b_sz, MLA_HEADS, nq),
        in_specs=[
            pl.BlockSpec((tq, hp), lambda b, h, i: (b * nq + i, h)),
            pl.BlockSpec((cn, hp), lambda b, h, i: (c0 + b, h)),
            pl.BlockSpec((cn, hp), lambda b, h, i: (c0 + b, h)),
            pl.BlockSpec((s_len, hp), lambda b, h, i: (b, h)),
            pl.BlockSpec((s_len, hp), lambda b, h, i: (b, h)),
        ],
        out_specs=pl.BlockSpec((tq, dv), lambda b, h, i: (b * nq + i, h)),
        out_shape=jax.ShapeDtypeStruct((b_sz * s_len, MLA_HEADS * dv), BF16),
        name="mla_latent",
        compiler_params=_cparams(("arbitrary",) * 3),
    )(qm, kcat, vext, kcat, vext)

    o_ctx = pl.pallas_call(
        functools.partial(_mla_kernel, tk=0, latent=False),
        grid=(b_sz, MLA_HEADS),
        in_specs=[
            pl.BlockSpec((cn, hp), lambda b, h: (c0 + b, h)),
            pl.BlockSpec((cn, hp), lambda b, h: (c0 + b, h)),
            pl.BlockSpec((cn, hp), lambda b, h: (c0 + b, h)),
        ],
        out_specs=pl.BlockSpec((cn, dv), lambda b, h: (b, h)),
        out_shape=jax.ShapeDtypeStruct((b_sz * cn, MLA_HEADS * dv), BF16),
        name="mla_context",
        compiler_params=_cparams(("arbitrary", "arbitrary")),
    )(qm, kcat, vext)
    return o_lat, o_ctx


def _out_proj_kernel(w_ref, mod_ref, *refs, n_parts, n_lat_tiles):
    o_ref = refs[-1]
    lat = refs[:n_parts + 1]
    ctx = refs[n_parts + 1:-1]

    def compute(side):
        acc = None
        k0 = 0
        for p in side[:-1]:
            kw = p.shape[1]
            t = _dot(p[...], w_ref[k0:k0 + kw, :])
            acc = t if acc is None else acc + t
            k0 += kw
        o_ref[...] = side[-1][...] + mod_ref[0, 0, 2:3, :] * acc

    if n_lat_tiles is None:
        compute(lat)
    else:
        i = pl.program_id(0)
        pl.when(i < n_lat_tiles)(lambda: compute(lat))
        pl.when(i >= n_lat_tiles)(lambda: compute(ctx))


def _out_proj(lat_parts, lat_res, ctx_parts, ctx_res, w, mod_all, layer, mrow, tm, tn):
    lat_rows = lat_parts[0].shape[0]
    d = lat_res.shape[1]
    kdim = w.shape[0]
    n_lat = lat_rows // tm
    if ctx_parts is None:
        n_lat_tiles = None
        rows = lat_rows
        specs = [pl.BlockSpec((tm, p.shape[1]), lambda i, j: (i, 0)) for p in lat_parts]
        specs.append(pl.BlockSpec((tm, tn), lambda i, j: (i, j)))
        operands = [*lat_parts, lat_res]
    else:
        n_lat_tiles = n_lat
        rows = lat_rows + ctx_parts[0].shape[0]
        is_lat = lambda i: i < n_lat
        lat_row = lambda i: jnp.minimum(i, n_lat - 1)
        ctx_row = lambda i: jnp.maximum(i - n_lat, 0)
        specs = [pl.BlockSpec((tm, p.shape[1]), lambda i, j: (lat_row(i), 0)) for p in lat_parts]
        specs.append(pl.BlockSpec((tm, tn), lambda i, j: (lat_row(i), jnp.where(is_lat(i), j, 0))))
        specs += [pl.BlockSpec((tm, p.shape[1]), lambda i, j: (ctx_row(i), 0)) for p in ctx_parts]
        specs.append(pl.BlockSpec((tm, tn), lambda i, j: (ctx_row(i), jnp.where(is_lat(i), 0, j))))
        operands = [*lat_parts, lat_res, *ctx_parts, ctx_res]
    return pl.pallas_call(
        functools.partial(_out_proj_kernel, n_parts=len(lat_parts), n_lat_tiles=n_lat_tiles),
        grid=(rows // tm, d // tn),
        in_specs=[
            pl.BlockSpec((kdim, tn), lambda i, j: (0, j)),
            pl.BlockSpec((1, 1, 6, tn), lambda i, j: (layer, mrow(i), 0, j)),
        ] + specs,
        out_specs=pl.BlockSpec((tm, tn), lambda i, j: (i, j)),
        out_shape=jax.ShapeDtypeStruct((rows, d), F32),
        name="out_proj",
        compiler_params=_cparams(("arbitrary", "arbitrary")),
    )(w, mod_all, *operands)


def _ffn_kernel(x_ref, g_ref, mod_ref, wg_ref, wu_ref, wd_ref, fg_ref, o_ref, h_ref, acc_ref, *, nf, final):
    j = pl.program_id(1)

    @pl.when(j == 0)
    def _():
        _norm_mod_into(h_ref, x_ref, g_ref, mod_ref, 3)
        acc_ref[...] = jnp.zeros_like(acc_ref)

    h = h_ref[...]
    a = (_silu(_dot(h, wg_ref[0])) * _dot(h, wu_ref[0])).astype(BF16)
    acc_ref[...] += _dot(a, wd_ref[0])

    @pl.when(j == nf - 1)
    def _():
        y = x_ref[...] + mod_ref[0, 0, 5:6, :] * acc_ref[...]
        if final:
            y = y * lax.rsqrt(jnp.mean(y * y, axis=-1, keepdims=True) + NORM_EPS) * fg_ref[...]
        o_ref[...] = y


def _ffn(xt, norm_g, mod_all, layer, wg, wu, wd, final_g, mrow, n_rows, tm, tf, final):
    d = xt.shape[1]
    f = wg.shape[2]
    nf = f // tf
    return pl.pallas_call(
        functools.partial(_ffn_kernel, nf=nf, final=final),
        grid=(n_rows // tm, nf),
        in_specs=[
            pl.BlockSpec((tm, d), lambda i, j: (i, 0)),
            pl.BlockSpec((1, d), lambda i, j: (0, 0)),
            pl.BlockSpec((1, 1, 6, d), lambda i, j: (layer, mrow(i), 0, 0)),
            pl.BlockSpec((1, d, tf), lambda i, j: (layer, 0, j)),
            pl.BlockSpec((1, d, tf), lambda i, j: (layer, 0, j)),
            pl.BlockSpec((1, tf, d), lambda i, j: (layer, j, 0)),
            pl.BlockSpec((1, d), lambda i, j: (0, 0)),
        ],
        out_specs=pl.BlockSpec((tm, d), lambda i, j: (i, 0)),
        out_shape=jax.ShapeDtypeStruct((n_rows, d), F32),
        scratch_shapes=[pltpu.VMEM((tm, d), BF16), pltpu.VMEM((tm, d), F32)],
        name="ffn_final" if final else "ffn",
        compiler_params=_cparams(("arbitrary", "arbitrary")),
    )(xt, norm_g, mod_all, wg, wu, wd, final_g)


def _ret_in_kernel(x_ref, g_ref, mod_ref, w_ref, cos_ref, sin_ref, o_ref, h_ref, *, n_qk, k_scale, tn):
    j = pl.program_id(1)

    pl.when(j == 0)(lambda: _norm_mod_into(h_ref, x_ref, g_ref, mod_ref, 0))

    def block(scale):
        h = h_ref[...]
        for sub in range(tn // SUB_N):
            o = sub * SUB_N
            z = _dot(h, w_ref[:, o:o + SUB_N])
            if scale is None:
                o_ref[:, o:o + SUB_N] = z.astype(BF16)
            else:
                c = cos_ref[...]
                s = sin_ref[...]
                x1 = z[:, :LANES] * scale if scale != 1.0 else z[:, :LANES]
                x2 = z[:, LANES:] * scale if scale != 1.0 else z[:, LANES:]
                o_ref[:, o:o + LANES] = (x1 * c - x2 * s).astype(BF16)
                o_ref[:, o + LANES:o + SUB_N] = (x1 * s + x2 * c).astype(BF16)

    pl.when(j < n_qk)(lambda: block(1.0))
    pl.when((j >= n_qk) & (j < 2 * n_qk))(lambda: block(k_scale))
    pl.when(j >= 2 * n_qk)(lambda: block(None))


def _ret_in_proj(xt, norm_g, mod_all, layer, w_in, cos_r, sin_r, mrow, tm, tn):
    nt, d = xt.shape
    n = w_in.shape[1]
    dk = d // RET_HEADS
    return pl.pallas_call(
        functools.partial(_ret_in_kernel, n_qk=d // tn, k_scale=dk ** -0.5, tn=tn),
        grid=(nt // tm, n // tn),
        in_specs=[
            pl.BlockSpec((tm, d), lambda i, j: (i, 0)),
            pl.BlockSpec((1, d), lambda i, j: (0, 0)),
            pl.BlockSpec((1, 1, 6, d), lambda i, j: (layer, mrow(i), 0, 0)),
            pl.BlockSpec((d, tn), lambda i, j: (0, j)),
            pl.BlockSpec((tm, LANES), lambda i, j: (i, 0)),
            pl.BlockSpec((tm, LANES), lambda i, j: (i, 0)),
        ],
        out_specs=pl.BlockSpec((tm, tn), lambda i, j: (i, j)),
        out_shape=jax.ShapeDtypeStruct((nt, n), BF16),
        scratch_shapes=[pltpu.VMEM((tm, d), BF16)],
        name="ret_in_proj",
        compiler_params=_cparams(("arbitrary", "arbitrary")),
    )(xt, norm_g, mod_all, w_in, cos_r, sin_r)


def _log_sigmoid(x):
    return jnp.minimum(x, 0.0) - jnp.log1p(jnp.exp(-jnp.abs(x)))


def _ret_kernel(lf_ref, lb_ref, gn_ref, qc_ref, kc_ref, vc_ref, q_ref, k_ref, v_ref, g_ref, y_ref,
                o_ref, s_ref, dint_ref, qdec_ref, kdec_ref, *, n_ctx, n_lat):
    c = RET_CHUNK
    dk = q_ref.shape[1]
    dv = v_ref.shape[1]
    row = lax.broadcasted_iota(jnp.int32, (c, c), 0)
    col = lax.broadcasted_iota(jnp.int32, (c, c), 1)
    pos = lax.broadcasted_iota(jnp.int32, (c, LANES), 0).astype(F32)
    c_dec = []
    for drn, l_ref in enumerate((lf_ref, lb_ref)):
        lg = _log_sigmoid(l_ref[0])
        lg_c = jnp.broadcast_to(lg[:, 0:1], (1, c))
        diff = (col - row) if drn else (row - col)
        dint_ref[drn] = jnp.where(diff >= 0, jnp.exp(lg_c * jnp.maximum(diff, 0).astype(F32)), 0.0)
        if drn:
            qdec_ref[drn] = jnp.exp(lg * (c - pos))
            kdec_ref[drn] = jnp.exp(lg * pos)
        else:
            qdec_ref[drn] = jnp.exp(lg * (pos + 1.0))
            kdec_ref[drn] = jnp.exp(lg * (c - 1.0 - pos))
        c_dec.append(jnp.exp(jnp.broadcast_to(lg[:, 0:1], (1, dv)) * c))

    def scale_rows(x, dec):
        return jnp.concatenate([x[:, k * LANES:(k + 1) * LANES].astype(F32) * dec for k in range(dk // LANES)],
                               axis=1).astype(BF16)

    def step(drn, q, k, v, want_out):
        s = s_ref[drn]
        o = None
        if want_out:
            a = (_dot_nt(q, k) * dint_ref[drn]).astype(BF16)
            o = _dot(a, v) + _dot(scale_rows(q, qdec_ref[drn]), s.astype(BF16))
        s_ref[drn] = s * c_dec[drn] + _dot_tn(scale_rows(k, kdec_ref[drn]), v)
        return o

    def rows(t):
        return pl.ds(pl.multiple_of(t * c, c), c)

    def lat_step(drn, t):
        sl = rows(t)
        return step(drn, q_ref[sl, :], k_ref[sl, :], v_ref[sl, :], True)

    def finish(t, o):
        sl = rows(t)
        mu = jnp.mean(o, axis=-1, keepdims=True)
        oc = o - mu
        var = jnp.mean(oc * oc, axis=-1, keepdims=True)
        yn = oc * lax.rsqrt(var + NORM_EPS) * gn_ref[...]
        y_ref[sl, :] = (_silu(g_ref[sl, :].astype(F32)) * yn).astype(BF16)

    s_ref[...] = jnp.zeros_like(s_ref)
    for t in range(n_ctx):
        for drn, tt in ((0, t), (1, n_ctx - 1 - t)):
            sl = slice(tt * c, (tt + 1) * c)
            step(drn, qc_ref[sl, :], kc_ref[sl, :], vc_ref[sl, :], False)

    half = n_lat // 2

    def first_half(t, carry):
        tb = n_lat - 1 - t
        o_ref[rows(t), :] = lat_step(0, t)
        o_ref[rows(tb), :] = lat_step(1, tb)
        return carry

    def second_half(t, carry):
        tb = n_lat - 1 - t
        finish(t, o_ref[rows(t), :] + lat_step(0, t))
        finish(tb, o_ref[rows(tb), :] + lat_step(1, tb))
        return carry

    lax.fori_loop(0, half, first_half, 0)
    lax.fori_loop(half, n_lat, second_half, 0)


def _retention(r, lg_f, lg_b, gn_g, b_sz, cn, s_len, d):
    h_n = RET_HEADS
    dk = d // h_n
    dv = 2 * dk
    c = RET_CHUNK
    assert cn % c == 0 and s_len % (2 * c) == 0
    c0 = (b_sz * s_len) // cn
    kq, kk, kv, kg = 0, d // dk, (2 * d) // dv, (4 * d) // dv

    def lg_spec():
        return pl.BlockSpec((1, 1, LANES), lambda b, h: (h, 0, 0))

    return pl.pallas_call(
        functools.partial(_ret_kernel, n_ctx=cn // c, n_lat=s_len // c),
        grid=(b_sz, h_n),
        in_specs=[
            lg_spec(), lg_spec(),
            pl.BlockSpec((1, dv), lambda b, h: (0, h)),
            pl.BlockSpec((cn, dk), lambda b, h: (c0 + b, kq + h)),
            pl.BlockSpec((cn, dk), lambda b, h: (c0 + b, kk + h)),
            pl.BlockSpec((cn, dv), lambda b, h: (c0 + b, kv + h)),
            pl.BlockSpec((s_len, dk), lambda b, h: (b, kq + h)),
            pl.BlockSpec((s_len, dk), lambda b, h: (b, kk + h)),
            pl.BlockSpec((s_len, dv), lambda b, h: (b, kv + h)),
            pl.BlockSpec((s_len, dv), lambda b, h: (b, kg + h)),
        ],
        out_specs=pl.BlockSpec((s_len, dv), lambda b, h: (b, h)),
        out_shape=jax.ShapeDtypeStruct((b_sz * s_len, h_n * dv), BF16),
        scratch_shapes=[
            pltpu.VMEM((s_len, dv), F32),
            pltpu.VMEM((2, dk, dv), F32),
            pltpu.VMEM((2, c, c), F32),
            pltpu.VMEM((2, c, LANES), F32),
            pltpu.VMEM((2, c, LANES), F32),
        ],
        name="retention",
        compiler_params=_cparams(("arbitrary", "arbitrary")),
    )(lg_f, lg_b, gn_g, r, r, r, r, r, r, r)


def _rope_angles(s_len, rot_dim):
    rows = s_len // GRID_W
    row = jnp.repeat(jnp.arange(rows, dtype=F32), GRID_W)
    col = (jnp.arange(s_len) % GRID_W).astype(F32)
    n_freq = rot_dim // 4
    inv = ROPE_BASE ** (-jnp.arange(n_freq, dtype=F32) / n_freq)
    ang = jnp.concatenate([row[:, None] * inv, col[:, None] * inv], axis=-1)
    return jnp.cos(ang), jnp.sin(ang)


def _token_table(lat_table, ctx_row, b_sz, n_ctx_rows):
    width = lat_table.shape[1]
    return jnp.concatenate([jnp.tile(lat_table, (b_sz, 1)),
                            jnp.broadcast_to(ctx_row[None, :], (n_ctx_rows, width))], axis=0)


def kernel(x, c, ctx, c_ctx, mod_w, mod_b, norm_mix_g, norm_ffn_g, ffn_w_gate, ffn_w_up, ffn_w_down, ab_w_in, ab_w_out, swa_sink, mla_q_norm_g, mla_w_q_b, mla_kv_norm_g, mla_w_kv_b, ret_w_in, ret_decay_logit_fwd, ret_decay_logit_bwd, ret_gn_g, ret_w_out, final_norm_g):
    b_sz, s_len, d = x.shape
    cn = ctx.shape[1]
    depth = mod_w.shape[0]
    assert depth == 2 and ab_w_in.shape[0] == 1 and ret_w_in.shape[0] == 1
    assert b_sz + 1 <= 8
    n_lat_rows = b_sz * s_len
    n_ctx_rows = b_sz * cn
    nt = n_lat_rows + n_ctx_rows

    tm = min(1024, n_ctx_rows, s_len)
    tm_ffn = min(512, tm)
    tq = min(512, s_len)
    tk = min(1024, s_len)
    assert n_ctx_rows % tm == 0 and s_len % tm == 0 and s_len % cn == 0 and s_len % tq == 0 and s_len % tk == 0

    def make_mrow(t):
        n_lat_tiles = n_lat_rows // t
        per_b = s_len // t
        return lambda i: jnp.where(i < n_lat_tiles, 1 + i // per_b, 0)

    mrow = make_mrow(tm)
    mrow_ffn = make_mrow(tm_ffn)

    x2d = x.reshape(n_lat_rows, d)
    ctx2d = ctx.reshape(n_ctx_rows, d)

    cond8 = jnp.zeros((8, d), F32).at[0].set(c_ctx).at[1:1 + b_sz].set(c)
    mod_all = _modulation(cond8, mod_w, mod_b).reshape(depth, 8, 6, d)

    cos_a, sin_a = _rope_angles(s_len, SWA_HEAD_DIM)
    cos_b, sin_b = _rope_angles(s_len, MLA_ROPE_DIM)
    cos_r, sin_r = _rope_angles(s_len, d // RET_HEADS)
    ones = jnp.ones((LANES,), F32)
    zeros = jnp.zeros((LANES,), F32)
    half = jnp.concatenate([jnp.ones((64,), F32), jnp.zeros((64,), F32)])
    z64 = jnp.zeros((s_len, 64), F32)
    table = functools.partial(_token_table, b_sz=b_sz, n_ctx_rows=n_ctx_rows)
    t_cos_a = table(jnp.concatenate([cos_a, cos_a], axis=1), ones)
    t_sin_a = table(jnp.concatenate([-sin_a, sin_a], axis=1), zeros)
    t_cos_b = table(jnp.concatenate([cos_b, cos_b, z64], axis=1), half)
    t_sin_b = table(jnp.concatenate([-sin_b, sin_b, z64], axis=1), zeros)
    t_cos_r = table(cos_r, ones)
    t_sin_r = table(sin_r, zeros)

    bf = lambda w: w.astype(BF16)
    w_ab_in = jnp.pad(ab_w_in[0], ((0, 0), (0, 2560 - ab_w_in.shape[2]))).astype(BF16)
    wq_b = jnp.pad(mla_w_q_b[0].reshape(MLA_Q_RANK, MLA_HEADS, MLA_NOPE_DIM + MLA_ROPE_DIM),
                   ((0, 0), (0, 0), (0, MLA_HEAD_PAD - MLA_NOPE_DIM - MLA_ROPE_DIM))
                   ).reshape(MLA_Q_RANK, MLA_HEADS * MLA_HEAD_PAD).astype(BF16)
    wkv_b = bf(mla_w_kv_b[0])
    wg, wu, wd = bf(ffn_w_gate), bf(ffn_w_up), bf(ffn_w_down)
    final_g = final_norm_g[None, :]

    a_qkv, lat = _ab_in_proj(x2d, ctx2d, norm_mix_g[0:1], mod_all, 0, w_ab_in, t_cos_a, t_sin_a, mrow_ffn, tm_ffn)
    qm, kcat, vext = _mla_proj(lat, mla_q_norm_g[0:1], mla_kv_norm_g[0:1], wq_b, wkv_b, t_cos_b, t_sin_b, tm)
    oa, oa_c = _swa_attention(a_qkv, swa_sink[0], b_sz, cn, s_len, tq)
    ob, ob_c = _mla_attention(qm, kcat, vext, b_sz, cn, s_len, tq, tk)
    xt = _out_proj([oa, ob], x2d, [oa_c, ob_c], ctx2d, bf(ab_w_out[0]), mod_all, 0, mrow, tm, 512)
    xt = _ffn(xt, norm_ffn_g[0:1], mod_all, 0, wg, wu, wd, final_g, mrow_ffn, nt, tm_ffn, 512, final=False)

    r = _ret_in_proj(xt, norm_mix_g[1:2], mod_all, 1, bf(ret_w_in[0]), t_cos_r, t_sin_r, mrow, tm, 1024)
    lg_shape = (RET_HEADS, 1, LANES)
    lg_f = jnp.broadcast_to(ret_decay_logit_fwd[0].astype(F32)[:, None, None], lg_shape)
    lg_b = jnp.broadcast_to(ret_decay_logit_bwd[0].astype(F32)[:, None, None], lg_shape)
    y = _retention(r, lg_f, lg_b, ret_gn_g[0:1], b_sz, cn, s_len, d)
    xl = _out_proj([y], xt, None, None, bf(ret_w_out[0]), mod_all, 1, mrow, tm, 512)
    out = _ffn(xl, norm_ffn_g[1:2], mod_all, 1, wg, wu, wd, final_g, mrow_ffn, n_lat_rows, tm_ffn, 512, final=True)
    return out.reshape(b_sz, s_len, d)
```

```python
import functools

import jax
import jax.numpy as jnp
from jax import lax
from jax.experimental import pallas as pl
from jax.experimental.pallas import tpu as pltpu

GRID_W = 64
ROPE_BASE = 10000.0
NORM_EPS = 1e-6
NEG_INF = -1e30
LOG2_E = 1.4426950408889634

SWA_HEADS = 8
SWA_KV_HEADS = 2
SWA_HEAD_DIM = 128
SWA_WINDOW = 128
SWA_QSCALE = SWA_HEAD_DIM ** -0.5 * LOG2_E

MLA_HEADS = 8
MLA_Q_RANK = 512
MLA_KV_RANK = 256
MLA_NOPE_DIM = 128
MLA_ROPE_DIM = 64
MLA_V_DIM = 128
MLA_HEAD_PAD = 256
MLA_QSCALE = (MLA_NOPE_DIM + MLA_ROPE_DIM) ** -0.5 * LOG2_E

RET_HEADS = 8
RET_CHUNK = 256

LANES = 128
V7X_VMEM_BYTES = 64 * 1024 * 1024
VMEM_LIMIT = V7X_VMEM_BYTES - 8 * 1024 * 1024

F32 = jnp.float32
BF16 = jnp.bfloat16


def _cparams(sem):
    return pltpu.CompilerParams(dimension_semantics=sem, vmem_limit_bytes=VMEM_LIMIT)


def _dot(a, b):
    return jnp.dot(a, b, preferred_element_type=F32)


def _dot_nt(a, b):
    return lax.dot_general(a, b, (((1,), (1,)), ((), ())), preferred_element_type=F32)


def _dot_tn(a, b):
    return lax.dot_general(a, b, (((0,), (0,)), ((), ())), preferred_element_type=F32)


def _silu(x):
    h = 0.5 * x
    return h + h * jnp.tanh(h)


NORM_ROWS = 32
SUB_N = 256


def _norm_mod_into(h_ref, x_ref, g_ref, mod_ref, shift_row):
    shift = mod_ref[0, 0, shift_row:shift_row + 1, :]
    gain = g_ref[...] * (1.0 + mod_ref[0, 0, shift_row + 1:shift_row + 2, :])

    def body(t, carry):
        sl = pl.ds(pl.multiple_of(t * NORM_ROWS, NORM_ROWS), NORM_ROWS)
        x = x_ref[sl, :]
        inv = lax.rsqrt(jnp.mean(x * x, axis=-1, keepdims=True) + NORM_EPS)
        h_ref[sl, :] = ((x * inv) * gain + shift).astype(BF16)
        return carry

    lax.fori_loop(0, x_ref.shape[0] // NORM_ROWS, body, 0, unroll=4)


def _split_rows(n_lat_tiles):
    lat = lambda i, j: (jnp.minimum(i, n_lat_tiles - 1), 0)
    ctx = lambda i, j: (jnp.maximum(i - n_lat_tiles, 0), 0)
    return lat, ctx


def _mod_kernel(c_ref, w_ref, b_ref, o_ref):
    a = _silu(c_ref[...]).astype(BF16)
    o_ref[0] = _dot(a, w_ref[0].astype(BF16)) + b_ref[0]


def _modulation(cond8, mod_w, mod_b):
    depth, d, n = mod_w.shape
    tn = 1024
    return pl.pallas_call(
        _mod_kernel,
        grid=(depth, n // tn),
        in_specs=[
            pl.BlockSpec((8, d), lambda l, j: (0, 0)),
            pl.BlockSpec((1, d, tn), lambda l, j: (l, 0, j)),
            pl.BlockSpec((1, 1, tn), lambda l, j: (l, 0, j)),
        ],
        out_specs=pl.BlockSpec((1, 8, tn), lambda l, j: (l, 0, j)),
        out_shape=jax.ShapeDtypeStruct((depth, 8, n), F32),
        name="modulation",
        compiler_params=_cparams(("arbitrary", "arbitrary")),
    )(cond8, mod_w, mod_b.reshape(depth, 1, n))


def _rope128(z, cosf, sins):
    return z * cosf + pltpu.roll(z, 64, axis=1) * sins


def _ab_in_kernel(x_ref, c_ref, g_ref, mod_ref, w_ref, cos_ref, sin_ref, a_ref, l_ref, h_ref, *, n_lat_tiles):
    i = pl.program_id(0)
    j = pl.program_id(1)

    pl.when((j == 0) & (i < n_lat_tiles))(lambda: _norm_mod_into(h_ref, x_ref, g_ref, mod_ref, 0))
    pl.when((j == 0) & (i >= n_lat_tiles))(lambda: _norm_mod_into(h_ref, c_ref, g_ref, mod_ref, 0))

    def block(kinds):
        h = h_ref[...]
        for half, kind in enumerate(kinds):
            z = _dot(h, w_ref[:, half * SUB_N:(half + 1) * SUB_N])
            for k in range(SUB_N // LANES):
                sl = slice(half * SUB_N + k * LANES, half * SUB_N + (k + 1) * LANES)
                zk = z[:, k * LANES:(k + 1) * LANES]
                if kind == "q":
                    a_ref[:, sl] = (_rope128(zk, cos_ref[...], sin_ref[...]) * SWA_QSCALE).astype(BF16)
                elif kind == "k":
                    a_ref[:, sl] = _rope128(zk, cos_ref[...], sin_ref[...]).astype(BF16)
                elif kind == "v":
                    a_ref[:, sl] = zk.astype(BF16)
                else:
                    l_ref[:, sl] = zk

    pl.when(j <= 1)(lambda: block(("q", "q")))
    pl.when(j == 2)(lambda: block(("k", "v")))
    pl.when(j >= 3)(lambda: block(("f32", "f32")))


def _ab_in_proj(x2d, ctx2d, norm_g, mod_all, layer, w_in, cos_a, sin_a, mrow, tm):
    d = x2d.shape[1]
    n_lat_tiles = x2d.shape[0] // tm
    nt = x2d.shape[0] + ctx2d.shape[0]
    tn = 512
    lat_map, ctx_map = _split_rows(n_lat_tiles)
    return pl.pallas_call(
        functools.partial(_ab_in_kernel, n_lat_tiles=n_lat_tiles),
        grid=(nt // tm, 5),
        in_specs=[
            pl.BlockSpec((tm, d), lat_map),
            pl.BlockSpec((tm, d), ctx_map, pipeline_mode=pl.Buffered(1)),
            pl.BlockSpec((1, d), lambda i, j: (0, 0)),
            pl.BlockSpec((1, 1, 6, d), lambda i, j: (layer, mrow(i), 0, 0)),
            pl.BlockSpec((d, tn), lambda i, j: (0, j)),
            pl.BlockSpec((tm, LANES), lambda i, j: (i, 0)),
            pl.BlockSpec((tm, LANES), lambda i, j: (i, 0)),
        ],
        out_specs=[
            pl.BlockSpec((tm, tn), lambda i, j: (i, jnp.minimum(j, 2))),
            pl.BlockSpec((tm, tn), lambda i, j: (i, jnp.maximum(j - 3, 0))),
        ],
        out_shape=[
            jax.ShapeDtypeStruct((nt, 3 * tn), BF16),
            jax.ShapeDtypeStruct((nt, 2 * tn), F32),
        ],
        scratch_shapes=[pltpu.VMEM((tm, d), BF16)],
        name="ab_in_proj",
        compiler_params=_cparams(("arbitrary", "arbitrary")),
    )(x2d, ctx2d, norm_g, mod_all, w_in, cos_a, sin_a)


def _rope64(r, cosp, sinp):
    lane = lax.broadcasted_iota(jnp.int32, r.shape, 1)
    partner = jnp.where(lane < 32, pltpu.roll(r, 96, axis=1), pltpu.roll(r, 32, axis=1))
    return r * cosp + partner * sinp


def _mla_proj_kernel(l_ref, qg_ref, kvg_ref, wq_ref, wkv_ref, cos_ref, sin_ref, qm_ref, kc_ref, vm_ref):
    cosp = cos_ref[...]
    sinp = sin_ref[...]

    def rms(x, g):
        return x * lax.rsqrt(jnp.mean(x * x, axis=-1, keepdims=True) + NORM_EPS) * g

    qn = rms(l_ref[:, :MLA_Q_RANK], qg_ref[...]).astype(BF16)
    qm = _dot(qn, wq_ref[...])
    kvn = rms(l_ref[:, MLA_Q_RANK:MLA_Q_RANK + MLA_KV_RANK], kvg_ref[...]).astype(BF16)
    kv = _dot(kvn, wkv_ref[...])
    kr = l_ref[:, MLA_Q_RANK + MLA_KV_RANK:MLA_Q_RANK + MLA_KV_RANK + LANES]
    krr = _rope64(kr, cosp, sinp).astype(BF16)
    ones = jnp.ones((l_ref.shape[0], LANES), BF16)
    for h in range(MLA_HEADS):
        o = h * MLA_HEAD_PAD
        qm_ref[:, o:o + LANES] = (qm[:, o:o + LANES] * MLA_QSCALE).astype(BF16)
        qm_ref[:, o + LANES:o + 2 * LANES] = (
            _rope64(qm[:, o + LANES:o + 2 * LANES], cosp, sinp) * MLA_QSCALE).astype(BF16)
        kc_ref[:, o:o + LANES] = kv[:, o:o + LANES].astype(BF16)
        kc_ref[:, o + LANES:o + 2 * LANES] = krr
        vm_ref[:, o:o + LANES] = kv[:, o + LANES:o + 2 * LANES].astype(BF16)
        vm_ref[:, o + LANES:o + 2 * LANES] = ones


def _mla_proj(lat, q_norm_g, kv_norm_g, wq, wkv, cos_b, sin_b, tm):
    nt = lat.shape[0]
    hp = MLA_HEADS * MLA_HEAD_PAD
    return pl.pallas_call(
        _mla_proj_kernel,
        grid=(nt // tm,),
        in_specs=[
            pl.BlockSpec((tm, lat.shape[1]), lambda i: (i, 0)),
            pl.BlockSpec((1, MLA_Q_RANK), lambda i: (0, 0)),
            pl.BlockSpec((1, MLA_KV_RANK), lambda i: (0, 0)),
            pl.BlockSpec(wq.shape, lambda i: (0, 0)),
            pl.BlockSpec(wkv.shape, lambda i: (0, 0)),
            pl.BlockSpec((tm, LANES), lambda i: (i, 0)),
            pl.BlockSpec((tm, LANES), lambda i: (i, 0)),
        ],
        out_specs=[
            pl.BlockSpec((tm, hp), lambda i: (i, 0)),
            pl.BlockSpec((tm, hp), lambda i: (i, 0)),
            pl.BlockSpec((tm, hp), lambda i: (i, 0)),
        ],
        out_shape=[
            jax.ShapeDtypeStruct((nt, hp), BF16),
            jax.ShapeDtypeStruct((nt, hp), BF16),
            jax.ShapeDtypeStruct((nt, hp), BF16),
        ],
        name="mla_proj",
        compiler_params=_cparams(("arbitrary",)),
    )(lat, q_norm_g, kv_norm_g, wq, wkv, cos_b, sin_b)


def _swa_softmax_pv(s, sk, v_ext):
    d = SWA_HEAD_DIM
    m = jnp.maximum(jnp.max(s, axis=-1, keepdims=True), sk)
    o_ext = _dot(jnp.exp2(s - m).astype(BF16), v_ext)
    return o_ext[:, :d] / (o_ext[:, d:] + jnp.exp2(sk - m))


def _swa_ctx_kernel(sink_ref, q_ref, kc, vc, o_ref):
    kvh = pl.program_id(1)
    d = SWA_HEAD_DIM
    g_heads = SWA_HEADS // SWA_KV_HEADS
    k_all = kc[...]
    v_ext = jnp.concatenate([vc[...], jnp.ones_like(vc)], axis=1)
    for g in range(g_heads):
        sk = sink_ref[kvh * g_heads + g] * LOG2_E
        s = _dot_nt(q_ref[:, g * d:(g + 1) * d], k_all)
        o_ref[:, g * d:(g + 1) * d] = _swa_softmax_pv(s, sk, v_ext).astype(BF16)


def _swa_kernel(sink_ref, q_ref, kp, km, kn, kc, vp, vm, vn, vc, o_ref, *, tq, s_len):
    kvh = pl.program_id(1)
    i = pl.program_id(2)
    d = SWA_HEAD_DIM
    g_heads = SWA_HEADS // SWA_KV_HEADS
    k_all = jnp.concatenate([kp[...], km[...], kn[...], kc[...]], axis=0)
    v_all = jnp.concatenate([vp[...], vm[...], vn[...], vc[...]], axis=0)
    v_ext = jnp.concatenate([v_all, jnp.ones_like(v_all)], axis=1)
    nb = tq + 2 * SWA_WINDOW
    shape = (tq, k_all.shape[0])
    row = lax.broadcasted_iota(jnp.int32, shape, 0)
    col = lax.broadcasted_iota(jnp.int32, shape, 1)
    kpos = i * tq - SWA_WINDOW + col
    valid = (col >= nb) | ((jnp.abs(col - SWA_WINDOW - row) <= SWA_WINDOW) & (kpos >= 0) & (kpos < s_len))
    for g in range(g_heads):
        sk = sink_ref[kvh * g_heads + g] * LOG2_E
        s = jnp.where(valid, _dot_nt(q_ref[:, g * d:(g + 1) * d], k_all), NEG_INF)
        o_ref[:, g * d:(g + 1) * d] = _swa_softmax_pv(s, sk, v_ext).astype(BF16)


def _swa_attention(a, sink, b_sz, cn, s_len, tq):
    d = SWA_HEAD_DIM
    w = SWA_WINDOW
    gw = (SWA_HEADS // SWA_KV_HEADS) * d
    kcol = SWA_HEADS
    vcol = SWA_HEADS + SWA_KV_HEADS
    nq = s_len // tq
    r = tq // w
    c0 = (b_sz * s_len) // cn
    smem = pl.BlockSpec(memory_space=pltpu.SMEM)

    def main(b, h, i):
        return b * nq + i

    def prev(b, h, i):
        return b * (s_len // w) + jnp.maximum(i * r - 1, 0)

    def nxt(b, h, i):
        return b * (s_len // w) + jnp.minimum((i + 1) * r, s_len // w - 1)

    o_lat = pl.pallas_call(
        functools.partial(_swa_kernel, tq=tq, s_len=s_len),
        grid=(b_sz, SWA_KV_HEADS, nq),
        in_specs=[
            smem,
            pl.BlockSpec((tq, gw), lambda b, h, i: (main(b, h, i), h)),
            pl.BlockSpec((w, d), lambda b, h, i: (prev(b, h, i), kcol + h)),
            pl.BlockSpec((tq, d), lambda b, h, i: (main(b, h, i), kcol + h)),
            pl.BlockSpec((w, d), lambda b, h, i: (nxt(b, h, i), kcol + h)),
            pl.BlockSpec((cn, d), lambda b, h, i: (c0 + b, kcol + h)),
            pl.BlockSpec((w, d), lambda b, h, i: (prev(b, h, i), vcol + h)),
            pl.BlockSpec((tq, d), lambda b, h, i: (main(b, h, i), vcol + h)),
            pl.BlockSpec((w, d), lambda b, h, i: (nxt(b, h, i), vcol + h)),
            pl.BlockSpec((cn, d), lambda b, h, i: (c0 + b, vcol + h)),
        ],
        out_specs=pl.BlockSpec((tq, gw), lambda b, h, i: (main(b, h, i), h)),
        out_shape=jax.ShapeDtypeStruct((b_sz * s_len, SWA_HEADS * d), BF16),
        name="swa_latent",
        compiler_params=_cparams(("arbitrary", "arbitrary", "arbitrary")),
    )(sink, a, a, a, a, a, a, a, a, a)

    o_ctx = pl.pallas_call(
        _swa_ctx_kernel,
        grid=(b_sz, SWA_KV_HEADS),
        in_specs=[
            smem,
            pl.BlockSpec((cn, gw), lambda b, h: (c0 + b, h)),
            pl.BlockSpec((cn, d), lambda b, h: (c0 + b, kcol + h)),
            pl.BlockSpec((cn, d), lambda b, h: (c0 + b, vcol + h)),
        ],
        out_specs=pl.BlockSpec((cn, gw), lambda b, h: (b, h)),
        out_shape=jax.ShapeDtypeStruct((b_sz * cn, SWA_HEADS * d), BF16),
        name="swa_context",
        compiler_params=_cparams(("arbitrary", "arbitrary")),
    )(sink, a, a, a)
    return o_lat, o_ctx


def _mla_kernel(q_ref, kc_ref, vc_ref, *refs, tk, latent):
    if latent:
        kl_ref, vl_ref, o_ref = refs
    else:
        (o_ref,) = refs
    dv = MLA_V_DIM
    q = q_ref[...]
    chunks = []
    if latent:
        chunks += [(kl_ref, vl_ref, slice(j * tk, (j + 1) * tk)) for j in range(kl_ref.shape[0] // tk)]
    chunks.append((kc_ref, vc_ref, slice(None)))
    m = acc = None
    for k_ref, v_ref, sl in chunks:
        s = _dot_nt(q, k_ref[sl, :])
        if m is None:
            m = jnp.max(s, axis=-1, keepdims=True)
            acc = _dot(jnp.exp2(s - m).astype(BF16), v_ref[sl, :])
        else:
            m_new = jnp.maximum(m, jnp.max(s, axis=-1, keepdims=True))
            acc = jnp.exp2(m - m_new) * acc + _dot(jnp.exp2(s - m_new).astype(BF16), v_ref[sl, :])
            m = m_new
    o_ref[...] = (acc[:, :dv] / acc[:, dv:]).astype(BF16)


def _mla_attention(qm, kcat, vext, b_sz, cn, s_len, tq, tk):
    hp = MLA_HEAD_PAD
    dv = MLA_V_DIM
    nq = s_len // tq
    c0 = (b_sz * s_len) // cn

    o_lat = pl.pallas_call(
        functools.partial(_mla_kernel, tk=tk, latent=True),
        grid=(b_sz, MLA_HEADS, nq),
        in_specs=[
            pl.BlockSpec((tq, hp), lambda b, h, i: (b * nq + i, h)),
            pl.BlockSpec((cn, hp), lambda b, h, i: (c0 + b, h)),
            pl.BlockSpec((cn, hp), lambda b, h, i: (c0 + b, h)),
            pl.BlockSpec((s_len, hp), lambda b, h, i: (b, h)),
            pl.BlockSpec((s_len, hp), lambda b, h, i: (b, h)),
        ],
        out_specs=pl.BlockSpec((tq, dv), lambda b, h, i: (b * nq + i, h)),
        out_shape=jax.ShapeDtypeStruct((b_sz * s_len, MLA_HEADS * dv), BF16),
        name="mla_latent",
        compiler_params=_cparams(("arbitrary",) * 3),
    )(qm, kcat, vext, kcat, vext)

    o_ctx = pl.pallas_call(
        functools.partial(_mla_kernel, tk=0, latent=False),
        grid=(b_sz, MLA_HEADS),
        in_specs=[
            pl.BlockSpec((cn, hp), lambda b, h: (c0 + b, h)),
            pl.BlockSpec((cn, hp), lambda b, h: (c0 + b, h)),
            pl.BlockSpec((cn, hp), lambda b, h: (c0 + b, h)),
        ],
        out_specs=pl.BlockSpec((cn, dv), lambda b, h: (b, h)),
        out_shape=jax.ShapeDtypeStruct((b_sz * cn, MLA_HEADS * dv), BF16),
        name="mla_context",
        compiler_params=_cparams(("arbitrary", "arbitrary")),
    )(qm, kcat, vext)
    return o_lat, o_ctx


def _out_proj_kernel(w_ref, mod_ref, *refs, n_parts, n_lat_tiles):
    o_ref = refs[-1]
    lat = refs[:n_parts + 1]
    ctx = refs[n_parts + 1:-1]

    def compute(side):
        acc = None
        k0 = 0
        for p in side[:-1]:
            kw = p.shape[1]
            t = _dot(p[...], w_ref[k0:k0 + kw, :])
            acc = t if acc is None else acc + t
            k0 += kw
        o_ref[...] = side[-1][...] + mod_ref[0, 0, 2:3, :] * acc

    if n_lat_tiles is None:
        compute(lat)
    else:
        i = pl.program_id(0)
        pl.when(i < n_lat_tiles)(lambda: compute(lat))
        pl.when(i >= n_lat_tiles)(lambda: compute(ctx))


def _out_proj(lat_parts, lat_res, ctx_parts, ctx_res, w, mod_all, layer, mrow, tm, tn):
    lat_rows = lat_parts[0].shape[0]
    d = lat_res.shape[1]
    kdim = w.shape[0]
    n_lat = lat_rows // tm
    if ctx_parts is None:
        n_lat_tiles = None
        rows = lat_rows
        specs = [pl.BlockSpec((tm, p.shape[1]), lambda i, j: (i, 0)) for p in lat_parts]
        specs.append(pl.BlockSpec((tm, tn), lambda i, j: (i, j)))
        operands = [*lat_parts, lat_res]
    else:
        n_lat_tiles = n_lat
        rows = lat_rows + ctx_parts[0].shape[0]
        is_lat = lambda i: i < n_lat
        lat_row = lambda i: jnp.minimum(i, n_lat - 1)
        ctx_row = lambda i: jnp.maximum(i - n_lat, 0)
        specs = [pl.BlockSpec((tm, p.shape[1]), lambda i, j: (lat_row(i), 0)) for p in lat_parts]
        specs.append(pl.BlockSpec((tm, tn), lambda i, j: (lat_row(i), jnp.where(is_lat(i), j, 0))))
        specs += [pl.BlockSpec((tm, p.shape[1]), lambda i, j: (ctx_row(i), 0)) for p in ctx_parts]
        specs.append(pl.BlockSpec((tm, tn), lambda i, j: (ctx_row(i), jnp.where(is_lat(i), 0, j))))
        operands = [*lat_parts, lat_res, *ctx_parts, ctx_res]
    return pl.pallas_call(
        functools.partial(_out_proj_kernel, n_parts=len(lat_parts), n_lat_tiles=n_lat_tiles),
        grid=(rows // tm, d // tn),
        in_specs=[
            pl.BlockSpec((kdim, tn), lambda i, j: (0, j)),
            pl.BlockSpec((1, 1, 6, tn), lambda i, j: (layer, mrow(i), 0, j)),
        ] + specs,
        out_specs=pl.BlockSpec((tm, tn), lambda i, j: (i, j)),
        out_shape=jax.ShapeDtypeStruct((rows, d), F32),
        name="out_proj",
        compiler_params=_cparams(("arbitrary", "arbitrary")),
    )(w, mod_all, *operands)


def _ffn_kernel(x_ref, g_ref, mod_ref, wg_ref, wu_ref, wd_ref, fg_ref, o_ref, h_ref, acc_ref, *, nf, final):
    j = pl.program_id(1)

    @pl.when(j == 0)
    def _():
        _norm_mod_into(h_ref, x_ref, g_ref, mod_ref, 3)
        acc_ref[...] = jnp.zeros_like(acc_ref)

    h = h_ref[...]
    a = (_silu(_dot(h, wg_ref[0])) * _dot(h, wu_ref[0])).astype(BF16)
    acc_ref[...] += _dot(a, wd_ref[0])

    @pl.when(j == nf - 1)
    def _():
        y = x_ref[...] + mod_ref[0, 0, 5:6, :] * acc_ref[...]
        if final:
            y = y * lax.rsqrt(jnp.mean(y * y, axis=-1, keepdims=True) + NORM_EPS) * fg_ref[...]
        o_ref[...] = y


def _ffn(xt, norm_g, mod_all, layer, wg, wu, wd, final_g, mrow, n_rows, tm, tf, final):
    d = xt.shape[1]
    f = wg.shape[2]
    nf = f // tf
    return pl.pallas_call(
        functools.partial(_ffn_kernel, nf=nf, final=final),
        grid=(n_rows // tm, nf),
        in_specs=[
            pl.BlockSpec((tm, d), lambda i, j: (i, 0)),
            pl.BlockSpec((1, d), lambda i, j: (0, 0)),
            pl.BlockSpec((1, 1, 6, d), lambda i, j: (layer, mrow(i), 0, 0)),
            pl.BlockSpec((1, d, tf), lambda i, j: (layer, 0, j)),
            pl.BlockSpec((1, d, tf), lambda i, j: (layer, 0, j)),
            pl.BlockSpec((1, tf, d), lambda i, j: (layer, j, 0)),
            pl.BlockSpec((1, d), lambda i, j: (0, 0)),
        ],
        out_specs=pl.BlockSpec((tm, d), lambda i, j: (i, 0)),
        out_shape=jax.ShapeDtypeStruct((n_rows, d), F32),
        scratch_shapes=[pltpu.VMEM((tm, d), BF16), pltpu.VMEM((tm, d), F32)],
        name="ffn_final" if final else "ffn",
        compiler_params=_cparams(("arbitrary", "arbitrary")),
    )(xt, norm_g, mod_all, wg, wu, wd, final_g)


def _ret_in_kernel(x_ref, g_ref, mod_ref, w_ref, cos_ref, sin_ref, o_ref, h_ref, *, n_qk, k_scale, tn):
    j = pl.program_id(1)

    pl.when(j == 0)(lambda: _norm_mod_into(h_ref, x_ref, g_ref, mod_ref, 0))

    def block(scale):
        h = h_ref[...]
        for sub in range(tn // SUB_N):
            o = sub * SUB_N
            z = _dot(h, w_ref[:, o:o + SUB_N])
            if scale is None:
                o_ref[:, o:o + SUB_N] = z.astype(BF16)
            else:
                c = cos_ref[...]
                s = sin_ref[...]
                x1 = z[:, :LANES] * scale if scale != 1.0 else z[:, :LANES]
                x2 = z[:, LANES:] * scale if scale != 1.0 else z[:, LANES:]
                o_ref[:, o:o + LANES] = (x1 * c - x2 * s).astype(BF16)
                o_ref[:, o + LANES:o + SUB_N] = (x1 * s + x2 * c).astype(BF16)

    pl.when(j < n_qk)(lambda: block(1.0))
    pl.when((j >= n_qk) & (j < 2 * n_qk))(lambda: block(k_scale))
    pl.when(j >= 2 * n_qk)(lambda: block(None))


def _ret_in_proj(xt, norm_g, mod_all, layer, w_in, cos_r, sin_r, mrow, tm, tn):
    nt, d = xt.shape
    n = w_in.shape[1]
    dk = d // RET_HEADS
    return pl.pallas_call(
        functools.partial(_ret_in_kernel, n_qk=d // tn, k_scale=dk ** -0.5, tn=tn),
        grid=(nt // tm, n // tn),
        in_specs=[
            pl.BlockSpec((tm, d), lambda i, j: (i, 0)),
            pl.BlockSpec((1, d), lambda i, j: (0, 0)),
            pl.BlockSpec((1, 1, 6, d), lambda i, j: (layer, mrow(i), 0, 0)),
            pl.BlockSpec((d, tn), lambda i, j: (0, j)),
            pl.BlockSpec((tm, LANES), lambda i, j: (i, 0)),
            pl.BlockSpec((tm, LANES), lambda i, j: (i, 0)),
        ],
        out_specs=pl.BlockSpec((tm, tn), lambda i, j: (i, j)),
        out_shape=jax.ShapeDtypeStruct((nt, n), BF16),
        scratch_shapes=[pltpu.VMEM((tm, d), BF16)],
        name="ret_in_proj",
        compiler_params=_cparams(("arbitrary", "arbitrary")),
    )(xt, norm_g, mod_all, w_in, cos_r, sin_r)


def _log_sigmoid(x):
    return jnp.minimum(x, 0.0) - jnp.log1p(jnp.exp(-jnp.abs(x)))


def _ret_kernel(lf_ref, lb_ref, gn_ref, qc_ref, kc_ref, vc_ref, q_ref, k_ref, v_ref, g_ref, y_ref,
                o_ref, s_ref, dint_ref, qdec_ref, kdec_ref, *, n_ctx, n_lat):
    c = RET_CHUNK
    dk = q_ref.shape[1]
    dv = v_ref.shape[1]
    row = lax.broadcasted_iota(jnp.int32, (c, c), 0)
    col = lax.broadcasted_iota(jnp.int32, (c, c), 1)
    pos = lax.broadcasted_iota(jnp.int32, (c, LANES), 0).astype(F32)
    c_dec = []
    for drn, l_ref in enumerate((lf_ref, lb_ref)):
        lg = _log_sigmoid(l_ref[0])
        lg_c = jnp.broadcast_to(lg[:, 0:1], (1, c))
        diff = (col - row) if drn else (row - col)
        dint_ref[drn] = jnp.where(diff >= 0, jnp.exp(lg_c * jnp.maximum(diff, 0).astype(F32)), 0.0)
        if drn:
            qdec_ref[drn] = jnp.exp(lg * (c - pos))
            kdec_ref[drn] = jnp.exp(lg * pos)
        else:
            qdec_ref[drn] = jnp.exp(lg * (pos + 1.0))
            kdec_ref[drn] = jnp.exp(lg * (c - 1.0 - pos))
        c_dec.append(jnp.exp(jnp.broadcast_to(lg[:, 0:1], (1, dv)) * c))

    def scale_rows(x, dec):
        return jnp.concatenate([x[:, k * LANES:(k + 1) * LANES].astype(F32) * dec for k in range(dk // LANES)],
                               axis=1).astype(BF16)

    def step(drn, q, k, v, want_out):
        s = s_ref[drn]
        o = None
        if want_out:
            a = (_dot_nt(q, k) * dint_ref[drn]).astype(BF16)
            o = _dot(a, v) + _dot(scale_rows(q, qdec_ref[drn]), s.astype(BF16))
        s_ref[drn] = s * c_dec[drn] + _dot_tn(scale_rows(k, kdec_ref[drn]), v)
        return o

    def rows(t):
        return pl.ds(pl.multiple_of(t * c, c), c)

    def lat_step(drn, t):
        sl = rows(t)
        return step(drn, q_ref[sl, :], k_ref[sl, :], v_ref[sl, :], True)

    def finish(t, o):
        sl = rows(t)
        mu = jnp.mean(o, axis=-1, keepdims=True)
        oc = o - mu
        var = jnp.mean(oc * oc, axis=-1, keepdims=True)
        yn = oc * lax.rsqrt(var + NORM_EPS) * gn_ref[...]
        y_ref[sl, :] = (_silu(g_ref[sl, :].astype(F32)) * yn).astype(BF16)

    s_ref[...] = jnp.zeros_like(s_ref)
    for t in range(n_ctx):
        for drn, tt in ((0, t), (1, n_ctx - 1 - t)):
            sl = slice(tt * c, (tt + 1) * c)
            step(drn, qc_ref[sl, :], kc_ref[sl, :], vc_ref[sl, :], False)

    half = n_lat // 2

    def first_half(t, carry):
        tb = n_lat - 1 - t
        o_ref[rows(t), :] = lat_step(0, t)
        o_ref[rows(tb), :] = lat_step(1, tb)
        return carry

    def second_half(t, carry):
        tb = n_lat - 1 - t
        finish(t, o_ref[rows(t), :] + lat_step(0, t))
        finish(tb, o_ref[rows(tb), :] + lat_step(1, tb))
        return carry

    lax.fori_loop(0, half, first_half, 0)
    lax.fori_loop(half, n_lat, second_half, 0)


def _retention(r, lg_f, lg_b, gn_g, b_sz, cn, s_len, d):
    h_n = RET_HEADS
    dk = d // h_n
    dv = 2 * dk
    c = RET_CHUNK
    assert cn % c == 0 and s_len % (2 * c) == 0
    c0 = (b_sz * s_len) // cn
    kq, kk, kv, kg = 0, d // dk, (2 * d) // dv, (4 * d) // dv

    def lg_spec():
        return pl.BlockSpec((1, 1, LANES), lambda b, h: (h, 0, 0))

    return pl.pallas_call(
        functools.partial(_ret_kernel, n_ctx=cn // c, n_lat=s_len // c),
        grid=(b_sz, h_n),
        in_specs=[
            lg_spec(), lg_spec(),
            pl.BlockSpec((1, dv), lambda b, h: (0, h)),
            pl.BlockSpec((cn, dk), lambda b, h: (c0 + b, kq + h)),
            pl.BlockSpec((cn, dk), lambda b, h: (c0 + b, kk + h)),
            pl.BlockSpec((cn, dv), lambda b, h: (c0 + b, kv + h)),
            pl.BlockSpec((s_len, dk), lambda b, h: (b, kq + h)),
            pl.BlockSpec((s_len, dk), lambda b, h: (b, kk + h)),
            pl.BlockSpec((s_len, dv), lambda b, h: (b, kv + h)),
            pl.BlockSpec((s_len, dv), lambda b, h: (b, kg + h)),
        ],
        out_specs=pl.BlockSpec((s_len, dv), lambda b, h: (b, h)),
        out_shape=jax.ShapeDtypeStruct((b_sz * s_len, h_n * dv), BF16),
        scratch_shapes=[
            pltpu.VMEM((s_len, dv), F32),
            pltpu.VMEM((2, dk, dv), F32),
            pltpu.VMEM((2, c, c), F32),
            pltpu.VMEM((2, c, LANES), F32),
            pltpu.VMEM((2, c, LANES), F32),
        ],
        name="retention",
        compiler_params=_cparams(("arbitrary", "arbitrary")),
    )(lg_f, lg_b, gn_g, r, r, r, r, r, r, r)


def _rope_angles(s_len, rot_dim):
    rows = s_len // GRID_W
    row = jnp.repeat(jnp.arange(rows, dtype=F32), GRID_W)
    col = (jnp.arange(s_len) % GRID_W).astype(F32)
    n_freq = rot_dim // 4
    inv = ROPE_BASE ** (-jnp.arange(n_freq, dtype=F32) / n_freq)
    ang = jnp.concatenate([row[:, None] * inv, col[:, None] * inv], axis=-1)
    return jnp.cos(ang), jnp.sin(ang)


def _token_table(lat_table, ctx_row, b_sz, n_ctx_rows):
    width = lat_table.shape[1]
    return jnp.concatenate([jnp.tile(lat_table, (b_sz, 1)),
                            jnp.broadcast_to(ctx_row[None, :], (n_ctx_rows, width))], axis=0)


def kernel(x, c, ctx, c_ctx, mod_w, mod_b, norm_mix_g, norm_ffn_g, ffn_w_gate, ffn_w_up, ffn_w_down, ab_w_in, ab_w_out, swa_sink, mla_q_norm_g, mla_w_q_b, mla_kv_norm_g, mla_w_kv_b, ret_w_in, ret_decay_logit_fwd, ret_decay_logit_bwd, ret_gn_g, ret_w_out, final_norm_g):
    b_sz, s_len, d = x.shape
    cn = ctx.shape[1]
    depth = mod_w.shape[0]
    assert depth == 2 and ab_w_in.shape[0] == 1 and ret_w_in.shape[0] == 1
    assert b_sz + 1 <= 8
    n_lat_rows = b_sz * s_len
    n_ctx_rows = b_sz * cn
    nt = n_lat_rows + n_ctx_rows

    tm = min(1024, n_ctx_rows, s_len)
    tm_ffn = min(512, tm)
    tq = min(512, s_len)
    tk = min(1024, s_len)
    assert n_ctx_rows % tm == 0 and s_len % tm == 0 and s_len % cn == 0 and s_len % tq == 0 and s_len % tk == 0

    def make_mrow(t):
        n_lat_tiles = n_lat_rows // t
        per_b = s_len // t
        return lambda i: jnp.where(i < n_lat_tiles, 1 + i // per_b, 0)

    mrow = make_mrow(tm)
    mrow_ffn = make_mrow(tm_ffn)

    x2d = x.reshape(n_lat_rows, d)
    ctx2d = ctx.reshape(n_ctx_rows, d)

    cond8 = jnp.zeros((8, d), F32).at[0].set(c_ctx).at[1:1 + b_sz].set(c)
    mod_all = _modulation(cond8, mod_w, mod_b).reshape(depth, 8, 6, d)

    cos_a, sin_a = _rope_angles(s_len, SWA_HEAD_DIM)
    cos_b, sin_b = _rope_angles(s_len, MLA_ROPE_DIM)
    cos_r, sin_r = _rope_angles(s_len, d // RET_HEADS)
    ones = jnp.ones((LANES,), F32)
    zeros = jnp.zeros((LANES,), F32)
    half = jnp.concatenate([jnp.ones((64,), F32), jnp.zeros((64,), F32)])
    z64 = jnp.zeros((s_len, 64), F32)
    table = functools.partial(_token_table, b_sz=b_sz, n_ctx_rows=n_ctx_rows)
    t_cos_a = table(jnp.concatenate([cos_a, cos_a], axis=1), ones)
    t_sin_a = table(jnp.concatenate([-sin_a, sin_a], axis=1), zeros)
    t_cos_b = table(jnp.concatenate([cos_b, cos_b, z64], axis=1), half)
    t_sin_b = table(jnp.concatenate([-sin_b, sin_b, z64], axis=1), zeros)
    t_cos_r = table(cos_r, ones)
    t_sin_r = table(sin_r, zeros)

    bf = lambda w: w.astype(BF16)
    w_ab_in = jnp.pad(ab_w_in[0], ((0, 0), (0, 2560 - ab_w_in.shape[2]))).astype(BF16)
    wq_b = jnp.pad(mla_w_q_b[0].reshape(MLA_Q_RANK, MLA_HEADS, MLA_NOPE_DIM + MLA_ROPE_DIM),
                   ((0, 0), (0, 0), (0, MLA_HEAD_PAD - MLA_NOPE_DIM - MLA_ROPE_DIM))
                   ).reshape(MLA_Q_RANK, MLA_HEADS * MLA_HEAD_PAD).astype(BF16)
    wkv_b = bf(mla_w_kv_b[0])
    wg, wu, wd = bf(ffn_w_gate), bf(ffn_w_up), bf(ffn_w_down)
    final_g = final_norm_g[None, :]

    a_qkv, lat = _ab_in_proj(x2d, ctx2d, norm_mix_g[0:1], mod_all, 0, w_ab_in, t_cos_a, t_sin_a, mrow, tm)
    qm, kcat, vext = _mla_proj(lat, mla_q_norm_g[0:1], mla_kv_norm_g[0:1], wq_b, wkv_b, t_cos_b, t_sin_b, tm)
    oa, oa_c = _swa_attention(a_qkv, swa_sink[0], b_sz, cn, s_len, tq)
    ob, ob_c = _mla_attention(qm, kcat, vext, b_sz, cn, s_len, min(1024, s_len), tk)
    xt = _out_proj([oa, ob], x2d, [oa_c, ob_c], ctx2d, bf(ab_w_out[0]), mod_all, 0, mrow, tm, 512)
    xt = _ffn(xt, norm_ffn_g[0:1], mod_all, 0, wg, wu, wd, final_g, mrow_ffn, nt, tm_ffn, 512, final=False)

    r = _ret_in_proj(xt, norm_mix_g[1:2], mod_all, 1, bf(ret_w_in[0]), t_cos_r, t_sin_r, mrow, tm, 1024)
    lg_shape = (RET_HEADS, 1, LANES)
    lg_f = jnp.broadcast_to(ret_decay_logit_fwd[0].astype(F32)[:, None, None], lg_shape)
    lg_b = jnp.broadcast_to(ret_decay_logit_bwd[0].astype(F32)[:, None, None], lg_shape)
    y = _retention(r, lg_f, lg_b, ret_gn_g[0:1], b_sz, cn, s_len, d)
    xl = _out_proj([y], xt, None, None, bf(ret_w_out[0]), mod_all, 1, mrow, tm, 512)
    out = _ffn(xl, norm_ffn_g[1:2], mod_all, 1, wg, wu, wd, final_g, mrow_ffn, n_lat_rows, tm_ffn, 512, final=True)
    return out.reshape(b_sz, s_len, d)
```

```python
import functools

import jax
import jax.numpy as jnp
from jax import lax
from jax.experimental import pallas as pl
from jax.experimental.pallas import tpu as pltpu

GRID_W = 64
ROPE_BASE = 10000.0
NORM_EPS = 1e-6
NEG_INF = -1e30
LOG2_E = 1.4426950408889634

SWA_HEADS = 8
SWA_KV_HEADS = 2
SWA_HEAD_DIM = 128
SWA_WINDOW = 128
SWA_QSCALE = SWA_HEAD_DIM ** -0.5 * LOG2_E

MLA_HEADS = 8
MLA_Q_RANK = 512
MLA_KV_RANK = 256
MLA_NOPE_DIM = 128
MLA_ROPE_DIM = 64
MLA_V_DIM = 128
MLA_HEAD_PAD = 256
MLA_QSCALE = (MLA_NOPE_DIM + MLA_ROPE_DIM) ** -0.5 * LOG2_E

RET_HEADS = 8
RET_CHUNK = 256

LANES = 128
V7X_VMEM_BYTES = 64 * 1024 * 1024
VMEM_LIMIT = V7X_VMEM_BYTES - 8 * 1024 * 1024

F32 = jnp.float32
BF16 = jnp.bfloat16


def _cparams(sem):
    return pltpu.CompilerParams(dimension_semantics=sem, vmem_limit_bytes=VMEM_LIMIT)


def _dot(a, b):
    return jnp.dot(a, b, preferred_element_type=F32)


def _dot_nt(a, b):
    return lax.dot_general(a, b, (((1,), (1,)), ((), ())), preferred_element_type=F32)


def _dot_tn(a, b):
    return lax.dot_general(a, b, (((0,), (0,)), ((), ())), preferred_element_type=F32)


def _silu(x):
    h = 0.5 * x
    return h + h * jnp.tanh(h)


NORM_ROWS = 32
SUB_N = 256


def _norm_mod_into(h_ref, x_ref, g_ref, mod_ref, shift_row):
    shift = mod_ref[0, 0, shift_row:shift_row + 1, :]
    gain = g_ref[...] * (1.0 + mod_ref[0, 0, shift_row + 1:shift_row + 2, :])

    def body(t, carry):
        sl = pl.ds(pl.multiple_of(t * NORM_ROWS, NORM_ROWS), NORM_ROWS)
        x = x_ref[sl, :]
        inv = lax.rsqrt(jnp.mean(x * x, axis=-1, keepdims=True) + NORM_EPS)
        h_ref[sl, :] = ((x * inv) * gain + shift).astype(BF16)
        return carry

    lax.fori_loop(0, x_ref.shape[0] // NORM_ROWS, body, 0, unroll=4)


def _split_rows(n_lat_tiles):
    lat = lambda i, j: (jnp.minimum(i, n_lat_tiles - 1), 0)
    ctx = lambda i, j: (jnp.maximum(i - n_lat_tiles, 0), 0)
    return lat, ctx


def _mod_kernel(c_ref, w_ref, b_ref, o_ref):
    a = _silu(c_ref[...]).astype(BF16)
    o_ref[0] = _dot(a, w_ref[0].astype(BF16)) + b_ref[0]


def _modulation(cond8, mod_w, mod_b):
    depth, d, n = mod_w.shape
    tn = 1024
    return pl.pallas_call(
        _mod_kernel,
        grid=(depth, n // tn),
        in_specs=[
            pl.BlockSpec((8, d), lambda l, j: (0, 0)),
            pl.BlockSpec((1, d, tn), lambda l, j: (l, 0, j)),
            pl.BlockSpec((1, 1, tn), lambda l, j: (l, 0, j)),
        ],
        out_specs=pl.BlockSpec((1, 8, tn), lambda l, j: (l, 0, j)),
        out_shape=jax.ShapeDtypeStruct((depth, 8, n), F32),
        name="modulation",
        compiler_params=_cparams(("arbitrary", "arbitrary")),
    )(cond8, mod_w, mod_b.reshape(depth, 1, n))


def _rope128(z, cosf, sins):
    return z * cosf + pltpu.roll(z, 64, axis=1) * sins


def _ab_in_kernel(x_ref, c_ref, g_ref, mod_ref, w_ref, cos_ref, sin_ref, a_ref, l_ref, h_ref, *, n_lat_tiles):
    i = pl.program_id(0)
    j = pl.program_id(1)

    pl.when((j == 0) & (i < n_lat_tiles))(lambda: _norm_mod_into(h_ref, x_ref, g_ref, mod_ref, 0))
    pl.when((j == 0) & (i >= n_lat_tiles))(lambda: _norm_mod_into(h_ref, c_ref, g_ref, mod_ref, 0))

    def block(kinds):
        h = h_ref[...]
        for half, kind in enumerate(kinds):
            z = _dot(h, w_ref[:, half * SUB_N:(half + 1) * SUB_N])
            for k in range(SUB_N // LANES):
                sl = slice(half * SUB_N + k * LANES, half * SUB_N + (k + 1) * LANES)
                zk = z[:, k * LANES:(k + 1) * LANES]
                if kind == "q":
                    a_ref[:, sl] = (_rope128(zk, cos_ref[...], sin_ref[...]) * SWA_QSCALE).astype(BF16)
                elif kind == "k":
                    a_ref[:, sl] = _rope128(zk, cos_ref[...], sin_ref[...]).astype(BF16)
                elif kind == "v":
                    a_ref[:, sl] = zk.astype(BF16)
                else:
                    l_ref[:, sl] = zk

    pl.when(j <= 1)(lambda: block(("q", "q")))
    pl.when(j == 2)(lambda: block(("k", "v")))
    pl.when(j >= 3)(lambda: block(("f32", "f32")))


def _ab_in_proj(x2d, ctx2d, norm_g, mod_all, layer, w_in, cos_a, sin_a, mrow, tm):
    d = x2d.shape[1]
    n_lat_tiles = x2d.shape[0] // tm
    nt = x2d.shape[0] + ctx2d.shape[0]
    tn = 512
    lat_map, ctx_map = _split_rows(n_lat_tiles)
    return pl.pallas_call(
        functools.partial(_ab_in_kernel, n_lat_tiles=n_lat_tiles),
        grid=(nt // tm, 5),
        in_specs=[
            pl.BlockSpec((tm, d), lat_map),
            pl.BlockSpec((tm, d), ctx_map, pipeline_mode=pl.Buffered(1)),
            pl.BlockSpec((1, d), lambda i, j: (0, 0)),
            pl.BlockSpec((1, 1, 6, d), lambda i, j: (layer, mrow(i), 0, 0)),
            pl.BlockSpec((d, tn), lambda i, j: (0, j)),
            pl.BlockSpec((tm, LANES), lambda i, j: (i, 0)),
            pl.BlockSpec((tm, LANES), lambda i, j: (i, 0)),
        ],
        out_specs=[
            pl.BlockSpec((tm, tn), lambda i, j: (i, jnp.minimum(j, 2))),
            pl.BlockSpec((tm, tn), lambda i, j: (i, jnp.maximum(j - 3, 0))),
        ],
        out_shape=[
            jax.ShapeDtypeStruct((nt, 3 * tn), BF16),
            jax.ShapeDtypeStruct((nt, 2 * tn), F32),
        ],
        scratch_shapes=[pltpu.VMEM((tm, d), BF16)],
        name="ab_in_proj",
        compiler_params=_cparams(("arbitrary", "arbitrary")),
    )(x2d, ctx2d, norm_g, mod_all, w_in, cos_a, sin_a)


def _rope64(r, cosp, sinp):
    lane = lax.broadcasted_iota(jnp.int32, r.shape, 1)
    partner = jnp.where(lane < 32, pltpu.roll(r, 96, axis=1), pltpu.roll(r, 32, axis=1))
    return r * cosp + partner * sinp


def _mla_proj_kernel(l_ref, qg_ref, kvg_ref, wq_ref, wkv_ref, cos_ref, sin_ref, qm_ref, kc_ref, vm_ref):
    cosp = cos_ref[...]
    sinp = sin_ref[...]

    def rms(x, g):
        return x * lax.rsqrt(jnp.mean(x * x, axis=-1, keepdims=True) + NORM_EPS) * g

    qn = rms(l_ref[:, :MLA_Q_RANK], qg_ref[...]).astype(BF16)
    qm = _dot(qn, wq_ref[...])
    kvn = rms(l_ref[:, MLA_Q_RANK:MLA_Q_RANK + MLA_KV_RANK], kvg_ref[...]).astype(BF16)
    kv = _dot(kvn, wkv_ref[...])
    kr = l_ref[:, MLA_Q_RANK + MLA_KV_RANK:MLA_Q_RANK + MLA_KV_RANK + LANES]
    krr = _rope64(kr, cosp, sinp).astype(BF16)
    ones = jnp.ones((l_ref.shape[0], LANES), BF16)
    for h in range(MLA_HEADS):
        o = h * MLA_HEAD_PAD
        qm_ref[:, o:o + LANES] = (qm[:, o:o + LANES] * MLA_QSCALE).astype(BF16)
        qm_ref[:, o + LANES:o + 2 * LANES] = (
            _rope64(qm[:, o + LANES:o + 2 * LANES], cosp, sinp) * MLA_QSCALE).astype(BF16)
        kc_ref[:, o:o + LANES] = kv[:, o:o + LANES].astype(BF16)
        kc_ref[:, o + LANES:o + 2 * LANES] = krr
        vm_ref[:, o:o + LANES] = kv[:, o + LANES:o + 2 * LANES].astype(BF16)
        vm_ref[:, o + LANES:o + 2 * LANES] = ones


def _mla_proj(lat, q_norm_g, kv_norm_g, wq, wkv, cos_b, sin_b, tm):
    nt = lat.shape[0]
    hp = MLA_HEADS * MLA_HEAD_PAD
    return pl.pallas_call(
        _mla_proj_kernel,
        grid=(nt // tm,),
        in_specs=[
            pl.BlockSpec((tm, lat.shape[1]), lambda i: (i, 0)),
            pl.BlockSpec((1, MLA_Q_RANK), lambda i: (0, 0)),
            pl.BlockSpec((1, MLA_KV_RANK), lambda i: (0, 0)),
            pl.BlockSpec(wq.shape, lambda i: (0, 0)),
            pl.BlockSpec(wkv.shape, lambda i: (0, 0)),
            pl.BlockSpec((tm, LANES), lambda i: (i, 0)),
            pl.BlockSpec((tm, LANES), lambda i: (i, 0)),
        ],
        out_specs=[
            pl.BlockSpec((tm, hp), lambda i: (i, 0)),
            pl.BlockSpec((tm, hp), lambda i: (i, 0)),
            pl.BlockSpec((tm, hp), lambda i: (i, 0)),
        ],
        out_shape=[
            jax.ShapeDtypeStruct((nt, hp), BF16),
            jax.ShapeDtypeStruct((nt, hp), BF16),
            jax.ShapeDtypeStruct((nt, hp), BF16),
        ],
        name="mla_proj",
        compiler_params=_cparams(("arbitrary",)),
    )(lat, q_norm_g, kv_norm_g, wq, wkv, cos_b, sin_b)


def _swa_softmax_pv(s, sk, v_ext):
    d = SWA_HEAD_DIM
    m = jnp.maximum(jnp.max(s, axis=-1, keepdims=True), sk)
    o_ext = _dot(jnp.exp2(s - m).astype(BF16), v_ext)
    return o_ext[:, :d] / (o_ext[:, d:] + jnp.exp2(sk - m))


def _swa_ctx_kernel(sink_ref, q_ref, kc, vc, o_ref):
    kvh = pl.program_id(1)
    d = SWA_HEAD_DIM
    g_heads = SWA_HEADS // SWA_KV_HEADS
    k_all = kc[...]
    v_ext = jnp.concatenate([vc[...], jnp.ones_like(vc)], axis=1)
    for g in range(g_heads):
        sk = sink_ref[kvh * g_heads + g] * LOG2_E
        s = _dot_nt(q_ref[:, g * d:(g + 1) * d], k_all)
        o_ref[:, g * d:(g + 1) * d] = _swa_softmax_pv(s, sk, v_ext).astype(BF16)


def _swa_kernel(sink_ref, q_ref, kp, km, kn, kc, vp, vm, vn, vc, o_ref, *, tq, s_len):
    kvh = pl.program_id(1)
    i = pl.program_id(2)
    d = SWA_HEAD_DIM
    g_heads = SWA_HEADS // SWA_KV_HEADS
    k_all = jnp.concatenate([kp[...], km[...], kn[...], kc[...]], axis=0)
    v_all = jnp.concatenate([vp[...], vm[...], vn[...], vc[...]], axis=0)
    v_ext = jnp.concatenate([v_all, jnp.ones_like(v_all)], axis=1)
    nb = tq + 2 * SWA_WINDOW
    shape = (tq, k_all.shape[0])
    row = lax.broadcasted_iota(jnp.int32, shape, 0)
    col = lax.broadcasted_iota(jnp.int32, shape, 1)
    kpos = i * tq - SWA_WINDOW + col
    valid = (col >= nb) | ((jnp.abs(col - SWA_WINDOW - row) <= SWA_WINDOW) & (kpos >= 0) & (kpos < s_len))
    for g in range(g_heads):
        sk = sink_ref[kvh * g_heads + g] * LOG2_E
        s = jnp.where(valid, _dot_nt(q_ref[:, g * d:(g + 1) * d], k_all), NEG_INF)
        o_ref[:, g * d:(g + 1) * d] = _swa_softmax_pv(s, sk, v_ext).astype(BF16)


def _swa_attention(a, sink, b_sz, cn, s_len, tq):
    d = SWA_HEAD_DIM
    w = SWA_WINDOW
    gw = (SWA_HEADS // SWA_KV_HEADS) * d
    kcol = SWA_HEADS
    vcol = SWA_HEADS + SWA_KV_HEADS
    nq = s_len // tq
    r = tq // w
    c0 = (b_sz * s_len) // cn
    smem = pl.BlockSpec(memory_space=pltpu.SMEM)

    def main(b, h, i):
        return b * nq + i

    def prev(b, h, i):
        return b * (s_len // w) + jnp.maximum(i * r - 1, 0)

    def nxt(b, h, i):
        return b * (s_len // w) + jnp.minimum((i + 1) * r, s_len // w - 1)

    o_lat = pl.pallas_call(
        functools.partial(_swa_kernel, tq=tq, s_len=s_len),
        grid=(b_sz, SWA_KV_HEADS, nq),
        in_specs=[
            smem,
            pl.BlockSpec((tq, gw), lambda b, h, i: (main(b, h, i), h)),
            pl.BlockSpec((w, d), lambda b, h, i: (prev(b, h, i), kcol + h)),
            pl.BlockSpec((tq, d), lambda b, h, i: (main(b, h, i), kcol + h)),
            pl.BlockSpec((w, d), lambda b, h, i: (nxt(b, h, i), kcol + h)),
            pl.BlockSpec((cn, d), lambda b, h, i: (c0 + b, kcol + h)),
            pl.BlockSpec((w, d), lambda b, h, i: (prev(b, h, i), vcol + h)),
            pl.BlockSpec((tq, d), lambda b, h, i: (main(b, h, i), vcol + h)),
            pl.BlockSpec((w, d), lambda b, h, i: (nxt(b, h, i), vcol + h)),
            pl.BlockSpec((cn, d), lambda b, h, i: (c0 + b, vcol + h)),
        ],
        out_specs=pl.BlockSpec((tq, gw), lambda b, h, i: (main(b, h, i), h)),
        out_shape=jax.ShapeDtypeStruct((b_sz * s_len, SWA_HEADS * d), BF16),
        name="swa_latent",
        compiler_params=_cparams(("arbitrary", "arbitrary", "arbitrary")),
    )(sink, a, a, a, a, a, a, a, a, a)

    o_ctx = pl.pallas_call(
        _swa_ctx_kernel,
        grid=(b_sz, SWA_KV_HEADS),
        in_specs=[
            smem,
            pl.BlockSpec((cn, gw), lambda b, h: (c0 + b, h)),
            pl.BlockSpec((cn, d), lambda b, h: (c0 + b, kcol + h)),
            pl.BlockSpec((cn, d), lambda b, h: (c0 + b, vcol + h)),
        ],
        out_specs=pl.BlockSpec((cn, gw), lambda b, h: (b, h)),
        out_shape=jax.ShapeDtypeStruct((b_sz * cn, SWA_HEADS * d), BF16),
        name="swa_context",
        compiler_params=_cparams(("arbitrary", "arbitrary")),
    )(sink, a, a, a)
    return o_lat, o_ctx


def _mla_kernel(q_ref, kc_ref, vc_ref, *refs, tk, latent):
    if latent:
        kl_ref, vl_ref, o_ref = refs
    else:
        (o_ref,) = refs
    dv = MLA_V_DIM
    q = q_ref[...]
    chunks = []
    if latent:
        chunks += [(kl_ref, vl_ref, slice(j * tk, (j + 1) * tk)) for j in range(kl_ref.shape[0] // tk)]
    chunks.append((kc_ref, vc_ref, slice(None)))
    m = acc = None
    for k_ref, v_ref, sl in chunks:
        s = _dot_nt(q, k_ref[sl, :])
        if m is None:
            m = jnp.max(s, axis=-1, keepdims=True)
            acc = _dot(jnp.exp2(s - m).astype(BF16), v_ref[sl, :])
        else:
            m_new = jnp.maximum(m, jnp.max(s, axis=-1, keepdims=True))
            acc = jnp.exp2(m - m_new) * acc + _dot(jnp.exp2(s - m_new).astype(BF16), v_ref[sl, :])
            m = m_new
    o_ref[...] = (acc[:, :dv] / acc[:, dv:]).astype(BF16)


def _mla_attention(qm, kcat, vext, b_sz, cn, s_len, tq, tk):
    hp = MLA_HEAD_PAD
    dv = MLA_V_DIM
    nq = s_len // tq
    c0 = (b_sz * s_len) // cn

    o_lat = pl.pallas_call(
        functools.partial(_mla_kernel, tk=tk, latent=True),
        grid=(b_sz, MLA_HEADS, nq),
        in_specs=[
            pl.BlockSpec((tq, hp), lambda b, h, i: (b * nq + i, h)),
            pl.BlockSpec((cn, hp), lambda b, h, i: (c0 + b, h)),
            pl.BlockSpec((cn, hp), lambda b, h, i: (c0 + b, h)),
            pl.BlockSpec((s_len, hp), lambda b, h, i: (b, h)),
            pl.BlockSpec((s_len, hp), lambda b, h, i: (b, h)),
        ],
        out_specs=pl.BlockSpec((tq, dv), lambda b, h, i: (b * nq + i, h)),
        out_shape=jax.ShapeDtypeStruct((b_sz * s_len, MLA_HEADS * dv), BF16),
        name="mla_latent",
        compiler_params=_cparams(("arbitrary",) * 3),
    )(qm, kcat, vext, kcat, vext)

    o_ctx = pl.pallas_call(
        functools.partial(_mla_kernel, tk=0, latent=False),
        grid=(b_sz, MLA_HEADS),
        in_specs=[
            pl.BlockSpec((cn, hp), lambda b, h: (c0 + b, h)),
            pl.BlockSpec((cn, hp), lambda b, h: (c0 + b, h)),
            pl.BlockSpec((cn, hp), lambda b, h: (c0 + b, h)),
        ],
        out_specs=pl.BlockSpec((cn, dv), lambda b, h: (b, h)),
        out_shape=jax.ShapeDtypeStruct((b_sz * cn, MLA_HEADS * dv), BF16),
        name="mla_context",
        compiler_params=_cparams(("arbitrary", "arbitrary")),
    )(qm, kcat, vext)
    return o_lat, o_ctx


def _out_proj_kernel(w_ref, mod_ref, *refs, n_parts, n_lat_tiles):
    o_ref = refs[-1]
    lat = refs[:n_parts + 1]
    ctx = refs[n_parts + 1:-1]

    def compute(side):
        parts = [p[...] for p in side[:-1]]
        res_ref = side[-1]
        for sub in range(o_ref.shape[1] // SUB_N):
            cols = slice(sub * SUB_N, (sub + 1) * SUB_N)
            acc = None
            k0 = 0
            for p in parts:
                kw = p.shape[1]
                t = _dot(p, w_ref[k0:k0 + kw, cols])
                acc = t if acc is None else acc + t
                k0 += kw
            o_ref[:, cols] = res_ref[:, cols] + mod_ref[0, 0, 2:3, cols] * acc

    if n_lat_tiles is None:
        compute(lat)
    else:
        i = pl.program_id(0)
        pl.when(i < n_lat_tiles)(lambda: compute(lat))
        pl.when(i >= n_lat_tiles)(lambda: compute(ctx))


def _out_proj(lat_parts, lat_res, ctx_parts, ctx_res, w, mod_all, layer, mrow, tm):
    lat_rows = lat_parts[0].shape[0]
    d = lat_res.shape[1]
    n_lat = lat_rows // tm
    if ctx_parts is None:
        n_lat_tiles = None
        rows = lat_rows
        specs = [pl.BlockSpec((tm, p.shape[1]), lambda i: (i, 0)) for p in lat_parts]
        specs.append(pl.BlockSpec((tm, d), lambda i: (i, 0)))
        operands = [*lat_parts, lat_res]
    else:
        n_lat_tiles = n_lat
        rows = lat_rows + ctx_parts[0].shape[0]
        lat_map = lambda i: (jnp.minimum(i, n_lat - 1), 0)
        ctx_map = lambda i: (jnp.maximum(i - n_lat, 0), 0)
        specs = [pl.BlockSpec((tm, p.shape[1]), lat_map) for p in lat_parts]
        specs.append(pl.BlockSpec((tm, d), lat_map))
        specs += [pl.BlockSpec((tm, p.shape[1]), ctx_map) for p in ctx_parts]
        specs.append(pl.BlockSpec((tm, d), ctx_map))
        operands = [*lat_parts, lat_res, *ctx_parts, ctx_res]
    return pl.pallas_call(
        functools.partial(_out_proj_kernel, n_parts=len(lat_parts), n_lat_tiles=n_lat_tiles),
        grid=(rows // tm,),
        in_specs=[
            pl.BlockSpec(w.shape, lambda i: (0, 0), pipeline_mode=pl.Buffered(1)),
            pl.BlockSpec((1, 1, 6, d), lambda i: (layer, mrow(i), 0, 0)),
        ] + specs,
        out_specs=pl.BlockSpec((tm, d), lambda i: (i, 0)),
        out_shape=jax.ShapeDtypeStruct((rows, d), F32),
        name="out_proj",
        compiler_params=_cparams(("arbitrary",)),
    )(w, mod_all, *operands)


def _ffn_kernel(x_ref, g_ref, mod_ref, wg_ref, wu_ref, wd_ref, fg_ref, o_ref, h_ref, *, nf, final):
    j = pl.program_id(1)

    @pl.when(j == 0)
    def _():
        _norm_mod_into(h_ref, x_ref, g_ref, mod_ref, 3)
        o_ref[...] = jnp.zeros_like(o_ref)

    h = h_ref[...]
    a = (_silu(_dot(h, wg_ref[0].astype(BF16))) * _dot(h, wu_ref[0].astype(BF16))).astype(BF16)
    o_ref[...] += _dot(a, wd_ref[0].astype(BF16))

    @pl.when(j == nf - 1)
    def _():
        gate = mod_ref[0, 0, 5:6, :]

        def body(t, carry):
            sl = pl.ds(pl.multiple_of(t * NORM_ROWS, NORM_ROWS), NORM_ROWS)
            y = x_ref[sl, :] + gate * o_ref[sl, :]
            if final:
                y = y * lax.rsqrt(jnp.mean(y * y, axis=-1, keepdims=True) + NORM_EPS) * fg_ref[...]
            o_ref[sl, :] = y
            return carry

        lax.fori_loop(0, x_ref.shape[0] // NORM_ROWS, body, 0, unroll=4)


def _ffn(xt, norm_g, mod_all, layer, wg, wu, wd, final_g, mrow, n_rows, tm, tf, final):
    d = xt.shape[1]
    f = wg.shape[2]
    nf = f // tf
    return pl.pallas_call(
        functools.partial(_ffn_kernel, nf=nf, final=final),
        grid=(n_rows // tm, nf),
        in_specs=[
            pl.BlockSpec((tm, d), lambda i, j: (i, 0)),
            pl.BlockSpec((1, d), lambda i, j: (0, 0)),
            pl.BlockSpec((1, 1, 6, d), lambda i, j: (layer, mrow(i), 0, 0)),
            pl.BlockSpec((1, d, tf), lambda i, j: (layer, 0, j)),
            pl.BlockSpec((1, d, tf), lambda i, j: (layer, 0, j)),
            pl.BlockSpec((1, tf, d), lambda i, j: (layer, j, 0)),
            pl.BlockSpec((1, d), lambda i, j: (0, 0)),
        ],
        out_specs=pl.BlockSpec((tm, d), lambda i, j: (i, 0)),
        out_shape=jax.ShapeDtypeStruct((n_rows, d), F32),
        scratch_shapes=[pltpu.VMEM((tm, d), BF16)],
        name="ffn_final" if final else "ffn",
        compiler_params=_cparams(("arbitrary", "arbitrary")),
    )(xt, norm_g, mod_all, wg, wu, wd, final_g)


def _ret_in_kernel(x_ref, g_ref, mod_ref, w_ref, cos_ref, sin_ref, o_ref, h_ref, *, n_qk, k_scale, tn):
    j = pl.program_id(1)

    pl.when(j == 0)(lambda: _norm_mod_into(h_ref, x_ref, g_ref, mod_ref, 0))

    def block(scale):
        h = h_ref[...]
        for sub in range(tn // SUB_N):
            o = sub * SUB_N
            z = _dot(h, w_ref[:, o:o + SUB_N])
            if scale is None:
                o_ref[:, o:o + SUB_N] = z.astype(BF16)
            else:
                c = cos_ref[...]
                s = sin_ref[...]
                x1 = z[:, :LANES] * scale if scale != 1.0 else z[:, :LANES]
                x2 = z[:, LANES:] * scale if scale != 1.0 else z[:, LANES:]
                o_ref[:, o:o + LANES] = (x1 * c - x2 * s).astype(BF16)
                o_ref[:, o + LANES:o + SUB_N] = (x1 * s + x2 * c).astype(BF16)

    pl.when(j < n_qk)(lambda: block(1.0))
    pl.when((j >= n_qk) & (j < 2 * n_qk))(lambda: block(k_scale))
    pl.when(j >= 2 * n_qk)(lambda: block(None))


def _ret_in_proj(xt, norm_g, mod_all, layer, w_in, cos_r, sin_r, mrow, tm, tn):
    nt, d = xt.shape
    n = w_in.shape[1]
    dk = d // RET_HEADS
    return pl.pallas_call(
        functools.partial(_ret_in_kernel, n_qk=d // tn, k_scale=dk ** -0.5, tn=tn),
        grid=(nt // tm, n // tn),
        in_specs=[
            pl.BlockSpec((tm, d), lambda i, j: (i, 0)),
            pl.BlockSpec((1, d), lambda i, j: (0, 0)),
            pl.BlockSpec((1, 1, 6, d), lambda i, j: (layer, mrow(i), 0, 0)),
            pl.BlockSpec((d, tn), lambda i, j: (0, j)),
            pl.BlockSpec((tm, LANES), lambda i, j: (i, 0)),
            pl.BlockSpec((tm, LANES), lambda i, j: (i, 0)),
        ],
        out_specs=pl.BlockSpec((tm, tn), lambda i, j: (i, j)),
        out_shape=jax.ShapeDtypeStruct((nt, n), BF16),
        scratch_shapes=[pltpu.VMEM((tm, d), BF16)],
        name="ret_in_proj",
        compiler_params=_cparams(("arbitrary", "arbitrary")),
    )(xt, norm_g, mod_all, w_in, cos_r, sin_r)


def _log_sigmoid(x):
    return jnp.minimum(x, 0.0) - jnp.log1p(jnp.exp(-jnp.abs(x)))


def _ret_kernel(lf_ref, lb_ref, gn_ref, qc_ref, kc_ref, vc_ref, q_ref, k_ref, v_ref, g_ref, y_ref,
                o_ref, s_ref, dint_ref, qdec_ref, kdec_ref, *, n_ctx, n_lat):
    c = RET_CHUNK
    dk = q_ref.shape[1]
    dv = v_ref.shape[1]
    row = lax.broadcasted_iota(jnp.int32, (c, c), 0)
    col = lax.broadcasted_iota(jnp.int32, (c, c), 1)
    pos = lax.broadcasted_iota(jnp.int32, (c, LANES), 0).astype(F32)
    c_dec = []
    for drn, l_ref in enumerate((lf_ref, lb_ref)):
        lg = _log_sigmoid(l_ref[0])
        lg_c = jnp.broadcast_to(lg[:, 0:1], (1, c))
        diff = (col - row) if drn else (row - col)
        dint_ref[drn] = jnp.where(diff >= 0, jnp.exp(lg_c * jnp.maximum(diff, 0).astype(F32)), 0.0)
        if drn:
            qdec_ref[drn] = jnp.exp(lg * (c - pos))
            kdec_ref[drn] = jnp.exp(lg * pos)
        else:
            qdec_ref[drn] = jnp.exp(lg * (pos + 1.0))
            kdec_ref[drn] = jnp.exp(lg * (c - 1.0 - pos))
        c_dec.append(jnp.exp(jnp.broadcast_to(lg[:, 0:1], (1, dv)) * c))

    def scale_rows(x, dec):
        return jnp.concatenate([x[:, k * LANES:(k + 1) * LANES].astype(F32) * dec for k in range(dk // LANES)],
                               axis=1).astype(BF16)

    def step(drn, q, k, v, want_out):
        s = s_ref[drn]
        o = None
        if want_out:
            a = (_dot_nt(q, k) * dint_ref[drn]).astype(BF16)
            o = _dot(a, v) + _dot(scale_rows(q, qdec_ref[drn]), s.astype(BF16))
        s_ref[drn] = s * c_dec[drn] + _dot_tn(scale_rows(k, kdec_ref[drn]), v)
        return o

    def rows(t):
        return pl.ds(pl.multiple_of(t * c, c), c)

    def lat_step(drn, t):
        sl = rows(t)
        return step(drn, q_ref[sl, :], k_ref[sl, :], v_ref[sl, :], True)

    def finish(t, o):
        sl = rows(t)
        mu = jnp.mean(o, axis=-1, keepdims=True)
        oc = o - mu
        var = jnp.mean(oc * oc, axis=-1, keepdims=True)
        yn = oc * lax.rsqrt(var + NORM_EPS) * gn_ref[...]
        y_ref[sl, :] = (_silu(g_ref[sl, :].astype(F32)) * yn).astype(BF16)

    s_ref[...] = jnp.zeros_like(s_ref)
    for t in range(n_ctx):
        for drn, tt in ((0, t), (1, n_ctx - 1 - t)):
            sl = slice(tt * c, (tt + 1) * c)
            step(drn, qc_ref[sl, :], kc_ref[sl, :], vc_ref[sl, :], False)

    half = n_lat // 2

    def first_half(t, carry):
        tb = n_lat - 1 - t
        o_ref[rows(t), :] = lat_step(0, t)
        o_ref[rows(tb), :] = lat_step(1, tb)
        return carry

    def second_half(t, carry):
        tb = n_lat - 1 - t
        finish(t, o_ref[rows(t), :] + lat_step(0, t))
        finish(tb, o_ref[rows(tb), :] + lat_step(1, tb))
        return carry

    lax.fori_loop(0, half, first_half, 0)
    lax.fori_loop(half, n_lat, second_half, 0)


def _retention(r, lg_f, lg_b, gn_g, b_sz, cn, s_len, d):
    h_n = RET_HEADS
    dk = d // h_n
    dv = 2 * dk
    c = RET_CHUNK
    assert cn % c == 0 and s_len % (2 * c) == 0
    c0 = (b_sz * s_len) // cn
    kq, kk, kv, kg = 0, d // dk, (2 * d) // dv, (4 * d) // dv

    def lg_spec():
        return pl.BlockSpec((1, 1, LANES), lambda b, h: (h, 0, 0))

    return pl.pallas_call(
        functools.partial(_ret_kernel, n_ctx=cn // c, n_lat=s_len // c),
        grid=(b_sz, h_n),
        in_specs=[
            lg_spec(), lg_spec(),
            pl.BlockSpec((1, dv), lambda b, h: (0, h)),
            pl.BlockSpec((cn, dk), lambda b, h: (c0 + b, kq + h)),
            pl.BlockSpec((cn, dk), lambda b, h: (c0 + b, kk + h)),
            pl.BlockSpec((cn, dv), lambda b, h: (c0 + b, kv + h)),
            pl.BlockSpec((s_len, dk), lambda b, h: (b, kq + h)),
            pl.BlockSpec((s_len, dk), lambda b, h: (b, kk + h)),
            pl.BlockSpec((s_len, dv), lambda b, h: (b, kv + h)),
            pl.BlockSpec((s_len, dv), lambda b, h: (b, kg + h)),
        ],
        out_specs=pl.BlockSpec((s_len, dv), lambda b, h: (b, h)),
        out_shape=jax.ShapeDtypeStruct((b_sz * s_len, h_n * dv), BF16),
        scratch_shapes=[
            pltpu.VMEM((s_len, dv), F32),
            pltpu.VMEM((2, dk, dv), F32),
            pltpu.VMEM((2, c, c), F32),
            pltpu.VMEM((2, c, LANES), F32),
            pltpu.VMEM((2, c, LANES), F32),
        ],
        name="retention",
        compiler_params=_cparams(("arbitrary", "arbitrary")),
    )(lg_f, lg_b, gn_g, r, r, r, r, r, r, r)


def _rope_angles(s_len, rot_dim):
    rows = s_len // GRID_W
    row = jnp.repeat(jnp.arange(rows, dtype=F32), GRID_W)
    col = (jnp.arange(s_len) % GRID_W).astype(F32)
    n_freq = rot_dim // 4
    inv = ROPE_BASE ** (-jnp.arange(n_freq, dtype=F32) / n_freq)
    ang = jnp.concatenate([row[:, None] * inv, col[:, None] * inv], axis=-1)
    return jnp.cos(ang), jnp.sin(ang)


def _token_table(lat_table, ctx_row, b_sz, n_ctx_rows):
    width = lat_table.shape[1]
    return jnp.concatenate([jnp.tile(lat_table, (b_sz, 1)),
                            jnp.broadcast_to(ctx_row[None, :], (n_ctx_rows, width))], axis=0)


def kernel(x, c, ctx, c_ctx, mod_w, mod_b, norm_mix_g, norm_ffn_g, ffn_w_gate, ffn_w_up, ffn_w_down, ab_w_in, ab_w_out, swa_sink, mla_q_norm_g, mla_w_q_b, mla_kv_norm_g, mla_w_kv_b, ret_w_in, ret_decay_logit_fwd, ret_decay_logit_bwd, ret_gn_g, ret_w_out, final_norm_g):
    b_sz, s_len, d = x.shape
    cn = ctx.shape[1]
    depth = mod_w.shape[0]
    assert depth == 2 and ab_w_in.shape[0] == 1 and ret_w_in.shape[0] == 1
    assert b_sz + 1 <= 8
    n_lat_rows = b_sz * s_len
    n_ctx_rows = b_sz * cn
    nt = n_lat_rows + n_ctx_rows

    tm = min(1024, n_ctx_rows, s_len)
    tq = min(512, s_len)
    tk = min(1024, s_len)
    assert n_ctx_rows % tm == 0 and s_len % tm == 0 and s_len % cn == 0 and s_len % tq == 0 and s_len % tk == 0

    def make_mrow(t):
        n_lat_tiles = n_lat_rows // t
        per_b = s_len // t
        return lambda i: jnp.where(i < n_lat_tiles, 1 + i // per_b, 0)

    mrow = make_mrow(tm)
    tm_o = min(512, tm)
    mrow_o = make_mrow(tm_o)

    x2d = x.reshape(n_lat_rows, d)
    ctx2d = ctx.reshape(n_ctx_rows, d)

    cond8 = jnp.zeros((8, d), F32).at[0].set(c_ctx).at[1:1 + b_sz].set(c)
    mod_all = _modulation(cond8, mod_w, mod_b).reshape(depth, 8, 6, d)

    cos_a, sin_a = _rope_angles(s_len, SWA_HEAD_DIM)
    cos_b, sin_b = _rope_angles(s_len, MLA_ROPE_DIM)
    cos_r, sin_r = _rope_angles(s_len, d // RET_HEADS)
    ones = jnp.ones((LANES,), F32)
    zeros = jnp.zeros((LANES,), F32)
    half = jnp.concatenate([jnp.ones((64,), F32), jnp.zeros((64,), F32)])
    z64 = jnp.zeros((s_len, 64), F32)
    table = functools.partial(_token_table, b_sz=b_sz, n_ctx_rows=n_ctx_rows)
    t_cos_a = table(jnp.concatenate([cos_a, cos_a], axis=1), ones)
    t_sin_a = table(jnp.concatenate([-sin_a, sin_a], axis=1), zeros)
    t_cos_b = table(jnp.concatenate([cos_b, cos_b, z64], axis=1), half)
    t_sin_b = table(jnp.concatenate([-sin_b, sin_b, z64], axis=1), zeros)
    t_cos_r = table(cos_r, ones)
    t_sin_r = table(sin_r, zeros)

    bf = lambda w: w.astype(BF16)
    w_ab_in = jnp.pad(ab_w_in[0], ((0, 0), (0, 2560 - ab_w_in.shape[2]))).astype(BF16)
    wq_b = jnp.pad(mla_w_q_b[0].reshape(MLA_Q_RANK, MLA_HEADS, MLA_NOPE_DIM + MLA_ROPE_DIM),
                   ((0, 0), (0, 0), (0, MLA_HEAD_PAD - MLA_NOPE_DIM - MLA_ROPE_DIM))
                   ).reshape(MLA_Q_RANK, MLA_HEADS * MLA_HEAD_PAD).astype(BF16)
    wkv_b = bf(mla_w_kv_b[0])
    wg, wu, wd = bf(ffn_w_gate), bf(ffn_w_up), bf(ffn_w_down)
    tf = 512
    final_g = final_norm_g[None, :]

    a_qkv, lat = _ab_in_proj(x2d, ctx2d, norm_mix_g[0:1], mod_all, 0, w_ab_in, t_cos_a, t_sin_a, mrow, tm)
    qm, kcat, vext = _mla_proj(lat, mla_q_norm_g[0:1], mla_kv_norm_g[0:1], wq_b, wkv_b, t_cos_b, t_sin_b, tm)
    oa, oa_c = _swa_attention(a_qkv, swa_sink[0], b_sz, cn, s_len, tq)
    ob, ob_c = _mla_attention(qm, kcat, vext, b_sz, cn, s_len, min(1024, s_len), tk)
    xt = _out_proj([oa, ob], x2d, [oa_c, ob_c], ctx2d, bf(ab_w_out[0]), mod_all, 0, mrow_o, tm_o)
    xt = _ffn(xt, norm_ffn_g[0:1], mod_all, 0, wg, wu, wd, final_g, mrow, nt, tm, tf, final=False)

    r = _ret_in_proj(xt, norm_mix_g[1:2], mod_all, 1, bf(ret_w_in[0]), t_cos_r, t_sin_r, mrow, tm, 1024)
    lg_shape = (RET_HEADS, 1, LANES)
    lg_f = jnp.broadcast_to(ret_decay_logit_fwd[0].astype(F32)[:, None, None], lg_shape)
    lg_b = jnp.broadcast_to(ret_decay_logit_bwd[0].astype(F32)[:, None, None], lg_shape)
    y = _retention(r, lg_f, lg_b, ret_gn_g[0:1], b_sz, cn, s_len, d)
    xl = _out_proj([y], xt, None, None, bf(ret_w_out[0]), mod_all, 1, mrow_o, tm_o)
    out = _ffn(xl, norm_ffn_g[1:2], mod_all, 1, wg, wu, wd, final_g, mrow, n_lat_rows, tm, tf, final=True)
    return out.reshape(b_sz, s_len, d)
```

```python
import functools

import jax
import jax.numpy as jnp
import numpy as np
from jax import lax
from jax.experimental import pallas as pl
from jax.experimental.pallas import tpu as pltpu

GRID_W = 64
ROPE_BASE = 10000.0
NORM_EPS = 1e-6
NEG_INF = -1e30
LOG2_E = 1.4426950408889634

SWA_HEADS = 8
SWA_KV_HEADS = 2
SWA_HEAD_DIM = 128
SWA_WINDOW = 128
SWA_QSCALE = SWA_HEAD_DIM ** -0.5 * LOG2_E

MLA_HEADS = 8
MLA_Q_RANK = 512
MLA_KV_RANK = 256
MLA_NOPE_DIM = 128
MLA_ROPE_DIM = 64
MLA_V_DIM = 128
MLA_HEAD_PAD = 256
MLA_QSCALE = (MLA_NOPE_DIM + MLA_ROPE_DIM) ** -0.5 * LOG2_E

RET_HEADS = 8
RET_CHUNK = 256

LANES = 128
V7X_VMEM_BYTES = 64 * 1024 * 1024
VMEM_LIMIT = V7X_VMEM_BYTES - 8 * 1024 * 1024

F32 = jnp.float32
BF16 = jnp.bfloat16


def _cparams(sem):
    return pltpu.CompilerParams(dimension_semantics=sem, vmem_limit_bytes=VMEM_LIMIT)


def _dot(a, b):
    return jnp.dot(a, b, preferred_element_type=F32)


def _dot_nt(a, b):
    return lax.dot_general(a, b, (((1,), (1,)), ((), ())), preferred_element_type=F32)


def _dot_tn(a, b):
    return lax.dot_general(a, b, (((0,), (0,)), ((), ())), preferred_element_type=F32)


def _silu(x):
    h = 0.5 * x
    return h + h * jnp.tanh(h)


NORM_ROWS = 32
SUB_N = 256


def _norm_mod_into(h_ref, x_ref, g_ref, mod_ref, shift_row):
    shift = mod_ref[0, 0, shift_row:shift_row + 1, :]
    gain = g_ref[...] * (1.0 + mod_ref[0, 0, shift_row + 1:shift_row + 2, :])

    def body(t, carry):
        sl = pl.ds(pl.multiple_of(t * NORM_ROWS, NORM_ROWS), NORM_ROWS)
        x = x_ref[sl, :]
        inv = lax.rsqrt(jnp.mean(x * x, axis=-1, keepdims=True) + NORM_EPS)
        h_ref[sl, :] = ((x * inv) * gain + shift).astype(BF16)
        return carry

    lax.fori_loop(0, x_ref.shape[0] // NORM_ROWS, body, 0, unroll=4)


def _split_rows(n_lat_tiles):
    lat = lambda i, j: (jnp.minimum(i, n_lat_tiles - 1), 0)
    ctx = lambda i, j: (jnp.maximum(i - n_lat_tiles, 0), 0)
    return lat, ctx


def _mod_kernel(c_ref, w_ref, b_ref, o_ref):
    a = _silu(c_ref[...]).astype(BF16)
    o_ref[0] = _dot(a, w_ref[0].astype(BF16)) + b_ref[0]


def _modulation(cond8, mod_w, mod_b):
    depth, d, n = mod_w.shape
    tn = 1024
    return pl.pallas_call(
        _mod_kernel,
        grid=(depth, n // tn),
        in_specs=[
            pl.BlockSpec((8, d), lambda l, j: (0, 0)),
            pl.BlockSpec((1, d, tn), lambda l, j: (l, 0, j)),
            pl.BlockSpec((1, 1, tn), lambda l, j: (l, 0, j)),
        ],
        out_specs=pl.BlockSpec((1, 8, tn), lambda l, j: (l, 0, j)),
        out_shape=jax.ShapeDtypeStruct((depth, 8, n), F32),
        name="modulation",
        compiler_params=_cparams(("arbitrary", "arbitrary")),
    )(cond8, mod_w, mod_b.reshape(depth, 1, n))


def _rope128(z, cosf, sins):
    return z * cosf + pltpu.roll(z, 64, axis=1) * sins


def _ab_in_kernel(x_ref, c_ref, g_ref, mod_ref, w_ref, cos_ref, sin_ref, a_ref, l_ref, h_ref, *, n_lat_tiles):
    i = pl.program_id(0)
    j = pl.program_id(1)

    pl.when((j == 0) & (i < n_lat_tiles))(lambda: _norm_mod_into(h_ref, x_ref, g_ref, mod_ref, 0))
    pl.when((j == 0) & (i >= n_lat_tiles))(lambda: _norm_mod_into(h_ref, c_ref, g_ref, mod_ref, 0))

    def block(kinds):
        h = h_ref[...]
        for half, kind in enumerate(kinds):
            z = _dot(h, w_ref[:, half * SUB_N:(half + 1) * SUB_N])
            for k in range(SUB_N // LANES):
                sl = slice(half * SUB_N + k * LANES, half * SUB_N + (k + 1) * LANES)
                zk = z[:, k * LANES:(k + 1) * LANES]
                if kind == "q":
                    a_ref[:, sl] = (_rope128(zk, cos_ref[...], sin_ref[...]) * SWA_QSCALE).astype(BF16)
                elif kind == "k":
                    a_ref[:, sl] = _rope128(zk, cos_ref[...], sin_ref[...]).astype(BF16)
                elif kind == "v":
                    a_ref[:, sl] = zk.astype(BF16)
                else:
                    l_ref[:, sl] = zk

    pl.when(j <= 1)(lambda: block(("q", "q")))
    pl.when(j == 2)(lambda: block(("k", "v")))
    pl.when(j >= 3)(lambda: block(("f32", "f32")))


def _ab_in_proj(x2d, ctx2d, norm_g, mod_all, layer, w_in, cos_a, sin_a, mrow, trow, tm):
    d = x2d.shape[1]
    n_lat_tiles = x2d.shape[0] // tm
    nt = x2d.shape[0] + ctx2d.shape[0]
    tn = 512
    lat_map, ctx_map = _split_rows(n_lat_tiles)
    return pl.pallas_call(
        functools.partial(_ab_in_kernel, n_lat_tiles=n_lat_tiles),
        grid=(nt // tm, 5),
        in_specs=[
            pl.BlockSpec((tm, d), lat_map),
            pl.BlockSpec((tm, d), ctx_map, pipeline_mode=pl.Buffered(1)),
            pl.BlockSpec((1, d), lambda i, j: (0, 0)),
            pl.BlockSpec((1, 1, 6, d), lambda i, j: (layer, mrow(i), 0, 0)),
            pl.BlockSpec((d, tn), lambda i, j: (0, j)),
            pl.BlockSpec((tm, LANES), lambda i, j: (trow(i), 0)),
            pl.BlockSpec((tm, LANES), lambda i, j: (trow(i), 0)),
        ],
        out_specs=[
            pl.BlockSpec((tm, tn), lambda i, j: (i, jnp.minimum(j, 2))),
            pl.BlockSpec((tm, tn), lambda i, j: (i, jnp.maximum(j - 3, 0))),
        ],
        out_shape=[
            jax.ShapeDtypeStruct((nt, 3 * tn), BF16),
            jax.ShapeDtypeStruct((nt, 2 * tn), F32),
        ],
        scratch_shapes=[pltpu.VMEM((tm, d), BF16)],
        name="ab_in_proj",
        compiler_params=_cparams(("arbitrary", "arbitrary")),
    )(x2d, ctx2d, norm_g, mod_all, w_in, cos_a, sin_a)


def _rope64(r, cosp, sinp):
    lane = lax.broadcasted_iota(jnp.int32, r.shape, 1)
    partner = jnp.where(lane < 32, pltpu.roll(r, 96, axis=1), pltpu.roll(r, 32, axis=1))
    return r * cosp + partner * sinp


def _mla_proj_kernel(l_ref, qg_ref, kvg_ref, wq_ref, wkv_ref, cos_ref, sin_ref, qm_ref, kc_ref, vm_ref):
    cosp = cos_ref[...]
    sinp = sin_ref[...]

    def rms(x, g):
        return x * lax.rsqrt(jnp.mean(x * x, axis=-1, keepdims=True) + NORM_EPS) * g

    qn = rms(l_ref[:, :MLA_Q_RANK], qg_ref[...]).astype(BF16)
    qm = _dot(qn, wq_ref[...])
    kvn = rms(l_ref[:, MLA_Q_RANK:MLA_Q_RANK + MLA_KV_RANK], kvg_ref[...]).astype(BF16)
    kv = _dot(kvn, wkv_ref[...])
    kr = l_ref[:, MLA_Q_RANK + MLA_KV_RANK:MLA_Q_RANK + MLA_KV_RANK + LANES]
    krr = _rope64(kr, cosp, sinp).astype(BF16)
    ones = jnp.ones((l_ref.shape[0], LANES), BF16)
    for h in range(MLA_HEADS):
        o = h * MLA_HEAD_PAD
        qm_ref[:, o:o + LANES] = (qm[:, o:o + LANES] * MLA_QSCALE).astype(BF16)
        qm_ref[:, o + LANES:o + 2 * LANES] = (
            _rope64(qm[:, o + LANES:o + 2 * LANES], cosp, sinp) * MLA_QSCALE).astype(BF16)
        kc_ref[:, o:o + LANES] = kv[:, o:o + LANES].astype(BF16)
        kc_ref[:, o + LANES:o + 2 * LANES] = krr
        vm_ref[:, o:o + LANES] = kv[:, o + LANES:o + 2 * LANES].astype(BF16)
        vm_ref[:, o + LANES:o + 2 * LANES] = ones


def _mla_proj(lat, q_norm_g, kv_norm_g, wq, wkv, cos_b, sin_b, trow, tm):
    nt = lat.shape[0]
    hp = MLA_HEADS * MLA_HEAD_PAD
    return pl.pallas_call(
        _mla_proj_kernel,
        grid=(nt // tm,),
        in_specs=[
            pl.BlockSpec((tm, lat.shape[1]), lambda i: (i, 0)),
            pl.BlockSpec((1, MLA_Q_RANK), lambda i: (0, 0)),
            pl.BlockSpec((1, MLA_KV_RANK), lambda i: (0, 0)),
            pl.BlockSpec(wq.shape, lambda i: (0, 0)),
            pl.BlockSpec(wkv.shape, lambda i: (0, 0)),
            pl.BlockSpec((tm, LANES), lambda i: (trow(i), 0)),
            pl.BlockSpec((tm, LANES), lambda i: (trow(i), 0)),
        ],
        out_specs=[
            pl.BlockSpec((tm, hp), lambda i: (i, 0)),
            pl.BlockSpec((tm, hp), lambda i: (i, 0)),
            pl.BlockSpec((tm, hp), lambda i: (i, 0)),
        ],
        out_shape=[
            jax.ShapeDtypeStruct((nt, hp), BF16),
            jax.ShapeDtypeStruct((nt, hp), BF16),
            jax.ShapeDtypeStruct((nt, hp), BF16),
        ],
        name="mla_proj",
        compiler_params=_cparams(("arbitrary",)),
    )(lat, q_norm_g, kv_norm_g, wq, wkv, cos_b, sin_b)


def _swa_softmax_pv(s, sk, v_ext):
    d = SWA_HEAD_DIM
    m = jnp.maximum(jnp.max(s, axis=-1, keepdims=True), sk)
    o_ext = _dot(jnp.exp2(s - m).astype(BF16), v_ext)
    return o_ext[:, :d] / (o_ext[:, d:] + jnp.exp2(sk - m))


def _swa_ctx_kernel(sink_ref, q_ref, kc, vc, o_ref):
    kvh = pl.program_id(1)
    d = SWA_HEAD_DIM
    g_heads = SWA_HEADS // SWA_KV_HEADS
    k_all = kc[...]
    v_ext = jnp.concatenate([vc[...], jnp.ones_like(vc)], axis=1)
    for g in range(g_heads):
        sk = sink_ref[kvh * g_heads + g] * LOG2_E
        s = _dot_nt(q_ref[:, g * d:(g + 1) * d], k_all)
        o_ref[:, g * d:(g + 1) * d] = _swa_softmax_pv(s, sk, v_ext).astype(BF16)


def _swa_kernel(sink_ref, q_ref, kp, km, kn, kc, vp, vm, vn, vc, o_ref, *, tq, s_len):
    kvh = pl.program_id(1)
    i = pl.program_id(2)
    d = SWA_HEAD_DIM
    g_heads = SWA_HEADS // SWA_KV_HEADS
    k_all = jnp.concatenate([kp[...], km[...], kn[...], kc[...]], axis=0)
    v_all = jnp.concatenate([vp[...], vm[...], vn[...], vc[...]], axis=0)
    v_ext = jnp.concatenate([v_all, jnp.ones_like(v_all)], axis=1)
    nb = tq + 2 * SWA_WINDOW
    shape = (tq, k_all.shape[0])
    row = lax.broadcasted_iota(jnp.int32, shape, 0)
    col = lax.broadcasted_iota(jnp.int32, shape, 1)
    kpos = i * tq - SWA_WINDOW + col
    valid = (col >= nb) | ((jnp.abs(col - SWA_WINDOW - row) <= SWA_WINDOW) & (kpos >= 0) & (kpos < s_len))
    for g in range(g_heads):
        sk = sink_ref[kvh * g_heads + g] * LOG2_E
        s = jnp.where(valid, _dot_nt(q_ref[:, g * d:(g + 1) * d], k_all), NEG_INF)
        o_ref[:, g * d:(g + 1) * d] = _swa_softmax_pv(s, sk, v_ext).astype(BF16)


def _swa_attention(a, sink, b_sz, cn, s_len, tq):
    d = SWA_HEAD_DIM
    w = SWA_WINDOW
    gw = (SWA_HEADS // SWA_KV_HEADS) * d
    kcol = SWA_HEADS
    vcol = SWA_HEADS + SWA_KV_HEADS
    nq = s_len // tq
    r = tq // w
    c0 = (b_sz * s_len) // cn
    smem = pl.BlockSpec(memory_space=pltpu.SMEM)

    def main(b, h, i):
        return b * nq + i

    def prev(b, h, i):
        return b * (s_len // w) + jnp.maximum(i * r - 1, 0)

    def nxt(b, h, i):
        return b * (s_len // w) + jnp.minimum((i + 1) * r, s_len // w - 1)

    o_lat = pl.pallas_call(
        functools.partial(_swa_kernel, tq=tq, s_len=s_len),
        grid=(b_sz, SWA_KV_HEADS, nq),
        in_specs=[
            smem,
            pl.BlockSpec((tq, gw), lambda b, h, i: (main(b, h, i), h)),
            pl.BlockSpec((w, d), lambda b, h, i: (prev(b, h, i), kcol + h)),
            pl.BlockSpec((tq, d), lambda b, h, i: (main(b, h, i), kcol + h)),
            pl.BlockSpec((w, d), lambda b, h, i: (nxt(b, h, i), kcol + h)),
            pl.BlockSpec((cn, d), lambda b, h, i: (c0 + b, kcol + h)),
            pl.BlockSpec((w, d), lambda b, h, i: (prev(b, h, i), vcol + h)),
            pl.BlockSpec((tq, d), lambda b, h, i: (main(b, h, i), vcol + h)),
            pl.BlockSpec((w, d), lambda b, h, i: (nxt(b, h, i), vcol + h)),
            pl.BlockSpec((cn, d), lambda b, h, i: (c0 + b, vcol + h)),
        ],
        out_specs=pl.BlockSpec((tq, gw), lambda b, h, i: (main(b, h, i), h)),
        out_shape=jax.ShapeDtypeStruct((b_sz * s_len, SWA_HEADS * d), BF16),
        name="swa_latent",
        compiler_params=_cparams(("arbitrary", "arbitrary", "arbitrary")),
    )(sink, a, a, a, a, a, a, a, a, a)

    o_ctx = pl.pallas_call(
        _swa_ctx_kernel,
        grid=(b_sz, SWA_KV_HEADS),
        in_specs=[
            smem,
            pl.BlockSpec((cn, gw), lambda b, h: (c0 + b, h)),
            pl.BlockSpec((cn, d), lambda b, h: (c0 + b, kcol + h)),
            pl.BlockSpec((cn, d), lambda b, h: (c0 + b, vcol + h)),
        ],
        out_specs=pl.BlockSpec((cn, gw), lambda b, h: (b, h)),
        out_shape=jax.ShapeDtypeStruct((b_sz * cn, SWA_HEADS * d), BF16),
        name="swa_context",
        compiler_params=_cparams(("arbitrary", "arbitrary")),
    )(sink, a, a, a)
    return o_lat, o_ctx


def _mla_kernel(q_ref, kc_ref, vc_ref, *refs, tk, latent, n_cast=0):
    if latent:
        kl_ref, vl_ref = refs[:2]
        cast_in = refs[2:2 + n_cast]
        o_ref = refs[2 + n_cast]
        cast_out = refs[3 + n_cast:]
        for src, dst in zip(cast_in, cast_out):
            dst[...] = src[...].astype(BF16)
    else:
        (o_ref,) = refs
    dv = MLA_V_DIM
    q = q_ref[...]
    chunks = []
    if latent:
        chunks += [(kl_ref, vl_ref, slice(j * tk, (j + 1) * tk)) for j in range(kl_ref.shape[0] // tk)]
    chunks.append((kc_ref, vc_ref, slice(None)))
    m = acc = None
    for k_ref, v_ref, sl in chunks:
        s = _dot_nt(q, k_ref[sl, :])
        if m is None:
            m = jnp.max(s, axis=-1, keepdims=True)
            acc = _dot(jnp.exp2(s - m).astype(BF16), v_ref[sl, :])
        else:
            m_new = jnp.maximum(m, jnp.max(s, axis=-1, keepdims=True))
            acc = jnp.exp2(m - m_new) * acc + _dot(jnp.exp2(s - m_new).astype(BF16), v_ref[sl, :])
            m = m_new
    o_ref[...] = (acc[:, :dv] / acc[:, dv:]).astype(BF16)


BF16_SUBLANES = 16


def _cast_block(rows, n_steps):
    share = 1
    while (rows * share) % n_steps or (rows * share // n_steps) % BF16_SUBLANES:
        share *= 2
        assert share <= n_steps
    return rows * share // n_steps, share


def _mla_attention(qm, kcat, vext, cast_weights, b_sz, cn, s_len, tq, tk):
    hp = MLA_HEAD_PAD
    dv = MLA_V_DIM
    nq = s_len // tq
    c0 = (b_sz * s_len) // cn
    n_steps = b_sz * MLA_HEADS * nq

    cast_specs = []
    for w in cast_weights:
        rows, share = _cast_block(w.shape[0], n_steps)
        cast_specs.append(pl.BlockSpec(
            (rows, w.shape[1]), lambda b, h, i, share=share: (((b * MLA_HEADS + h) * nq + i) // share, 0)))

    o_lat, *w_bf16 = pl.pallas_call(
        functools.partial(_mla_kernel, tk=tk, latent=True, n_cast=len(cast_weights)),
        grid=(b_sz, MLA_HEADS, nq),
        in_specs=[
            pl.BlockSpec((tq, hp), lambda b, h, i: (b * nq + i, h)),
            pl.BlockSpec((cn, hp), lambda b, h, i: (c0 + b, h)),
            pl.BlockSpec((cn, hp), lambda b, h, i: (c0 + b, h)),
            pl.BlockSpec((s_len, hp), lambda b, h, i: (b, h)),
            pl.BlockSpec((s_len, hp), lambda b, h, i: (b, h)),
        ] + cast_specs,
        out_specs=[pl.BlockSpec((tq, dv), lambda b, h, i: (b * nq + i, h))] + cast_specs,
        out_shape=[jax.ShapeDtypeStruct((b_sz * s_len, MLA_HEADS * dv), BF16)]
        + [jax.ShapeDtypeStruct(w.shape, BF16) for w in cast_weights],
        name="mla_latent",
        compiler_params=_cparams(("arbitrary",) * 3),
    )(qm, kcat, vext, kcat, vext, *cast_weights)

    o_ctx = pl.pallas_call(
        functools.partial(_mla_kernel, tk=0, latent=False),
        grid=(b_sz, MLA_HEADS),
        in_specs=[
            pl.BlockSpec((cn, hp), lambda b, h: (c0 + b, h)),
            pl.BlockSpec((cn, hp), lambda b, h: (c0 + b, h)),
            pl.BlockSpec((cn, hp), lambda b, h: (c0 + b, h)),
        ],
        out_specs=pl.BlockSpec((cn, dv), lambda b, h: (b, h)),
        out_shape=jax.ShapeDtypeStruct((b_sz * cn, MLA_HEADS * dv), BF16),
        name="mla_context",
        compiler_params=_cparams(("arbitrary", "arbitrary")),
    )(qm, kcat, vext)
    return o_lat, o_ctx, w_bf16


def _out_proj_kernel(w_ref, mod_ref, *refs, n_parts, n_lat_tiles):
    o_ref = refs[-1]
    lat = refs[:n_parts + 1]
    ctx = refs[n_parts + 1:-1]

    def compute(side):
        parts = [p[...] for p in side[:-1]]
        res_ref = side[-1]
        for sub in range(o_ref.shape[1] // SUB_N):
            cols = slice(sub * SUB_N, (sub + 1) * SUB_N)
            acc = None
            k0 = 0
            for p in parts:
                kw = p.shape[1]
                t = _dot(p, w_ref[k0:k0 + kw, cols])
                acc = t if acc is None else acc + t
                k0 += kw
            o_ref[:, cols] = res_ref[:, cols] + mod_ref[0, 0, 2:3, cols] * acc

    if n_lat_tiles is None:
        compute(lat)
    else:
        i = pl.program_id(0)
        pl.when(i < n_lat_tiles)(lambda: compute(lat))
        pl.when(i >= n_lat_tiles)(lambda: compute(ctx))


def _out_proj(lat_parts, lat_res, ctx_parts, ctx_res, w, mod_all, layer, mrow, tm):
    lat_rows = lat_parts[0].shape[0]
    d = lat_res.shape[1]
    n_lat = lat_rows // tm
    if ctx_parts is None:
        n_lat_tiles = None
        rows = lat_rows
        specs = [pl.BlockSpec((tm, p.shape[1]), lambda i: (i, 0)) for p in lat_parts]
        specs.append(pl.BlockSpec((tm, d), lambda i: (i, 0)))
        operands = [*lat_parts, lat_res]
    else:
        n_lat_tiles = n_lat
        rows = lat_rows + ctx_parts[0].shape[0]
        lat_map = lambda i: (jnp.minimum(i, n_lat - 1), 0)
        ctx_map = lambda i: (jnp.maximum(i - n_lat, 0), 0)
        specs = [pl.BlockSpec((tm, p.shape[1]), lat_map) for p in lat_parts]
        specs.append(pl.BlockSpec((tm, d), lat_map))
        specs += [pl.BlockSpec((tm, p.shape[1]), ctx_map) for p in ctx_parts]
        specs.append(pl.BlockSpec((tm, d), ctx_map))
        operands = [*lat_parts, lat_res, *ctx_parts, ctx_res]
    return pl.pallas_call(
        functools.partial(_out_proj_kernel, n_parts=len(lat_parts), n_lat_tiles=n_lat_tiles),
        grid=(rows // tm,),
        in_specs=[
            pl.BlockSpec(w.shape, lambda i: (0, 0), pipeline_mode=pl.Buffered(1)),
            pl.BlockSpec((1, 1, 6, d), lambda i: (layer, mrow(i), 0, 0)),
        ] + specs,
        out_specs=pl.BlockSpec((tm, d), lambda i: (i, 0)),
        out_shape=jax.ShapeDtypeStruct((rows, d), F32),
        name="out_proj",
        compiler_params=_cparams(("arbitrary",)),
    )(w, mod_all, *operands)


def _ffn_kernel(x_ref, g_ref, mod_ref, wg_ref, wu_ref, wd_ref, fg_ref, o_ref, h_ref, *, nf, final):
    j = pl.program_id(1)

    @pl.when(j == 0)
    def _():
        _norm_mod_into(h_ref, x_ref, g_ref, mod_ref, 3)
        o_ref[...] = jnp.zeros_like(o_ref)

    h = h_ref[...]
    a = (_silu(_dot(h, wg_ref[0].astype(BF16))) * _dot(h, wu_ref[0].astype(BF16))).astype(BF16)
    o_ref[...] += _dot(a, wd_ref[0].astype(BF16))

    @pl.when(j == nf - 1)
    def _():
        gate = mod_ref[0, 0, 5:6, :]

        def body(t, carry):
            sl = pl.ds(pl.multiple_of(t * NORM_ROWS, NORM_ROWS), NORM_ROWS)
            y = x_ref[sl, :] + gate * o_ref[sl, :]
            if final:
                y = y * lax.rsqrt(jnp.mean(y * y, axis=-1, keepdims=True) + NORM_EPS) * fg_ref[...]
            o_ref[sl, :] = y
            return carry

        lax.fori_loop(0, x_ref.shape[0] // NORM_ROWS, body, 0, unroll=4)


def _ffn(xt, norm_g, mod_all, layer, wg, wu, wd, final_g, mrow, n_rows, tm, tf, final):
    d = xt.shape[1]
    f = wg.shape[2]
    nf = f // tf
    return pl.pallas_call(
        functools.partial(_ffn_kernel, nf=nf, final=final),
        grid=(n_rows // tm, nf),
        in_specs=[
            pl.BlockSpec((tm, d), lambda i, j: (i, 0)),
            pl.BlockSpec((1, d), lambda i, j: (0, 0)),
            pl.BlockSpec((1, 1, 6, d), lambda i, j: (layer, mrow(i), 0, 0)),
            pl.BlockSpec((1, d, tf), lambda i, j: (layer, 0, j)),
            pl.BlockSpec((1, d, tf), lambda i, j: (layer, 0, j)),
            pl.BlockSpec((1, tf, d), lambda i, j: (layer, j, 0)),
            pl.BlockSpec((1, d), lambda i, j: (0, 0)),
        ],
        out_specs=pl.BlockSpec((tm, d), lambda i, j: (i, 0)),
        out_shape=jax.ShapeDtypeStruct((n_rows, d), F32),
        scratch_shapes=[pltpu.VMEM((tm, d), BF16)],
        name="ffn_final" if final else "ffn",
        compiler_params=_cparams(("arbitrary", "arbitrary")),
    )(xt, norm_g, mod_all, wg, wu, wd, final_g)


def _ret_in_kernel(x_ref, g_ref, mod_ref, w_ref, cos_ref, sin_ref, o_ref, h_ref, *, n_qk, k_scale, tn):
    j = pl.program_id(1)

    pl.when(j == 0)(lambda: _norm_mod_into(h_ref, x_ref, g_ref, mod_ref, 0))

    def block(scale):
        h = h_ref[...]
        for sub in range(tn // SUB_N):
            o = sub * SUB_N
            z = _dot(h, w_ref[:, o:o + SUB_N])
            if scale is None:
                o_ref[:, o:o + SUB_N] = z.astype(BF16)
            else:
                c = cos_ref[...]
                s = sin_ref[...]
                x1 = z[:, :LANES] * scale if scale != 1.0 else z[:, :LANES]
                x2 = z[:, LANES:] * scale if scale != 1.0 else z[:, LANES:]
                o_ref[:, o:o + LANES] = (x1 * c - x2 * s).astype(BF16)
                o_ref[:, o + LANES:o + SUB_N] = (x1 * s + x2 * c).astype(BF16)

    pl.when(j < n_qk)(lambda: block(1.0))
    pl.when((j >= n_qk) & (j < 2 * n_qk))(lambda: block(k_scale))
    pl.when(j >= 2 * n_qk)(lambda: block(None))


def _ret_in_proj(xt, norm_g, mod_all, layer, w_in, cos_r, sin_r, mrow, trow, tm, tn):
    nt, d = xt.shape
    n = w_in.shape[1]
    dk = d // RET_HEADS
    return pl.pallas_call(
        functools.partial(_ret_in_kernel, n_qk=d // tn, k_scale=dk ** -0.5, tn=tn),
        grid=(nt // tm, n // tn),
        in_specs=[
            pl.BlockSpec((tm, d), lambda i, j: (i, 0)),
            pl.BlockSpec((1, d), lambda i, j: (0, 0)),
            pl.BlockSpec((1, 1, 6, d), lambda i, j: (layer, mrow(i), 0, 0)),
            pl.BlockSpec((d, tn), lambda i, j: (0, j)),
            pl.BlockSpec((tm, LANES), lambda i, j: (trow(i), 0)),
            pl.BlockSpec((tm, LANES), lambda i, j: (trow(i), 0)),
        ],
        out_specs=pl.BlockSpec((tm, tn), lambda i, j: (i, j)),
        out_shape=jax.ShapeDtypeStruct((nt, n), BF16),
        scratch_shapes=[pltpu.VMEM((tm, d), BF16)],
        name="ret_in_proj",
        compiler_params=_cparams(("arbitrary", "arbitrary")),
    )(xt, norm_g, mod_all, w_in, cos_r, sin_r)


def _log_sigmoid(x):
    return jnp.minimum(x, 0.0) - jnp.log1p(jnp.exp(-jnp.abs(x)))


def _ret_kernel(lf_ref, lb_ref, gn_ref, qc_ref, kc_ref, vc_ref, q_ref, k_ref, v_ref, g_ref, y_ref,
                o_ref, s_ref, dint_ref, qdec_ref, kdec_ref, *, n_ctx, n_lat):
    c = RET_CHUNK
    dk = q_ref.shape[1]
    dv = v_ref.shape[1]
    row = lax.broadcasted_iota(jnp.int32, (c, c), 0)
    col = lax.broadcasted_iota(jnp.int32, (c, c), 1)
    pos = lax.broadcasted_iota(jnp.int32, (c, LANES), 0).astype(F32)
    c_dec = []
    for drn, l_ref in enumerate((lf_ref, lb_ref)):
        lg = _log_sigmoid(l_ref[0])
        lg_c = jnp.broadcast_to(lg[:, 0:1], (1, c))
        diff = (col - row) if drn else (row - col)
        dint_ref[drn] = jnp.where(diff >= 0, jnp.exp(lg_c * jnp.maximum(diff, 0).astype(F32)), 0.0)
        if drn:
            qdec_ref[drn] = jnp.exp(lg * (c - pos))
            kdec_ref[drn] = jnp.exp(lg * pos)
        else:
            qdec_ref[drn] = jnp.exp(lg * (pos + 1.0))
            kdec_ref[drn] = jnp.exp(lg * (c - 1.0 - pos))
        c_dec.append(jnp.exp(jnp.broadcast_to(lg[:, 0:1], (1, dv)) * c))

    def scale_rows(x, dec):
        return jnp.concatenate([x[:, k * LANES:(k + 1) * LANES].astype(F32) * dec for k in range(dk // LANES)],
                               axis=1).astype(BF16)

    def step(drn, q, k, v, want_out):
        s = s_ref[drn]
        o = None
        if want_out:
            a = (_dot_nt(q, k) * dint_ref[drn]).astype(BF16)
            o = _dot(a, v) + _dot(scale_rows(q, qdec_ref[drn]), s.astype(BF16))
        s_ref[drn] = s * c_dec[drn] + _dot_tn(scale_rows(k, kdec_ref[drn]), v)
        return o

    def rows(t):
        return pl.ds(pl.multiple_of(t * c, c), c)

    def lat_step(drn, t):
        sl = rows(t)
        return step(drn, q_ref[sl, :], k_ref[sl, :], v_ref[sl, :], True)

    def finish(t, o):
        sl = rows(t)
        mu = jnp.mean(o, axis=-1, keepdims=True)
        oc = o - mu
        var = jnp.mean(oc * oc, axis=-1, keepdims=True)
        yn = oc * lax.rsqrt(var + NORM_EPS) * gn_ref[...]
        y_ref[sl, :] = (_silu(g_ref[sl, :].astype(F32)) * yn).astype(BF16)

    s_ref[...] = jnp.zeros_like(s_ref)
    for t in range(n_ctx):
        for drn, tt in ((0, t), (1, n_ctx - 1 - t)):
            sl = slice(tt * c, (tt + 1) * c)
            step(drn, qc_ref[sl, :], kc_ref[sl, :], vc_ref[sl, :], False)

    half = n_lat // 2

    def first_half(t, carry):
        tb = n_lat - 1 - t
        o_ref[rows(t), :] = lat_step(0, t)
        o_ref[rows(tb), :] = lat_step(1, tb)
        return carry

    def second_half(t, carry):
        tb = n_lat - 1 - t
        finish(t, o_ref[rows(t), :] + lat_step(0, t))
        finish(tb, o_ref[rows(tb), :] + lat_step(1, tb))
        return carry

    lax.fori_loop(0, half, first_half, 0)
    lax.fori_loop(half, n_lat, second_half, 0)


def _retention(r, lg_f, lg_b, gn_g, b_sz, cn, s_len, d):
    h_n = RET_HEADS
    dk = d // h_n
    dv = 2 * dk
    c = RET_CHUNK
    assert cn % c == 0 and s_len % (2 * c) == 0
    c0 = (b_sz * s_len) // cn
    kq, kk, kv, kg = 0, d // dk, (2 * d) // dv, (4 * d) // dv

    def lg_spec():
        return pl.BlockSpec((1, 1, LANES), lambda b, h: (h, 0, 0))

    return pl.pallas_call(
        functools.partial(_ret_kernel, n_ctx=cn // c, n_lat=s_len // c),
        grid=(b_sz, h_n),
        in_specs=[
            lg_spec(), lg_spec(),
            pl.BlockSpec((1, dv), lambda b, h: (0, h)),
            pl.BlockSpec((cn, dk), lambda b, h: (c0 + b, kq + h)),
            pl.BlockSpec((cn, dk), lambda b, h: (c0 + b, kk + h)),
            pl.BlockSpec((cn, dv), lambda b, h: (c0 + b, kv + h)),
            pl.BlockSpec((s_len, dk), lambda b, h: (b, kq + h)),
            pl.BlockSpec((s_len, dk), lambda b, h: (b, kk + h)),
            pl.BlockSpec((s_len, dv), lambda b, h: (b, kv + h)),
            pl.BlockSpec((s_len, dv), lambda b, h: (b, kg + h)),
        ],
        out_specs=pl.BlockSpec((s_len, dv), lambda b, h: (b, h)),
        out_shape=jax.ShapeDtypeStruct((b_sz * s_len, h_n * dv), BF16),
        scratch_shapes=[
            pltpu.VMEM((s_len, dv), F32),
            pltpu.VMEM((2, dk, dv), F32),
            pltpu.VMEM((2, c, c), F32),
            pltpu.VMEM((2, c, LANES), F32),
            pltpu.VMEM((2, c, LANES), F32),
        ],
        name="retention",
        compiler_params=_cparams(("arbitrary", "arbitrary")),
    )(lg_f, lg_b, gn_g, r, r, r, r, r, r, r)


def _rope_angles(s_len, rot_dim):
    rows = s_len // GRID_W
    row = np.repeat(np.arange(rows, dtype=np.float32), GRID_W)
    col = (np.arange(s_len) % GRID_W).astype(np.float32)
    n_freq = rot_dim // 4
    inv = (np.float32(ROPE_BASE) ** (-np.arange(n_freq, dtype=np.float32) / np.float32(n_freq))).astype(np.float32)
    ang = np.concatenate([row[:, None] * inv, col[:, None] * inv], axis=-1).astype(np.float32)
    return np.cos(ang).astype(np.float32), np.sin(ang).astype(np.float32)


def _position_table(lat_table, ctx_row, pad_rows):
    width = lat_table.shape[1]
    return jnp.asarray(np.concatenate([lat_table, np.broadcast_to(ctx_row[None, :], (pad_rows, width))],
                                      axis=0).astype(np.float32))


def kernel(x, c, ctx, c_ctx, mod_w, mod_b, norm_mix_g, norm_ffn_g, ffn_w_gate, ffn_w_up, ffn_w_down, ab_w_in, ab_w_out, swa_sink, mla_q_norm_g, mla_w_q_b, mla_kv_norm_g, mla_w_kv_b, ret_w_in, ret_decay_logit_fwd, ret_decay_logit_bwd, ret_gn_g, ret_w_out, final_norm_g):
    b_sz, s_len, d = x.shape
    cn = ctx.shape[1]
    depth = mod_w.shape[0]
    assert depth == 2 and ab_w_in.shape[0] == 1 and ret_w_in.shape[0] == 1
    assert b_sz + 1 <= 8
    n_lat_rows = b_sz * s_len
    n_ctx_rows = b_sz * cn
    nt = n_lat_rows + n_ctx_rows

    tm = min(1024, n_ctx_rows, s_len)
    tq = min(512, s_len)
    tk = min(1024, s_len)
    assert n_ctx_rows % tm == 0 and s_len % tm == 0 and s_len % cn == 0 and s_len % tq == 0 and s_len % tk == 0

    def make_mrow(t):
        n_lat_tiles = n_lat_rows // t
        per_b = s_len // t
        return lambda i: jnp.where(i < n_lat_tiles, 1 + i // per_b, 0)

    mrow = make_mrow(tm)
    tm_o = min(512, tm)
    mrow_o = make_mrow(tm_o)

    x2d = x.reshape(n_lat_rows, d)
    ctx2d = ctx.reshape(n_ctx_rows, d)

    cond8 = jnp.zeros((8, d), F32).at[0].set(c_ctx).at[1:1 + b_sz].set(c)
    mod_all = _modulation(cond8, mod_w, mod_b).reshape(depth, 8, 6, d)

    cos_a, sin_a = _rope_angles(s_len, SWA_HEAD_DIM)
    cos_b, sin_b = _rope_angles(s_len, MLA_ROPE_DIM)
    cos_r, sin_r = _rope_angles(s_len, d // RET_HEADS)
    ones = np.ones((LANES,), np.float32)
    zeros = np.zeros((LANES,), np.float32)
    half = np.concatenate([np.ones((64,), np.float32), np.zeros((64,), np.float32)])
    z64 = np.zeros((s_len, 64), np.float32)
    table = functools.partial(_position_table, pad_rows=tm)
    per_b = s_len // tm
    trow = lambda i: jnp.where(i < n_lat_rows // tm, i % per_b, per_b)
    t_cos_a = table(np.concatenate([cos_a, cos_a], axis=1), ones)
    t_sin_a = table(np.concatenate([-sin_a, sin_a], axis=1), zeros)
    t_cos_b = table(np.concatenate([cos_b, cos_b, z64], axis=1), half)
    t_sin_b = table(np.concatenate([-sin_b, sin_b, z64], axis=1), zeros)
    t_cos_r = table(cos_r, ones)
    t_sin_r = table(sin_r, zeros)

    bf = lambda w: w.astype(BF16)
    w_ab_in = jnp.pad(ab_w_in[0], ((0, 0), (0, 2560 - ab_w_in.shape[2]))).astype(BF16)
    wq_b = jnp.pad(mla_w_q_b[0].reshape(MLA_Q_RANK, MLA_HEADS, MLA_NOPE_DIM + MLA_ROPE_DIM),
                   ((0, 0), (0, 0), (0, MLA_HEAD_PAD - MLA_NOPE_DIM - MLA_ROPE_DIM))
                   ).reshape(MLA_Q_RANK, MLA_HEADS * MLA_HEAD_PAD).astype(BF16)
    wkv_b = bf(mla_w_kv_b[0])
    tf = 512
    final_g = final_norm_g[None, :]
    f_hidden = ffn_w_gate.shape[2]
    later_weights = [ffn_w_gate.reshape(depth * d, f_hidden), ffn_w_up.reshape(depth * d, f_hidden),
                     ffn_w_down.reshape(depth * f_hidden, d), ab_w_out[0], ret_w_in[0], ret_w_out[0]]

    a_qkv, lat = _ab_in_proj(x2d, ctx2d, norm_mix_g[0:1], mod_all, 0, w_ab_in, t_cos_a, t_sin_a, mrow, trow, tm)
    qm, kcat, vext = _mla_proj(lat, mla_q_norm_g[0:1], mla_kv_norm_g[0:1], wq_b, wkv_b, t_cos_b, t_sin_b, trow, tm)
    oa, oa_c = _swa_attention(a_qkv, swa_sink[0], b_sz, cn, s_len, tq)
    ob, ob_c, (wg, wu, wd, w_ab_out, w_ret_in, w_ret_out) = _mla_attention(
        qm, kcat, vext, later_weights, b_sz, cn, s_len, min(1024, s_len), tk)
    wg = wg.reshape(depth, d, f_hidden)
    wu = wu.reshape(depth, d, f_hidden)
    wd = wd.reshape(depth, f_hidden, d)
    xt = _out_proj([oa, ob], x2d, [oa_c, ob_c], ctx2d, w_ab_out, mod_all, 0, mrow_o, tm_o)
    xt = _ffn(xt, norm_ffn_g[0:1], mod_all, 0, wg, wu, wd, final_g, mrow, nt, tm, tf, final=False)

    r = _ret_in_proj(xt, norm_mix_g[1:2], mod_all, 1, w_ret_in, t_cos_r, t_sin_r, mrow, trow, tm, 1024)
    lg_shape = (RET_HEADS, 1, LANES)
    lg_f = jnp.broadcast_to(ret_decay_logit_fwd[0].astype(F32)[:, None, None], lg_shape)
    lg_b = jnp.broadcast_to(ret_decay_logit_bwd[0].astype(F32)[:, None, None], lg_shape)
    y = _retention(r, lg_f, lg_b, ret_gn_g[0:1], b_sz, cn, s_len, d)
    xl = _out_proj([y], xt, None, None, w_ret_out, mod_all, 1, mrow_o, tm_o)
    out = _ffn(xl, norm_ffn_g[1:2], mod_all, 1, wg, wu, wd, final_g, mrow, n_lat_rows, tm, tf, final=True)
    return out.reshape(b_sz, s_len, d)
```

```python
import functools

import jax
import jax.numpy as jnp
import numpy as np
from jax import lax
from jax.experimental import pallas as pl
from jax.experimental.pallas import tpu as pltpu

GRID_W = 64
ROPE_BASE = 10000.0
NORM_EPS = 1e-6
NEG_INF = -1e30
LOG2_E = 1.4426950408889634

SWA_HEADS = 8
SWA_KV_HEADS = 2
SWA_HEAD_DIM = 128
SWA_WINDOW = 128
SWA_QSCALE = SWA_HEAD_DIM ** -0.5 * LOG2_E

MLA_HEADS = 8
MLA_Q_RANK = 512
MLA_KV_RANK = 256
MLA_NOPE_DIM = 128
MLA_ROPE_DIM = 64
MLA_V_DIM = 128
MLA_HEAD_PAD = 256
MLA_QSCALE = (MLA_NOPE_DIM + MLA_ROPE_DIM) ** -0.5 * LOG2_E
MLA_HEADS_PER_STEP = 2

RET_HEADS = 8
RET_CHUNK = 256

LANES = 128
V7X_VMEM_BYTES = 64 * 1024 * 1024
VMEM_LIMIT = V7X_VMEM_BYTES - 8 * 1024 * 1024

F32 = jnp.float32
BF16 = jnp.bfloat16


def _cparams(sem):
    return pltpu.CompilerParams(dimension_semantics=sem, vmem_limit_bytes=VMEM_LIMIT)


def _dot(a, b):
    return jnp.dot(a, b, preferred_element_type=F32)


def _dot_nt(a, b):
    return lax.dot_general(a, b, (((1,), (1,)), ((), ())), preferred_element_type=F32)


def _dot_tn(a, b):
    return lax.dot_general(a, b, (((0,), (0,)), ((), ())), preferred_element_type=F32)


def _silu(x):
    h = 0.5 * x
    return h + h * jnp.tanh(h)


NORM_ROWS = 32
SUB_N = 256


def _norm_mod_into(h_ref, x_ref, g_ref, mod_ref, shift_row):
    shift = mod_ref[0, 0, shift_row:shift_row + 1, :]
    gain = g_ref[...] * (1.0 + mod_ref[0, 0, shift_row + 1:shift_row + 2, :])

    def body(t, carry):
        sl = pl.ds(pl.multiple_of(t * NORM_ROWS, NORM_ROWS), NORM_ROWS)
        x = x_ref[sl, :]
        inv = lax.rsqrt(jnp.mean(x * x, axis=-1, keepdims=True) + NORM_EPS)
        h_ref[sl, :] = ((x * inv) * gain + shift).astype(BF16)
        return carry

    lax.fori_loop(0, x_ref.shape[0] // NORM_ROWS, body, 0, unroll=4)


def _split_rows(n_lat_tiles):
    lat = lambda i, j: (jnp.minimum(i, n_lat_tiles - 1), 0)
    ctx = lambda i, j: (jnp.maximum(i - n_lat_tiles, 0), 0)
    return lat, ctx


def _mod_kernel(c_ref, w_ref, b_ref, o_ref):
    a = _silu(c_ref[...]).astype(BF16)
    o_ref[0] = _dot(a, w_ref[0].astype(BF16)) + b_ref[0]


def _modulation(cond8, mod_w, mod_b):
    depth, d, n = mod_w.shape
    tn = 1024
    return pl.pallas_call(
        _mod_kernel,
        grid=(depth, n // tn),
        in_specs=[
            pl.BlockSpec((8, d), lambda l, j: (0, 0)),
            pl.BlockSpec((1, d, tn), lambda l, j: (l, 0, j)),
            pl.BlockSpec((1, 1, tn), lambda l, j: (l, 0, j)),
        ],
        out_specs=pl.BlockSpec((1, 8, tn), lambda l, j: (l, 0, j)),
        out_shape=jax.ShapeDtypeStruct((depth, 8, n), F32),
        name="modulation",
        compiler_params=_cparams(("arbitrary", "arbitrary")),
    )(cond8, mod_w, mod_b.reshape(depth, 1, n))


def _rope128(z, cosf, sins):
    return z * cosf + pltpu.roll(z, 64, axis=1) * sins


def _ab_in_kernel(x_ref, c_ref, g_ref, mod_ref, w_ref, cos_ref, sin_ref, a_ref, l_ref, h_ref, *, n_lat_tiles):
    i = pl.program_id(0)
    j = pl.program_id(1)

    pl.when((j == 0) & (i < n_lat_tiles))(lambda: _norm_mod_into(h_ref, x_ref, g_ref, mod_ref, 0))
    pl.when((j == 0) & (i >= n_lat_tiles))(lambda: _norm_mod_into(h_ref, c_ref, g_ref, mod_ref, 0))

    def block(kinds):
        h = h_ref[...]
        for half, kind in enumerate(kinds):
            z = _dot(h, w_ref[:, half * SUB_N:(half + 1) * SUB_N])
            for k in range(SUB_N // LANES):
                sl = slice(half * SUB_N + k * LANES, half * SUB_N + (k + 1) * LANES)
                zk = z[:, k * LANES:(k + 1) * LANES]
                if kind == "q":
                    a_ref[:, sl] = (_rope128(zk, cos_ref[...], sin_ref[...]) * SWA_QSCALE).astype(BF16)
                elif kind == "k":
                    a_ref[:, sl] = _rope128(zk, cos_ref[...], sin_ref[...]).astype(BF16)
                elif kind == "v":
                    a_ref[:, sl] = zk.astype(BF16)
                else:
                    l_ref[:, sl] = zk

    pl.when(j <= 1)(lambda: block(("q", "q")))
    pl.when(j == 2)(lambda: block(("k", "v")))
    pl.when(j >= 3)(lambda: block(("f32", "f32")))


def _ab_in_proj(x2d, ctx2d, norm_g, mod_all, layer, w_in, cos_a, sin_a, mrow, trow, tm):
    d = x2d.shape[1]
    n_lat_tiles = x2d.shape[0] // tm
    nt = x2d.shape[0] + ctx2d.shape[0]
    tn = 512
    lat_map, ctx_map = _split_rows(n_lat_tiles)
    return pl.pallas_call(
        functools.partial(_ab_in_kernel, n_lat_tiles=n_lat_tiles),
        grid=(nt // tm, 5),
        in_specs=[
            pl.BlockSpec((tm, d), lat_map),
            pl.BlockSpec((tm, d), ctx_map, pipeline_mode=pl.Buffered(1)),
            pl.BlockSpec((1, d), lambda i, j: (0, 0)),
            pl.BlockSpec((1, 1, 6, d), lambda i, j: (layer, mrow(i), 0, 0)),
            pl.BlockSpec((d, tn), lambda i, j: (0, j)),
            pl.BlockSpec((tm, LANES), lambda i, j: (trow(i), 0)),
            pl.BlockSpec((tm, LANES), lambda i, j: (trow(i), 0)),
        ],
        out_specs=[
            pl.BlockSpec((tm, tn), lambda i, j: (i, jnp.minimum(j, 2))),
            pl.BlockSpec((tm, tn), lambda i, j: (i, jnp.maximum(j - 3, 0))),
        ],
        out_shape=[
            jax.ShapeDtypeStruct((nt, 3 * tn), BF16),
            jax.ShapeDtypeStruct((nt, 2 * tn), F32),
        ],
        scratch_shapes=[pltpu.VMEM((tm, d), BF16)],
        name="ab_in_proj",
        compiler_params=_cparams(("arbitrary", "arbitrary")),
    )(x2d, ctx2d, norm_g, mod_all, w_in, cos_a, sin_a)


def _rope64(r, cosp, sinp):
    lane = lax.broadcasted_iota(jnp.int32, r.shape, 1)
    partner = jnp.where(lane < 32, pltpu.roll(r, 96, axis=1), pltpu.roll(r, 32, axis=1))
    return r * cosp + partner * sinp


def _mla_proj_kernel(l_ref, qg_ref, kvg_ref, wq_ref, wkv_ref, cos_ref, sin_ref, qm_ref, kc_ref, vm_ref):
    cosp = cos_ref[...]
    sinp = sin_ref[...]

    def rms(x, g):
        return x * lax.rsqrt(jnp.mean(x * x, axis=-1, keepdims=True) + NORM_EPS) * g

    qn = rms(l_ref[:, :MLA_Q_RANK], qg_ref[...]).astype(BF16)
    qm = _dot(qn, wq_ref[...])
    kvn = rms(l_ref[:, MLA_Q_RANK:MLA_Q_RANK + MLA_KV_RANK], kvg_ref[...]).astype(BF16)
    kv = _dot(kvn, wkv_ref[...])
    kr = l_ref[:, MLA_Q_RANK + MLA_KV_RANK:MLA_Q_RANK + MLA_KV_RANK + LANES]
    krr = _rope64(kr, cosp, sinp).astype(BF16)
    ones = jnp.ones((l_ref.shape[0], LANES), BF16)
    for h in range(MLA_HEADS):
        o = h * MLA_HEAD_PAD
        qm_ref[:, o:o + LANES] = (qm[:, o:o + LANES] * MLA_QSCALE).astype(BF16)
        qm_ref[:, o + LANES:o + 2 * LANES] = (
            _rope64(qm[:, o + LANES:o + 2 * LANES], cosp, sinp) * MLA_QSCALE).astype(BF16)
        kc_ref[:, o:o + LANES] = kv[:, o:o + LANES].astype(BF16)
        kc_ref[:, o + LANES:o + 2 * LANES] = krr
        vm_ref[:, o:o + LANES] = kv[:, o + LANES:o + 2 * LANES].astype(BF16)
        vm_ref[:, o + LANES:o + 2 * LANES] = ones


def _mla_proj(lat, q_norm_g, kv_norm_g, wq, wkv, cos_b, sin_b, trow, tm):
    nt = lat.shape[0]
    hp = MLA_HEADS * MLA_HEAD_PAD
    return pl.pallas_call(
        _mla_proj_kernel,
        grid=(nt // tm,),
        in_specs=[
            pl.BlockSpec((tm, lat.shape[1]), lambda i: (i, 0)),
            pl.BlockSpec((1, MLA_Q_RANK), lambda i: (0, 0)),
            pl.BlockSpec((1, MLA_KV_RANK), lambda i: (0, 0)),
            pl.BlockSpec(wq.shape, lambda i: (0, 0)),
            pl.BlockSpec(wkv.shape, lambda i: (0, 0)),
            pl.BlockSpec((tm, LANES), lambda i: (trow(i), 0)),
            pl.BlockSpec((tm, LANES), lambda i: (trow(i), 0)),
        ],
        out_specs=[
            pl.BlockSpec((tm, hp), lambda i: (i, 0)),
            pl.BlockSpec((tm, hp), lambda i: (i, 0)),
            pl.BlockSpec((tm, hp), lambda i: (i, 0)),
        ],
        out_shape=[
            jax.ShapeDtypeStruct((nt, hp), BF16),
            jax.ShapeDtypeStruct((nt, hp), BF16),
            jax.ShapeDtypeStruct((nt, hp), BF16),
        ],
        name="mla_proj",
        compiler_params=_cparams(("arbitrary",)),
    )(lat, q_norm_g, kv_norm_g, wq, wkv, cos_b, sin_b)


def _swa_softmax_pv(s, sk, v_ext):
    d = SWA_HEAD_DIM
    m = jnp.maximum(jnp.max(s, axis=-1, keepdims=True), sk)
    o_ext = _dot(jnp.exp2(s - m).astype(BF16), v_ext)
    return o_ext[:, :d] / (o_ext[:, d:] + jnp.exp2(sk - m))


def _swa_ctx_kernel(sink_ref, q_ref, kc, vc, o_ref):
    kvh = pl.program_id(1)
    d = SWA_HEAD_DIM
    g_heads = SWA_HEADS // SWA_KV_HEADS
    k_all = kc[...]
    v_ext = jnp.concatenate([vc[...], jnp.ones_like(vc)], axis=1)
    for g in range(g_heads):
        sk = sink_ref[kvh * g_heads + g] * LOG2_E
        s = _dot_nt(q_ref[:, g * d:(g + 1) * d], k_all)
        o_ref[:, g * d:(g + 1) * d] = _swa_softmax_pv(s, sk, v_ext).astype(BF16)


def _swa_kernel(sink_ref, q_ref, kp, km, kn, kc, vp, vm, vn, vc, o_ref, *, tq, s_len):
    i = pl.program_id(1)
    d = SWA_HEAD_DIM
    g_heads = SWA_HEADS // SWA_KV_HEADS
    nb = tq + 2 * SWA_WINDOW
    shape = (tq, nb + kc.shape[0])
    row = lax.broadcasted_iota(jnp.int32, shape, 0)
    col = lax.broadcasted_iota(jnp.int32, shape, 1)
    kpos = i * tq - SWA_WINDOW + col
    valid = (col >= nb) | ((jnp.abs(col - SWA_WINDOW - row) <= SWA_WINDOW) & (kpos >= 0) & (kpos < s_len))
    for kvh in range(SWA_KV_HEADS):
        kv = slice(kvh * d, (kvh + 1) * d)
        k_all = jnp.concatenate([kp[:, kv], km[:, kv], kn[:, kv], kc[:, kv]], axis=0)
        v_all = jnp.concatenate([vp[:, kv], vm[:, kv], vn[:, kv], vc[:, kv]], axis=0)
        v_ext = jnp.concatenate([v_all, jnp.ones_like(v_all)], axis=1)
        for g in range(g_heads):
            head = kvh * g_heads + g
            sk = sink_ref[head] * LOG2_E
            s = jnp.where(valid, _dot_nt(q_ref[:, head * d:(head + 1) * d], k_all), NEG_INF)
            o_ref[:, head * d:(head + 1) * d] = _swa_softmax_pv(s, sk, v_ext).astype(BF16)


def _swa_attention(a, sink, b_sz, cn, s_len, tq):
    d = SWA_HEAD_DIM
    w = SWA_WINDOW
    gw = (SWA_HEADS // SWA_KV_HEADS) * d
    kcol = SWA_HEADS
    vcol = SWA_HEADS + SWA_KV_HEADS
    nq = s_len // tq
    r = tq // w
    c0 = (b_sz * s_len) // cn
    smem = pl.BlockSpec(memory_space=pltpu.SMEM)

    def main(b, i):
        return b * nq + i

    def prev(b, i):
        return b * (s_len // w) + jnp.maximum(i * r - 1, 0)

    def nxt(b, i):
        return b * (s_len // w) + jnp.minimum((i + 1) * r, s_len // w - 1)

    qw = SWA_HEADS * d
    kw = SWA_KV_HEADS * d
    kb = qw // kw
    o_lat = pl.pallas_call(
        functools.partial(_swa_kernel, tq=tq, s_len=s_len),
        grid=(b_sz, nq),
        in_specs=[
            smem,
            pl.BlockSpec((tq, qw), lambda b, i: (main(b, i), 0)),
            pl.BlockSpec((w, kw), lambda b, i: (prev(b, i), kb)),
            pl.BlockSpec((tq, kw), lambda b, i: (main(b, i), kb)),
            pl.BlockSpec((w, kw), lambda b, i: (nxt(b, i), kb)),
            pl.BlockSpec((cn, kw), lambda b, i: (c0 + b, kb)),
            pl.BlockSpec((w, kw), lambda b, i: (prev(b, i), kb + 1)),
            pl.BlockSpec((tq, kw), lambda b, i: (main(b, i), kb + 1)),
            pl.BlockSpec((w, kw), lambda b, i: (nxt(b, i), kb + 1)),
            pl.BlockSpec((cn, kw), lambda b, i: (c0 + b, kb + 1)),
        ],
        out_specs=pl.BlockSpec((tq, qw), lambda b, i: (main(b, i), 0)),
        out_shape=jax.ShapeDtypeStruct((b_sz * s_len, qw), BF16),
        name="swa_latent",
        compiler_params=_cparams(("arbitrary", "arbitrary")),
    )(sink, a, a, a, a, a, a, a, a, a)

    o_ctx = pl.pallas_call(
        _swa_ctx_kernel,
        grid=(b_sz, SWA_KV_HEADS),
        in_specs=[
            smem,
            pl.BlockSpec((cn, gw), lambda b, h: (c0 + b, h)),
            pl.BlockSpec((cn, d), lambda b, h: (c0 + b, kcol + h)),
            pl.BlockSpec((cn, d), lambda b, h: (c0 + b, vcol + h)),
        ],
        out_specs=pl.BlockSpec((cn, gw), lambda b, h: (b, h)),
        out_shape=jax.ShapeDtypeStruct((b_sz * cn, SWA_HEADS * d), BF16),
        name="swa_context",
        compiler_params=_cparams(("arbitrary", "arbitrary")),
    )(sink, a, a, a)
    return o_lat, o_ctx


def _mla_kernel(q_ref, kc_ref, vc_ref, *refs, tk, latent, n_cast=0):
    if latent:
        kl_ref, vl_ref = refs[:2]
        cast_in = refs[2:2 + n_cast]
        o_ref = refs[2 + n_cast]
        cast_out = refs[3 + n_cast:]
        for src, dst in zip(cast_in, cast_out):
            dst[...] = src[...].astype(BF16)
    else:
        (o_ref,) = refs
    dv = MLA_V_DIM
    hp = MLA_HEAD_PAD
    n_heads = q_ref.shape[1] // hp
    chunks = []
    if latent:
        chunks += [(kl_ref, vl_ref, slice(j * tk, (j + 1) * tk)) for j in range(kl_ref.shape[0] // tk)]
    chunks.append((kc_ref, vc_ref, slice(None)))
    m = [None] * n_heads
    acc = [None] * n_heads
    for k_ref, v_ref, sl in chunks:
        for hh in range(n_heads):
            cols = slice(hh * hp, (hh + 1) * hp)
            s = _dot_nt(q_ref[:, cols], k_ref[sl, cols])
            if m[hh] is None:
                m[hh] = jnp.max(s, axis=-1, keepdims=True)
                acc[hh] = _dot(jnp.exp2(s - m[hh]).astype(BF16), v_ref[sl, cols])
            else:
                m_new = jnp.maximum(m[hh], jnp.max(s, axis=-1, keepdims=True))
                acc[hh] = (jnp.exp2(m[hh] - m_new) * acc[hh]
                           + _dot(jnp.exp2(s - m_new).astype(BF16), v_ref[sl, cols]))
                m[hh] = m_new
    for hh in range(n_heads):
        o_ref[:, hh * dv:(hh + 1) * dv] = (acc[hh][:, :dv] / acc[hh][:, dv:]).astype(BF16)


BF16_SUBLANES = 16


def _cast_block(rows, n_steps):
    share = 1
    while (rows * share) % n_steps or (rows * share // n_steps) % BF16_SUBLANES:
        share *= 2
        assert share <= n_steps
    return rows * share // n_steps, share


def _mla_attention(qm, kcat, vext, cast_weights, b_sz, cn, s_len, tq, tk):
    hp = MLA_HEAD_PAD * MLA_HEADS_PER_STEP
    dv = MLA_V_DIM * MLA_HEADS_PER_STEP
    n_hg = MLA_HEADS // MLA_HEADS_PER_STEP
    nq = s_len // tq
    c0 = (b_sz * s_len) // cn
    n_steps = b_sz * n_hg * nq

    cast_specs = []
    for w in cast_weights:
        rows, share = _cast_block(w.shape[0], n_steps)
        cast_specs.append(pl.BlockSpec(
            (rows, w.shape[1]), lambda b, h, i, share=share: (((b * n_hg + h) * nq + i) // share, 0)))

    o_lat, *w_bf16 = pl.pallas_call(
        functools.partial(_mla_kernel, tk=tk, latent=True, n_cast=len(cast_weights)),
        grid=(b_sz, n_hg, nq),
        in_specs=[
            pl.BlockSpec((tq, hp), lambda b, h, i: (b * nq + i, h)),
            pl.BlockSpec((cn, hp), lambda b, h, i: (c0 + b, h)),
            pl.BlockSpec((cn, hp), lambda b, h, i: (c0 + b, h)),
            pl.BlockSpec((s_len, hp), lambda b, h, i: (b, h)),
            pl.BlockSpec((s_len, hp), lambda b, h, i: (b, h)),
        ] + cast_specs,
        out_specs=[pl.BlockSpec((tq, dv), lambda b, h, i: (b * nq + i, h))] + cast_specs,
        out_shape=[jax.ShapeDtypeStruct((b_sz * s_len, n_hg * dv), BF16)]
        + [jax.ShapeDtypeStruct(w.shape, BF16) for w in cast_weights],
        name="mla_latent",
        compiler_params=_cparams(("arbitrary",) * 3),
    )(qm, kcat, vext, kcat, vext, *cast_weights)

    o_ctx = pl.pallas_call(
        functools.partial(_mla_kernel, tk=0, latent=False),
        grid=(b_sz, n_hg),
        in_specs=[
            pl.BlockSpec((cn, hp), lambda b, h: (c0 + b, h)),
            pl.BlockSpec((cn, hp), lambda b, h: (c0 + b, h)),
            pl.BlockSpec((cn, hp), lambda b, h: (c0 + b, h)),
        ],
        out_specs=pl.BlockSpec((cn, dv), lambda b, h: (b, h)),
        out_shape=jax.ShapeDtypeStruct((b_sz * cn, n_hg * dv), BF16),
        name="mla_context",
        compiler_params=_cparams(("arbitrary", "arbitrary")),
    )(qm, kcat, vext)
    return o_lat, o_ctx, w_bf16


def _out_proj_kernel(w_ref, mod_ref, *refs, n_parts, n_lat_tiles):
    o_ref = refs[-1]
    lat = refs[:n_parts + 1]
    ctx = refs[n_parts + 1:-1]

    def compute(side):
        parts = [p[...] for p in side[:-1]]
        res_ref = side[-1]
        for sub in range(o_ref.shape[1] // SUB_N):
            cols = slice(sub * SUB_N, (sub + 1) * SUB_N)
            acc = None
            k0 = 0
            for p in parts:
                kw = p.shape[1]
                t = _dot(p, w_ref[k0:k0 + kw, cols])
                acc = t if acc is None else acc + t
                k0 += kw
            o_ref[:, cols] = res_ref[:, cols] + mod_ref[0, 0, 2:3, cols] * acc

    if n_lat_tiles is None:
        compute(lat)
    else:
        i = pl.program_id(0)
        pl.when(i < n_lat_tiles)(lambda: compute(lat))
        pl.when(i >= n_lat_tiles)(lambda: compute(ctx))


def _out_proj(lat_parts, lat_res, ctx_parts, ctx_res, w, mod_all, layer, mrow, tm):
    lat_rows = lat_parts[0].shape[0]
    d = lat_res.shape[1]
    n_lat = lat_rows // tm
    if ctx_parts is None:
        n_lat_tiles = None
        rows = lat_rows
        specs = [pl.BlockSpec((tm, p.shape[1]), lambda i: (i, 0)) for p in lat_parts]
        specs.append(pl.BlockSpec((tm, d), lambda i: (i, 0)))
        operands = [*lat_parts, lat_res]
    else:
        n_lat_tiles = n_lat
        rows = lat_rows + ctx_parts[0].shape[0]
        lat_map = lambda i: (jnp.minimum(i, n_lat - 1), 0)
        ctx_map = lambda i: (jnp.maximum(i - n_lat, 0), 0)
        specs = [pl.BlockSpec((tm, p.shape[1]), lat_map) for p in lat_parts]
        specs.append(pl.BlockSpec((tm, d), lat_map))
        specs += [pl.BlockSpec((tm, p.shape[1]), ctx_map) for p in ctx_parts]
        specs.append(pl.BlockSpec((tm, d), ctx_map))
        operands = [*lat_parts, lat_res, *ctx_parts, ctx_res]
    return pl.pallas_call(
        functools.partial(_out_proj_kernel, n_parts=len(lat_parts), n_lat_tiles=n_lat_tiles),
        grid=(rows // tm,),
        in_specs=[
            pl.BlockSpec(w.shape, lambda i: (0, 0), pipeline_mode=pl.Buffered(1)),
            pl.BlockSpec((1, 1, 6, d), lambda i: (layer, mrow(i), 0, 0)),
        ] + specs,
        out_specs=pl.BlockSpec((tm, d), lambda i: (i, 0)),
        out_shape=jax.ShapeDtypeStruct((rows, d), F32),
        name="out_proj",
        compiler_params=_cparams(("arbitrary",)),
    )(w, mod_all, *operands)


def _ffn_kernel(x_ref, g_ref, mod_ref, wg_ref, wu_ref, wd_ref, fg_ref, o_ref, h_ref, *, nf, final):
    j = pl.program_id(1)

    @pl.when(j == 0)
    def _():
        _norm_mod_into(h_ref, x_ref, g_ref, mod_ref, 3)
        o_ref[...] = jnp.zeros_like(o_ref)

    h = h_ref[...]
    a = (_silu(_dot(h, wg_ref[0].astype(BF16))) * _dot(h, wu_ref[0].astype(BF16))).astype(BF16)
    o_ref[...] += _dot(a, wd_ref[0].astype(BF16))

    @pl.when(j == nf - 1)
    def _():
        gate = mod_ref[0, 0, 5:6, :]

        def body(t, carry):
            sl = pl.ds(pl.multiple_of(t * NORM_ROWS, NORM_ROWS), NORM_ROWS)
            y = x_ref[sl, :] + gate * o_ref[sl, :]
            if final:
                y = y * lax.rsqrt(jnp.mean(y * y, axis=-1, keepdims=True) + NORM_EPS) * fg_ref[...]
            o_ref[sl, :] = y
            return carry

        lax.fori_loop(0, x_ref.shape[0] // NORM_ROWS, body, 0, unroll=4)


def _ffn(xt, norm_g, mod_all, layer, wg, wu, wd, final_g, mrow, n_rows, tm, tf, final):
    d = xt.shape[1]
    f = wg.shape[2]
    nf = f // tf
    return pl.pallas_call(
        functools.partial(_ffn_kernel, nf=nf, final=final),
        grid=(n_rows // tm, nf),
        in_specs=[
            pl.BlockSpec((tm, d), lambda i, j: (i, 0)),
            pl.BlockSpec((1, d), lambda i, j: (0, 0)),
            pl.BlockSpec((1, 1, 6, d), lambda i, j: (layer, mrow(i), 0, 0)),
            pl.BlockSpec((1, d, tf), lambda i, j: (layer, 0, j)),
            pl.BlockSpec((1, d, tf), lambda i, j: (layer, 0, j)),
            pl.BlockSpec((1, tf, d), lambda i, j: (layer, j, 0)),
            pl.BlockSpec((1, d), lambda i, j: (0, 0)),
        ],
        out_specs=pl.BlockSpec((tm, d), lambda i, j: (i, 0)),
        out_shape=jax.ShapeDtypeStruct((n_rows, d), F32),
        scratch_shapes=[pltpu.VMEM((tm, d), BF16)],
        name="ffn_final" if final else "ffn",
        compiler_params=_cparams(("arbitrary", "arbitrary")),
    )(xt, norm_g, mod_all, wg, wu, wd, final_g)


def _ret_in_kernel(x_ref, g_ref, mod_ref, w_ref, cos_ref, sin_ref, o_ref, h_ref, *, n_qk, k_scale, tn):
    j = pl.program_id(1)

    pl.when(j == 0)(lambda: _norm_mod_into(h_ref, x_ref, g_ref, mod_ref, 0))

    def block(scale):
        h = h_ref[...]
        for sub in range(tn // SUB_N):
            o = sub * SUB_N
            z = _dot(h, w_ref[:, o:o + SUB_N])
            if scale is None:
                o_ref[:, o:o + SUB_N] = z.astype(BF16)
            else:
                c = cos_ref[...]
                s = sin_ref[...]
                x1 = z[:, :LANES] * scale if scale != 1.0 else z[:, :LANES]
                x2 = z[:, LANES:] * scale if scale != 1.0 else z[:, LANES:]
                o_ref[:, o:o + LANES] = (x1 * c - x2 * s).astype(BF16)
                o_ref[:, o + LANES:o + SUB_N] = (x1 * s + x2 * c).astype(BF16)

    pl.when(j < n_qk)(lambda: block(1.0))
    pl.when((j >= n_qk) & (j < 2 * n_qk))(lambda: block(k_scale))
    pl.when(j >= 2 * n_qk)(lambda: block(None))


def _ret_in_proj(xt, norm_g, mod_all, layer, w_in, cos_r, sin_r, mrow, trow, tm, tn):
    nt, d = xt.shape
    n = w_in.shape[1]
    dk = d // RET_HEADS
    return pl.pallas_call(
        functools.partial(_ret_in_kernel, n_qk=d // tn, k_scale=dk ** -0.5, tn=tn),
        grid=(nt // tm, n // tn),
        in_specs=[
            pl.BlockSpec((tm, d), lambda i, j: (i, 0)),
            pl.BlockSpec((1, d), lambda i, j: (0, 0)),
            pl.BlockSpec((1, 1, 6, d), lambda i, j: (layer, mrow(i), 0, 0)),
            pl.BlockSpec((d, tn), lambda i, j: (0, j)),
            pl.BlockSpec((tm, LANES), lambda i, j: (trow(i), 0)),
            pl.BlockSpec((tm, LANES), lambda i, j: (trow(i), 0)),
        ],
        out_specs=pl.BlockSpec((tm, tn), lambda i, j: (i, j)),
        out_shape=jax.ShapeDtypeStruct((nt, n), BF16),
        scratch_shapes=[pltpu.VMEM((tm, d), BF16)],
        name="ret_in_proj",
        compiler_params=_cparams(("arbitrary", "arbitrary")),
    )(xt, norm_g, mod_all, w_in, cos_r, sin_r)


def _log_sigmoid(x):
    return jnp.minimum(x, 0.0) - jnp.log1p(jnp.exp(-jnp.abs(x)))


def _ret_kernel(lf_ref, lb_ref, gn_ref, qc_ref, kc_ref, vc_ref, q_ref, k_ref, v_ref, g_ref, y_ref,
                o_ref, s_ref, dint_ref, qdec_ref, kdec_ref, *, n_ctx, n_lat):
    c = RET_CHUNK
    dk = q_ref.shape[1]
    dv = v_ref.shape[1]
    row = lax.broadcasted_iota(jnp.int32, (c, c), 0)
    col = lax.broadcasted_iota(jnp.int32, (c, c), 1)
    pos = lax.broadcasted_iota(jnp.int32, (c, LANES), 0).astype(F32)
    c_dec = []
    for drn, l_ref in enumerate((lf_ref, lb_ref)):
        lg = _log_sigmoid(l_ref[0])
        lg_c = jnp.broadcast_to(lg[:, 0:1], (1, c))
        diff = (col - row) if drn else (row - col)
        dint_ref[drn] = jnp.where(diff >= 0, jnp.exp(lg_c * jnp.maximum(diff, 0).astype(F32)), 0.0)
        if drn:
            qdec_ref[drn] = jnp.exp(lg * (c - pos))
            kdec_ref[drn] = jnp.exp(lg * pos)
        else:
            qdec_ref[drn] = jnp.exp(lg * (pos + 1.0))
            kdec_ref[drn] = jnp.exp(lg * (c - 1.0 - pos))
        c_dec.append(jnp.exp(jnp.broadcast_to(lg[:, 0:1], (1, dv)) * c))

    def scale_rows(x, dec):
        return jnp.concatenate([x[:, k * LANES:(k + 1) * LANES].astype(F32) * dec for k in range(dk // LANES)],
                               axis=1).astype(BF16)

    def step(drn, q, k, v, want_out):
        s = s_ref[drn]
        o = None
        if want_out:
            a = (_dot_nt(q, k) * dint_ref[drn]).astype(BF16)
            o = _dot(a, v) + _dot(scale_rows(q, qdec_ref[drn]), s.astype(BF16))
        s_ref[drn] = s * c_dec[drn] + _dot_tn(scale_rows(k, kdec_ref[drn]), v)
        return o

    def rows(t):
        return pl.ds(pl.multiple_of(t * c, c), c)

    def lat_step(drn, t):
        sl = rows(t)
        return step(drn, q_ref[sl, :], k_ref[sl, :], v_ref[sl, :], True)

    def finish(t, o):
        sl = rows(t)
        mu = jnp.mean(o, axis=-1, keepdims=True)
        oc = o - mu
        var = jnp.mean(oc * oc, axis=-1, keepdims=True)
        yn = oc * lax.rsqrt(var + NORM_EPS) * gn_ref[...]
        y_ref[sl, :] = (_silu(g_ref[sl, :].astype(F32)) * yn).astype(BF16)

    s_ref[...] = jnp.zeros_like(s_ref)
    for t in range(n_ctx):
        for drn, tt in ((0, t), (1, n_ctx - 1 - t)):
            sl = slice(tt * c, (tt + 1) * c)
            step(drn, qc_ref[sl, :], kc_ref[sl, :], vc_ref[sl, :], False)

    half = n_lat // 2

    def first_half(t, carry):
        tb = n_lat - 1 - t
        o_ref[rows(t), :] = lat_step(0, t)
        o_ref[rows(tb), :] = lat_step(1, tb)
        return carry

    def second_half(t, carry):
        tb = n_lat - 1 - t
        finish(t, o_ref[rows(t), :] + lat_step(0, t))
        finish(tb, o_ref[rows(tb), :] + lat_step(1, tb))
        return carry

    lax.fori_loop(0, half, first_half, 0, unroll=2)
    lax.fori_loop(half, n_lat, second_half, 0, unroll=2)


def _retention(r, lg_f, lg_b, gn_g, b_sz, cn, s_len, d):
    h_n = RET_HEADS
    dk = d // h_n
    dv = 2 * dk
    c = RET_CHUNK
    assert cn % c == 0 and s_len % (2 * c) == 0
    c0 = (b_sz * s_len) // cn
    kq, kk, kv, kg = 0, d // dk, (2 * d) // dv, (4 * d) // dv

    def lg_spec():
        return pl.BlockSpec((1, 1, LANES), lambda b, h: (h, 0, 0))

    return pl.pallas_call(
        functools.partial(_ret_kernel, n_ctx=cn // c, n_lat=s_len // c),
        grid=(b_sz, h_n),
        in_specs=[
            lg_spec(), lg_spec(),
            pl.BlockSpec((1, dv), lambda b, h: (0, h)),
            pl.BlockSpec((cn, dk), lambda b, h: (c0 + b, kq + h)),
            pl.BlockSpec((cn, dk), lambda b, h: (c0 + b, kk + h)),
            pl.BlockSpec((cn, dv), lambda b, h: (c0 + b, kv + h)),
            pl.BlockSpec((s_len, dk), lambda b, h: (b, kq + h)),
            pl.BlockSpec((s_len, dk), lambda b, h: (b, kk + h)),
            pl.BlockSpec((s_len, dv), lambda b, h: (b, kv + h)),
            pl.BlockSpec((s_len, dv), lambda b, h: (b, kg + h)),
        ],
        out_specs=pl.BlockSpec((s_len, dv), lambda b, h: (b, h)),
        out_shape=jax.ShapeDtypeStruct((b_sz * s_len, h_n * dv), BF16),
        scratch_shapes=[
            pltpu.VMEM((s_len, dv), F32),
            pltpu.VMEM((2, dk, dv), F32),
            pltpu.VMEM((2, c, c), F32),
            pltpu.VMEM((2, c, LANES), F32),
            pltpu.VMEM((2, c, LANES), F32),
        ],
        name="retention",
        compiler_params=_cparams(("arbitrary", "arbitrary")),
    )(lg_f, lg_b, gn_g, r, r, r, r, r, r, r)


def _rope_angles(s_len, rot_dim):
    rows = s_len // GRID_W
    row = np.repeat(np.arange(rows, dtype=np.float32), GRID_W)
    col = (np.arange(s_len) % GRID_W).astype(np.float32)
    n_freq = rot_dim // 4
    inv = (np.float32(ROPE_BASE) ** (-np.arange(n_freq, dtype=np.float32) / np.float32(n_freq))).astype(np.float32)
    ang = np.concatenate([row[:, None] * inv, col[:, None] * inv], axis=-1).astype(np.float32)
    return np.cos(ang).astype(np.float32), np.sin(ang).astype(np.float32)


def _position_table(lat_table, ctx_row, pad_rows):
    width = lat_table.shape[1]
    return jnp.asarray(np.concatenate([lat_table, np.broadcast_to(ctx_row[None, :], (pad_rows, width))],
                                      axis=0).astype(np.float32))


def kernel(x, c, ctx, c_ctx, mod_w, mod_b, norm_mix_g, norm_ffn_g, ffn_w_gate, ffn_w_up, ffn_w_down, ab_w_in, ab_w_out, swa_sink, mla_q_norm_g, mla_w_q_b, mla_kv_norm_g, mla_w_kv_b, ret_w_in, ret_decay_logit_fwd, ret_decay_logit_bwd, ret_gn_g, ret_w_out, final_norm_g):
    b_sz, s_len, d = x.shape
    cn = ctx.shape[1]
    depth = mod_w.shape[0]
    assert depth == 2 and ab_w_in.shape[0] == 1 and ret_w_in.shape[0] == 1
    assert b_sz + 1 <= 8
    n_lat_rows = b_sz * s_len
    n_ctx_rows = b_sz * cn
    nt = n_lat_rows + n_ctx_rows

    tm = min(1024, n_ctx_rows, s_len)
    tq = min(512, s_len)
    tk = min(1024, s_len)
    assert n_ctx_rows % tm == 0 and s_len % tm == 0 and s_len % cn == 0 and s_len % tq == 0 and s_len % tk == 0

    def make_mrow(t):
        n_lat_tiles = n_lat_rows // t
        per_b = s_len // t
        return lambda i: jnp.where(i < n_lat_tiles, 1 + i // per_b, 0)

    mrow = make_mrow(tm)
    tm_o = min(512, tm)
    mrow_o = make_mrow(tm_o)

    x2d = x.reshape(n_lat_rows, d)
    ctx2d = ctx.reshape(n_ctx_rows, d)

    cond8 = jnp.zeros((8, d), F32).at[0].set(c_ctx).at[1:1 + b_sz].set(c)
    mod_all = _modulation(cond8, mod_w, mod_b).reshape(depth, 8, 6, d)

    cos_a, sin_a = _rope_angles(s_len, SWA_HEAD_DIM)
    cos_b, sin_b = _rope_angles(s_len, MLA_ROPE_DIM)
    cos_r, sin_r = _rope_angles(s_len, d // RET_HEADS)
    ones = np.ones((LANES,), np.float32)
    zeros = np.zeros((LANES,), np.float32)
    half = np.concatenate([np.ones((64,), np.float32), np.zeros((64,), np.float32)])
    z64 = np.zeros((s_len, 64), np.float32)
    table = functools.partial(_position_table, pad_rows=tm)
    per_b = s_len // tm
    trow = lambda i: jnp.where(i < n_lat_rows // tm, i % per_b, per_b)
    t_cos_a = table(np.concatenate([cos_a, cos_a], axis=1), ones)
    t_sin_a = table(np.concatenate([-sin_a, sin_a], axis=1), zeros)
    t_cos_b = table(np.concatenate([cos_b, cos_b, z64], axis=1), half)
    t_sin_b = table(np.concatenate([-sin_b, sin_b, z64], axis=1), zeros)
    t_cos_r = table(cos_r, ones)
    t_sin_r = table(sin_r, zeros)

    bf = lambda w: w.astype(BF16)
    w_ab_in = jnp.pad(ab_w_in[0], ((0, 0), (0, 2560 - ab_w_in.shape[2]))).astype(BF16)
    wq_b = jnp.pad(mla_w_q_b[0].reshape(MLA_Q_RANK, MLA_HEADS, MLA_NOPE_DIM + MLA_ROPE_DIM),
                   ((0, 0), (0, 0), (0, MLA_HEAD_PAD - MLA_NOPE_DIM - MLA_ROPE_DIM))
                   ).reshape(MLA_Q_RANK, MLA_HEADS * MLA_HEAD_PAD).astype(BF16)
    wkv_b = bf(mla_w_kv_b[0])
    tf = 512
    final_g = final_norm_g[None, :]
    f_hidden = ffn_w_gate.shape[2]
    later_weights = [ffn_w_gate.reshape(depth * d, f_hidden), ffn_w_up.reshape(depth * d, f_hidden),
                     ffn_w_down.reshape(depth * f_hidden, d), ab_w_out[0], ret_w_in[0], ret_w_out[0]]

    a_qkv, lat = _ab_in_proj(x2d, ctx2d, norm_mix_g[0:1], mod_all, 0, w_ab_in, t_cos_a, t_sin_a, mrow, trow, tm)
    qm, kcat, vext = _mla_proj(lat, mla_q_norm_g[0:1], mla_kv_norm_g[0:1], wq_b, wkv_b, t_cos_b, t_sin_b, trow, tm)
    oa, oa_c = _swa_attention(a_qkv, swa_sink[0], b_sz, cn, s_len, tq)
    ob, ob_c, (wg, wu, wd, w_ab_out, w_ret_in, w_ret_out) = _mla_attention(
        qm, kcat, vext, later_weights, b_sz, cn, s_len, min(1024, s_len), tk)
    wg = wg.reshape(depth, d, f_hidden)
    wu = wu.reshape(depth, d, f_hidden)
    wd = wd.reshape(depth, f_hidden, d)
    xt = _out_proj([oa, ob], x2d, [oa_c, ob_c], ctx2d, w_ab_out, mod_all, 0, mrow_o, tm_o)
    xt = _ffn(xt, norm_ffn_g[0:1], mod_all, 0, wg, wu, wd, final_g, mrow, nt, tm, tf, final=False)

    r = _ret_in_proj(xt, norm_mix_g[1:2], mod_all, 1, w_ret_in, t_cos_r, t_sin_r, mrow, trow, tm, 2048)
    lg_shape = (RET_HEADS, 1, LANES)
    lg_f = jnp.broadcast_to(ret_decay_logit_fwd[0].astype(F32)[:, None, None], lg_shape)
    lg_b = jnp.broadcast_to(ret_decay_logit_bwd[0].astype(F32)[:, None, None], lg_shape)
    y = _retention(r, lg_f, lg_b, ret_gn_g[0:1], b_sz, cn, s_len, d)
    xl = _out_proj([y], xt, None, None, w_ret_out, mod_all, 1, mrow_o, tm_o)
    out = _ffn(xl, norm_ffn_g[1:2], mod_all, 1, wg, wu, wd, final_g, mrow, n_lat_rows, tm, tf, final=True)
    return out.reshape(b_sz, s_len, d)
```

```python
import functools

import jax
import jax.numpy as jnp
import numpy as np
from jax import lax
from jax.experimental import pallas as pl
from jax.experimental.pallas import tpu as pltpu

GRID_W = 64
ROPE_BASE = 10000.0
NORM_EPS = 1e-6
NEG_INF = -1e30
LOG2_E = 1.4426950408889634

SWA_HEADS = 8
SWA_KV_HEADS = 2
SWA_HEAD_DIM = 128
SWA_WINDOW = 128
SWA_QSCALE = SWA_HEAD_DIM ** -0.5 * LOG2_E

MLA_HEADS = 8
MLA_Q_RANK = 512
MLA_KV_RANK = 256
MLA_NOPE_DIM = 128
MLA_ROPE_DIM = 64
MLA_V_DIM = 128
MLA_HEAD_PAD = 256
MLA_QSCALE = (MLA_NOPE_DIM + MLA_ROPE_DIM) ** -0.5 * LOG2_E
MLA_HEADS_PER_STEP = 2

RET_HEADS = 8
RET_CHUNK = 256

LANES = 128
V7X_VMEM_BYTES = 64 * 1024 * 1024
VMEM_LIMIT = V7X_VMEM_BYTES - 8 * 1024 * 1024

F32 = jnp.float32
BF16 = jnp.bfloat16


def _cparams(sem):
    return pltpu.CompilerParams(dimension_semantics=sem, vmem_limit_bytes=VMEM_LIMIT)


def _dot(a, b):
    return jnp.dot(a, b, preferred_element_type=F32)


def _dot_nt(a, b):
    return lax.dot_general(a, b, (((1,), (1,)), ((), ())), preferred_element_type=F32)


def _dot_tn(a, b):
    return lax.dot_general(a, b, (((0,), (0,)), ((), ())), preferred_element_type=F32)


def _silu(x):
    h = 0.5 * x
    return h + h * jnp.tanh(h)


NORM_ROWS = 32
SUB_N = 256


def _norm_mod_into(h_ref, x_ref, g_ref, mod_ref, shift_row):
    shift = mod_ref[0, 0, shift_row:shift_row + 1, :]
    gain = g_ref[...] * (1.0 + mod_ref[0, 0, shift_row + 1:shift_row + 2, :])

    def body(t, carry):
        sl = pl.ds(pl.multiple_of(t * NORM_ROWS, NORM_ROWS), NORM_ROWS)
        x = x_ref[sl, :]
        inv = lax.rsqrt(jnp.mean(x * x, axis=-1, keepdims=True) + NORM_EPS)
        h_ref[sl, :] = ((x * inv) * gain + shift).astype(BF16)
        return carry

    lax.fori_loop(0, x_ref.shape[0] // NORM_ROWS, body, 0, unroll=4)


def _split_rows(n_lat_tiles):
    lat = lambda i, j: (jnp.minimum(i, n_lat_tiles - 1), 0)
    ctx = lambda i, j: (jnp.maximum(i - n_lat_tiles, 0), 0)
    return lat, ctx


def _mod_kernel(c_ref, w_ref, b_ref, o_ref):
    a = _silu(c_ref[...]).astype(BF16)
    o_ref[0] = _dot(a, w_ref[0].astype(BF16)) + b_ref[0]


def _modulation(cond8, mod_w, mod_b):
    depth, d, n = mod_w.shape
    tn = 1024
    return pl.pallas_call(
        _mod_kernel,
        grid=(depth, n // tn),
        in_specs=[
            pl.BlockSpec((8, d), lambda l, j: (0, 0)),
            pl.BlockSpec((1, d, tn), lambda l, j: (l, 0, j)),
            pl.BlockSpec((1, 1, tn), lambda l, j: (l, 0, j)),
        ],
        out_specs=pl.BlockSpec((1, 8, tn), lambda l, j: (l, 0, j)),
        out_shape=jax.ShapeDtypeStruct((depth, 8, n), F32),
        name="modulation",
        compiler_params=_cparams(("arbitrary", "arbitrary")),
    )(cond8, mod_w, mod_b.reshape(depth, 1, n))


def _rope128(z, cosf, sins):
    return z * cosf + pltpu.roll(z, 64, axis=1) * sins


def _ab_in_kernel(x_ref, c_ref, g_ref, mod_ref, w_ref, cos_ref, sin_ref, a_ref, l_ref, h_ref, *, n_lat_tiles):
    i = pl.program_id(0)
    j = pl.program_id(1)

    pl.when((j == 0) & (i < n_lat_tiles))(lambda: _norm_mod_into(h_ref, x_ref, g_ref, mod_ref, 0))
    pl.when((j == 0) & (i >= n_lat_tiles))(lambda: _norm_mod_into(h_ref, c_ref, g_ref, mod_ref, 0))

    def block(kinds):
        h = h_ref[...]
        for half, kind in enumerate(kinds):
            z = _dot(h, w_ref[:, half * SUB_N:(half + 1) * SUB_N])
            for k in range(SUB_N // LANES):
                sl = slice(half * SUB_N + k * LANES, half * SUB_N + (k + 1) * LANES)
                zk = z[:, k * LANES:(k + 1) * LANES]
                if kind == "q":
                    a_ref[:, sl] = (_rope128(zk, cos_ref[...], sin_ref[...]) * SWA_QSCALE).astype(BF16)
                elif kind == "k":
                    a_ref[:, sl] = _rope128(zk, cos_ref[...], sin_ref[...]).astype(BF16)
                elif kind == "v":
                    a_ref[:, sl] = zk.astype(BF16)
                else:
                    l_ref[:, sl] = zk

    pl.when(j <= 1)(lambda: block(("q", "q")))
    pl.when(j == 2)(lambda: block(("k", "v")))
    pl.when(j >= 3)(lambda: block(("f32", "f32")))


def _ab_in_proj(x2d, ctx2d, norm_g, mod_all, layer, w_in, cos_a, sin_a, mrow, trow, tm):
    d = x2d.shape[1]
    n_lat_tiles = x2d.shape[0] // tm
    nt = x2d.shape[0] + ctx2d.shape[0]
    tn = 512
    lat_map, ctx_map = _split_rows(n_lat_tiles)
    return pl.pallas_call(
        functools.partial(_ab_in_kernel, n_lat_tiles=n_lat_tiles),
        grid=(nt // tm, 5),
        in_specs=[
            pl.BlockSpec((tm, d), lat_map),
            pl.BlockSpec((tm, d), ctx_map, pipeline_mode=pl.Buffered(1)),
            pl.BlockSpec((1, d), lambda i, j: (0, 0)),
            pl.BlockSpec((1, 1, 6, d), lambda i, j: (layer, mrow(i), 0, 0)),
            pl.BlockSpec((d, tn), lambda i, j: (0, j)),
            pl.BlockSpec((tm, LANES), lambda i, j: (trow(i), 0)),
            pl.BlockSpec((tm, LANES), lambda i, j: (trow(i), 0)),
        ],
        out_specs=[
            pl.BlockSpec((tm, tn), lambda i, j: (i, jnp.minimum(j, 2))),
            pl.BlockSpec((tm, tn), lambda i, j: (i, jnp.maximum(j - 3, 0))),
        ],
        out_shape=[
            jax.ShapeDtypeStruct((nt, 3 * tn), BF16),
            jax.ShapeDtypeStruct((nt, 2 * tn), F32),
        ],
        scratch_shapes=[pltpu.VMEM((tm, d), BF16)],
        name="ab_in_proj",
        compiler_params=_cparams(("arbitrary", "arbitrary")),
    )(x2d, ctx2d, norm_g, mod_all, w_in, cos_a, sin_a)


def _rope64(r, cosp, sinp):
    lane = lax.broadcasted_iota(jnp.int32, r.shape, 1)
    partner = jnp.where(lane < 32, pltpu.roll(r, 96, axis=1), pltpu.roll(r, 32, axis=1))
    return r * cosp + partner * sinp


def _mla_proj_kernel(l_ref, qg_ref, kvg_ref, wq_ref, wkv_ref, cos_ref, sin_ref, qm_ref, kc_ref, vm_ref):
    cosp = cos_ref[...]
    sinp = sin_ref[...]

    def rms(x, g):
        return x * lax.rsqrt(jnp.mean(x * x, axis=-1, keepdims=True) + NORM_EPS) * g

    qn = rms(l_ref[:, :MLA_Q_RANK], qg_ref[...]).astype(BF16)
    qm = _dot(qn, wq_ref[...])
    kvn = rms(l_ref[:, MLA_Q_RANK:MLA_Q_RANK + MLA_KV_RANK], kvg_ref[...]).astype(BF16)
    kv = _dot(kvn, wkv_ref[...])
    kr = l_ref[:, MLA_Q_RANK + MLA_KV_RANK:MLA_Q_RANK + MLA_KV_RANK + LANES]
    krr = _rope64(kr, cosp, sinp).astype(BF16)
    ones = jnp.ones((l_ref.shape[0], LANES), BF16)
    for h in range(MLA_HEADS):
        o = h * MLA_HEAD_PAD
        qm_ref[:, o:o + LANES] = (qm[:, o:o + LANES] * MLA_QSCALE).astype(BF16)
        qm_ref[:, o + LANES:o + 2 * LANES] = (
            _rope64(qm[:, o + LANES:o + 2 * LANES], cosp, sinp) * MLA_QSCALE).astype(BF16)
        kc_ref[:, o:o + LANES] = kv[:, o:o + LANES].astype(BF16)
        kc_ref[:, o + LANES:o + 2 * LANES] = krr
        vm_ref[:, o:o + LANES] = kv[:, o + LANES:o + 2 * LANES].astype(BF16)
        vm_ref[:, o + LANES:o + 2 * LANES] = ones


def _mla_proj(lat, q_norm_g, kv_norm_g, wq, wkv, cos_b, sin_b, trow, tm):
    nt = lat.shape[0]
    hp = MLA_HEADS * MLA_HEAD_PAD
    return pl.pallas_call(
        _mla_proj_kernel,
        grid=(nt // tm,),
        in_specs=[
            pl.BlockSpec((tm, lat.shape[1]), lambda i: (i, 0)),
            pl.BlockSpec((1, MLA_Q_RANK), lambda i: (0, 0)),
            pl.BlockSpec((1, MLA_KV_RANK), lambda i: (0, 0)),
            pl.BlockSpec(wq.shape, lambda i: (0, 0)),
            pl.BlockSpec(wkv.shape, lambda i: (0, 0)),
            pl.BlockSpec((tm, LANES), lambda i: (trow(i), 0)),
            pl.BlockSpec((tm, LANES), lambda i: (trow(i), 0)),
        ],
        out_specs=[
            pl.BlockSpec((tm, hp), lambda i: (i, 0)),
            pl.BlockSpec((tm, hp), lambda i: (i, 0)),
            pl.BlockSpec((tm, hp), lambda i: (i, 0)),
        ],
        out_shape=[
            jax.ShapeDtypeStruct((nt, hp), BF16),
            jax.ShapeDtypeStruct((nt, hp), BF16),
            jax.ShapeDtypeStruct((nt, hp), BF16),
        ],
        name="mla_proj",
        compiler_params=_cparams(("arbitrary",)),
    )(lat, q_norm_g, kv_norm_g, wq, wkv, cos_b, sin_b)


def _swa_softmax_pv(s, sk, v_ext):
    d = SWA_HEAD_DIM
    m = jnp.maximum(jnp.max(s, axis=-1, keepdims=True), sk)
    o_ext = _dot(jnp.exp2(s - m).astype(BF16), v_ext)
    return o_ext[:, :d] / (o_ext[:, d:] + jnp.exp2(sk - m))


def _swa_ctx_kernel(sink_ref, q_ref, kc, vc, o_ref):
    kvh = pl.program_id(1)
    d = SWA_HEAD_DIM
    g_heads = SWA_HEADS // SWA_KV_HEADS
    k_all = kc[...]
    v_ext = jnp.concatenate([vc[...], jnp.ones_like(vc)], axis=1)
    for g in range(g_heads):
        sk = sink_ref[kvh * g_heads + g] * LOG2_E
        s = _dot_nt(q_ref[:, g * d:(g + 1) * d], k_all)
        o_ref[:, g * d:(g + 1) * d] = _swa_softmax_pv(s, sk, v_ext).astype(BF16)


def _swa_kernel(sink_ref, q_ref, kp, km, kn, kc, vp, vm, vn, vc, o_ref, *, tq, s_len):
    i = pl.program_id(1)
    d = SWA_HEAD_DIM
    g_heads = SWA_HEADS // SWA_KV_HEADS
    nb = tq + 2 * SWA_WINDOW
    shape = (tq, nb + kc.shape[0])
    row = lax.broadcasted_iota(jnp.int32, shape, 0)
    col = lax.broadcasted_iota(jnp.int32, shape, 1)
    kpos = i * tq - SWA_WINDOW + col
    valid = (col >= nb) | ((jnp.abs(col - SWA_WINDOW - row) <= SWA_WINDOW) & (kpos >= 0) & (kpos < s_len))
    for kvh in range(SWA_KV_HEADS):
        kv = slice(kvh * d, (kvh + 1) * d)
        k_all = jnp.concatenate([kp[:, kv], km[:, kv], kn[:, kv], kc[:, kv]], axis=0)
        v_all = jnp.concatenate([vp[:, kv], vm[:, kv], vn[:, kv], vc[:, kv]], axis=0)
        v_ext = jnp.concatenate([v_all, jnp.ones_like(v_all)], axis=1)
        for g in range(g_heads):
            head = kvh * g_heads + g
            sk = sink_ref[head] * LOG2_E
            s = jnp.where(valid, _dot_nt(q_ref[:, head * d:(head + 1) * d], k_all), NEG_INF)
            o_ref[:, head * d:(head + 1) * d] = _swa_softmax_pv(s, sk, v_ext).astype(BF16)


def _swa_attention(a, sink, b_sz, cn, s_len, tq):
    d = SWA_HEAD_DIM
    w = SWA_WINDOW
    gw = (SWA_HEADS // SWA_KV_HEADS) * d
    kcol = SWA_HEADS
    vcol = SWA_HEADS + SWA_KV_HEADS
    nq = s_len // tq
    r = tq // w
    c0 = (b_sz * s_len) // cn
    smem = pl.BlockSpec(memory_space=pltpu.SMEM)

    def main(b, i):
        return b * nq + i

    def prev(b, i):
        return b * (s_len // w) + jnp.maximum(i * r - 1, 0)

    def nxt(b, i):
        return b * (s_len // w) + jnp.minimum((i + 1) * r, s_len // w - 1)

    qw = SWA_HEADS * d
    kw = SWA_KV_HEADS * d
    kb = qw // kw
    o_lat = pl.pallas_call(
        functools.partial(_swa_kernel, tq=tq, s_len=s_len),
        grid=(b_sz, nq),
        in_specs=[
            smem,
            pl.BlockSpec((tq, qw), lambda b, i: (main(b, i), 0)),
            pl.BlockSpec((w, kw), lambda b, i: (prev(b, i), kb)),
            pl.BlockSpec((tq, kw), lambda b, i: (main(b, i), kb)),
            pl.BlockSpec((w, kw), lambda b, i: (nxt(b, i), kb)),
            pl.BlockSpec((cn, kw), lambda b, i: (c0 + b, kb)),
            pl.BlockSpec((w, kw), lambda b, i: (prev(b, i), kb + 1)),
            pl.BlockSpec((tq, kw), lambda b, i: (main(b, i), kb + 1)),
            pl.BlockSpec((w, kw), lambda b, i: (nxt(b, i), kb + 1)),
            pl.BlockSpec((cn, kw), lambda b, i: (c0 + b, kb + 1)),
        ],
        out_specs=pl.BlockSpec((tq, qw), lambda b, i: (main(b, i), 0)),
        out_shape=jax.ShapeDtypeStruct((b_sz * s_len, qw), BF16),
        name="swa_latent",
        compiler_params=_cparams(("arbitrary", "arbitrary")),
    )(sink, a, a, a, a, a, a, a, a, a)

    o_ctx = pl.pallas_call(
        _swa_ctx_kernel,
        grid=(b_sz, SWA_KV_HEADS),
        in_specs=[
            smem,
            pl.BlockSpec((cn, gw), lambda b, h: (c0 + b, h)),
            pl.BlockSpec((cn, d), lambda b, h: (c0 + b, kcol + h)),
            pl.BlockSpec((cn, d), lambda b, h: (c0 + b, vcol + h)),
        ],
        out_specs=pl.BlockSpec((cn, gw), lambda b, h: (b, h)),
        out_shape=jax.ShapeDtypeStruct((b_sz * cn, SWA_HEADS * d), BF16),
        name="swa_context",
        compiler_params=_cparams(("arbitrary", "arbitrary")),
    )(sink, a, a, a)
    return o_lat, o_ctx


def _mla_kernel(q_ref, kc_ref, vc_ref, *refs, tk, latent, n_cast=0):
    if latent:
        kl_ref, vl_ref = refs[:2]
        cast_in = refs[2:2 + n_cast]
        o_ref = refs[2 + n_cast]
        cast_out = refs[3 + n_cast:]
        for src, dst in zip(cast_in, cast_out):
            dst[...] = src[...].astype(BF16)
    else:
        (o_ref,) = refs
    dv = MLA_V_DIM
    hp = MLA_HEAD_PAD
    n_heads = q_ref.shape[1] // hp
    chunks = []
    if latent:
        chunks += [(kl_ref, vl_ref, slice(j * tk, (j + 1) * tk)) for j in range(kl_ref.shape[0] // tk)]
    chunks.append((kc_ref, vc_ref, slice(None)))
    m = [None] * n_heads
    acc = [None] * n_heads
    for k_ref, v_ref, sl in chunks:
        for hh in range(n_heads):
            cols = slice(hh * hp, (hh + 1) * hp)
            s = _dot_nt(q_ref[:, cols], k_ref[sl, cols])
            if m[hh] is None:
                m[hh] = jnp.max(s, axis=-1, keepdims=True)
                acc[hh] = _dot(jnp.exp2(s - m[hh]).astype(BF16), v_ref[sl, cols])
            else:
                m_new = jnp.maximum(m[hh], jnp.max(s, axis=-1, keepdims=True))
                acc[hh] = (jnp.exp2(m[hh] - m_new) * acc[hh]
                           + _dot(jnp.exp2(s - m_new).astype(BF16), v_ref[sl, cols]))
                m[hh] = m_new
    for hh in range(n_heads):
        o_ref[:, hh * dv:(hh + 1) * dv] = (acc[hh][:, :dv] / acc[hh][:, dv:]).astype(BF16)


BF16_SUBLANES = 16


def _cast_block(rows, n_steps):
    share = 1
    while (rows * share) % n_steps or (rows * share // n_steps) % BF16_SUBLANES:
        share *= 2
        assert share <= n_steps
    return rows * share // n_steps, share


def _mla_attention(qm, kcat, vext, cast_weights, b_sz, cn, s_len, tq, tk):
    hp = MLA_HEAD_PAD * MLA_HEADS_PER_STEP
    dv = MLA_V_DIM * MLA_HEADS_PER_STEP
    n_hg = MLA_HEADS // MLA_HEADS_PER_STEP
    nq = s_len // tq
    c0 = (b_sz * s_len) // cn
    n_steps = b_sz * n_hg * nq

    cast_specs = []
    for w in cast_weights:
        rows, share = _cast_block(w.shape[0], n_steps)
        cast_specs.append(pl.BlockSpec(
            (rows, w.shape[1]), lambda b, h, i, share=share: (((b * n_hg + h) * nq + i) // share, 0)))

    o_lat, *w_bf16 = pl.pallas_call(
        functools.partial(_mla_kernel, tk=tk, latent=True, n_cast=len(cast_weights)),
        grid=(b_sz, n_hg, nq),
        in_specs=[
            pl.BlockSpec((tq, hp), lambda b, h, i: (b * nq + i, h)),
            pl.BlockSpec((cn, hp), lambda b, h, i: (c0 + b, h)),
            pl.BlockSpec((cn, hp), lambda b, h, i: (c0 + b, h)),
            pl.BlockSpec((s_len, hp), lambda b, h, i: (b, h)),
            pl.BlockSpec((s_len, hp), lambda b, h, i: (b, h)),
        ] + cast_specs,
        out_specs=[pl.BlockSpec((tq, dv), lambda b, h, i: (b * nq + i, h))] + cast_specs,
        out_shape=[jax.ShapeDtypeStruct((b_sz * s_len, n_hg * dv), BF16)]
        + [jax.ShapeDtypeStruct(w.shape, BF16) for w in cast_weights],
        name="mla_latent",
        compiler_params=_cparams(("arbitrary",) * 3),
    )(qm, kcat, vext, kcat, vext, *cast_weights)

    o_ctx = pl.pallas_call(
        functools.partial(_mla_kernel, tk=0, latent=False),
        grid=(b_sz, n_hg),
        in_specs=[
            pl.BlockSpec((cn, hp), lambda b, h: (c0 + b, h)),
            pl.BlockSpec((cn, hp), lambda b, h: (c0 + b, h)),
            pl.BlockSpec((cn, hp), lambda b, h: (c0 + b, h)),
        ],
        out_specs=pl.BlockSpec((cn, dv), lambda b, h: (b, h)),
        out_shape=jax.ShapeDtypeStruct((b_sz * cn, n_hg * dv), BF16),
        name="mla_context",
        compiler_params=_cparams(("arbitrary", "arbitrary")),
    )(qm, kcat, vext)
    return o_lat, o_ctx, w_bf16


def _out_proj_kernel(w_ref, mod_ref, *refs, n_parts, n_lat_tiles):
    o_ref = refs[-1]
    lat = refs[:n_parts + 1]
    ctx = refs[n_parts + 1:-1]

    def compute(side):
        parts = [p[...] for p in side[:-1]]
        res_ref = side[-1]
        for sub in range(o_ref.shape[1] // SUB_N):
            cols = slice(sub * SUB_N, (sub + 1) * SUB_N)
            acc = None
            k0 = 0
            for p in parts:
                kw = p.shape[1]
                t = _dot(p, w_ref[k0:k0 + kw, cols])
                acc = t if acc is None else acc + t
                k0 += kw
            o_ref[:, cols] = res_ref[:, cols] + mod_ref[0, 0, 2:3, cols] * acc

    if n_lat_tiles is None:
        compute(lat)
    else:
        i = pl.program_id(0)
        pl.when(i < n_lat_tiles)(lambda: compute(lat))
        pl.when(i >= n_lat_tiles)(lambda: compute(ctx))


def _out_proj(lat_parts, lat_res, ctx_parts, ctx_res, w, mod_all, layer, mrow, tm):
    lat_rows = lat_parts[0].shape[0]
    d = lat_res.shape[1]
    n_lat = lat_rows // tm
    if ctx_parts is None:
        n_lat_tiles = None
        rows = lat_rows
        specs = [pl.BlockSpec((tm, p.shape[1]), lambda i: (i, 0)) for p in lat_parts]
        specs.append(pl.BlockSpec((tm, d), lambda i: (i, 0)))
        operands = [*lat_parts, lat_res]
    else:
        n_lat_tiles = n_lat
        rows = lat_rows + ctx_parts[0].shape[0]
        lat_map = lambda i: (jnp.minimum(i, n_lat - 1), 0)
        ctx_map = lambda i: (jnp.maximum(i - n_lat, 0), 0)
        specs = [pl.BlockSpec((tm, p.shape[1]), lat_map) for p in lat_parts]
        specs.append(pl.BlockSpec((tm, d), lat_map))
        specs += [pl.BlockSpec((tm, p.shape[1]), ctx_map) for p in ctx_parts]
        specs.append(pl.BlockSpec((tm, d), ctx_map))
        operands = [*lat_parts, lat_res, *ctx_parts, ctx_res]
    return pl.pallas_call(
        functools.partial(_out_proj_kernel, n_parts=len(lat_parts), n_lat_tiles=n_lat_tiles),
        grid=(rows // tm,),
        in_specs=[
            pl.BlockSpec(w.shape, lambda i: (0, 0), pipeline_mode=pl.Buffered(1)),
            pl.BlockSpec((1, 1, 6, d), lambda i: (layer, mrow(i), 0, 0)),
        ] + specs,
        out_specs=pl.BlockSpec((tm, d), lambda i: (i, 0)),
        out_shape=jax.ShapeDtypeStruct((rows, d), F32),
        name="out_proj",
        compiler_params=_cparams(("arbitrary",)),
    )(w, mod_all, *operands)


def _ffn_kernel(x_ref, g_ref, mod_ref, wg_ref, wu_ref, wd_ref, fg_ref, o_ref, h_ref, *, nf, final):
    j = pl.program_id(1)

    @pl.when(j == 0)
    def _():
        _norm_mod_into(h_ref, x_ref, g_ref, mod_ref, 3)
        o_ref[...] = jnp.zeros_like(o_ref)

    h = h_ref[...]
    a = (_silu(_dot(h, wg_ref[0].astype(BF16))) * _dot(h, wu_ref[0].astype(BF16))).astype(BF16)
    o_ref[...] += _dot(a, wd_ref[0].astype(BF16))

    @pl.when(j == nf - 1)
    def _():
        gate = mod_ref[0, 0, 5:6, :]

        def body(t, carry):
            sl = pl.ds(pl.multiple_of(t * NORM_ROWS, NORM_ROWS), NORM_ROWS)
            y = x_ref[sl, :] + gate * o_ref[sl, :]
            if final:
                y = y * lax.rsqrt(jnp.mean(y * y, axis=-1, keepdims=True) + NORM_EPS) * fg_ref[...]
            o_ref[sl, :] = y
            return carry

        lax.fori_loop(0, x_ref.shape[0] // NORM_ROWS, body, 0, unroll=4)


def _ffn(xt, norm_g, mod_all, layer, wg, wu, wd, final_g, mrow, n_rows, tm, tf, final):
    d = xt.shape[1]
    f = wg.shape[2]
    nf = f // tf
    return pl.pallas_call(
        functools.partial(_ffn_kernel, nf=nf, final=final),
        grid=(n_rows // tm, nf),
        in_specs=[
            pl.BlockSpec((tm, d), lambda i, j: (i, 0)),
            pl.BlockSpec((1, d), lambda i, j: (0, 0)),
            pl.BlockSpec((1, 1, 6, d), lambda i, j: (layer, mrow(i), 0, 0)),
            pl.BlockSpec((1, d, tf), lambda i, j: (layer, 0, j)),
            pl.BlockSpec((1, d, tf), lambda i, j: (layer, 0, j)),
            pl.BlockSpec((1, tf, d), lambda i, j: (layer, j, 0)),
            pl.BlockSpec((1, d), lambda i, j: (0, 0)),
        ],
        out_specs=pl.BlockSpec((tm, d), lambda i, j: (i, 0)),
        out_shape=jax.ShapeDtypeStruct((n_rows, d), F32),
        scratch_shapes=[pltpu.VMEM((tm, d), BF16)],
        name="ffn_final" if final else "ffn",
        compiler_params=_cparams(("arbitrary", "arbitrary")),
    )(xt, norm_g, mod_all, wg, wu, wd, final_g)


def _ret_in_kernel(x_ref, g_ref, mod_ref, w_ref, cos_ref, sin_ref, o_ref, h_ref, *, n_qk, k_scale, tn):
    j = pl.program_id(1)

    pl.when(j == 0)(lambda: _norm_mod_into(h_ref, x_ref, g_ref, mod_ref, 0))

    def block(scale):
        h = h_ref[...]
        for sub in range(tn // SUB_N):
            o = sub * SUB_N
            z = _dot(h, w_ref[:, o:o + SUB_N])
            if scale is None:
                o_ref[:, o:o + SUB_N] = z.astype(BF16)
            elif scale == "silu":
                o_ref[:, o:o + SUB_N] = _silu(z).astype(BF16)
            else:
                c = cos_ref[...]
                s = sin_ref[...]
                x1 = z[:, :LANES] * scale if scale != 1.0 else z[:, :LANES]
                x2 = z[:, LANES:] * scale if scale != 1.0 else z[:, LANES:]
                o_ref[:, o:o + LANES] = (x1 * c - x2 * s).astype(BF16)
                o_ref[:, o + LANES:o + SUB_N] = (x1 * s + x2 * c).astype(BF16)

    pl.when(j < n_qk)(lambda: block(1.0))
    pl.when((j >= n_qk) & (j < 2 * n_qk))(lambda: block(k_scale))
    pl.when((j >= 2 * n_qk) & (j < 4 * n_qk))(lambda: block(None))
    pl.when(j >= 4 * n_qk)(lambda: block("silu"))


def _ret_in_proj(xt, norm_g, mod_all, layer, w_in, cos_r, sin_r, mrow, trow, tm, tn):
    nt, d = xt.shape
    n = w_in.shape[1]
    dk = d // RET_HEADS
    return pl.pallas_call(
        functools.partial(_ret_in_kernel, n_qk=d // tn, k_scale=dk ** -0.5, tn=tn),
        grid=(nt // tm, n // tn),
        in_specs=[
            pl.BlockSpec((tm, d), lambda i, j: (i, 0)),
            pl.BlockSpec((1, d), lambda i, j: (0, 0)),
            pl.BlockSpec((1, 1, 6, d), lambda i, j: (layer, mrow(i), 0, 0)),
            pl.BlockSpec((d, tn), lambda i, j: (0, j)),
            pl.BlockSpec((tm, LANES), lambda i, j: (trow(i), 0)),
            pl.BlockSpec((tm, LANES), lambda i, j: (trow(i), 0)),
        ],
        out_specs=pl.BlockSpec((tm, tn), lambda i, j: (i, j)),
        out_shape=jax.ShapeDtypeStruct((nt, n), BF16),
        scratch_shapes=[pltpu.VMEM((tm, d), BF16)],
        name="ret_in_proj",
        compiler_params=_cparams(("arbitrary", "arbitrary")),
    )(xt, norm_g, mod_all, w_in, cos_r, sin_r)


def _log_sigmoid(x):
    return jnp.minimum(x, 0.0) - jnp.log1p(jnp.exp(-jnp.abs(x)))


def _ret_kernel(lf_ref, lb_ref, gn_ref, qc_ref, kc_ref, vc_ref, q_ref, k_ref, v_ref, g_ref, y_ref,
                o_ref, s_ref, dint_ref, qdec_ref, kdec_ref, *, n_ctx, n_lat):
    c = RET_CHUNK
    dk = q_ref.shape[1]
    dv = v_ref.shape[1]
    row = lax.broadcasted_iota(jnp.int32, (c, c), 0)
    col = lax.broadcasted_iota(jnp.int32, (c, c), 1)
    pos = lax.broadcasted_iota(jnp.int32, (c, LANES), 0).astype(F32)
    c_dec = []
    for drn, l_ref in enumerate((lf_ref, lb_ref)):
        lg = _log_sigmoid(l_ref[0])
        lg_c = jnp.broadcast_to(lg[:, 0:1], (1, c))
        diff = (col - row) if drn else (row - col)
        dint_ref[drn] = jnp.where(diff >= 0, jnp.exp(lg_c * jnp.maximum(diff, 0).astype(F32)), 0.0)
        if drn:
            qdec_ref[drn] = jnp.exp(lg * (c - pos))
            kdec_ref[drn] = jnp.exp(lg * pos)
        else:
            qdec_ref[drn] = jnp.exp(lg * (pos + 1.0))
            kdec_ref[drn] = jnp.exp(lg * (c - 1.0 - pos))
        c_dec.append(jnp.exp(jnp.broadcast_to(lg[:, 0:1], (1, dv)) * c))

    def scale_rows(x, dec):
        return jnp.concatenate([x[:, k * LANES:(k + 1) * LANES].astype(F32) * dec for k in range(dk // LANES)],
                               axis=1).astype(BF16)

    def step(drn, q, k, v, want_out):
        s = s_ref[drn]
        o = None
        if want_out:
            a = (_dot_nt(q, k) * dint_ref[drn]).astype(BF16)
            o = _dot(a, v) + _dot(scale_rows(q, qdec_ref[drn]), s.astype(BF16))
        s_ref[drn] = s * c_dec[drn] + _dot_tn(scale_rows(k, kdec_ref[drn]), v)
        return o

    def rows(t):
        return pl.ds(pl.multiple_of(t * c, c), c)

    def lat_step(drn, t):
        sl = rows(t)
        return step(drn, q_ref[sl, :], k_ref[sl, :], v_ref[sl, :], True)

    def finish(t, o):
        sl = rows(t)
        mu = jnp.mean(o, axis=-1, keepdims=True)
        oc = o - mu
        var = jnp.mean(oc * oc, axis=-1, keepdims=True)
        yn = oc * lax.rsqrt(var + NORM_EPS) * gn_ref[...]
        y_ref[sl, :] = (g_ref[sl, :].astype(F32) * yn).astype(BF16)

    s_ref[...] = jnp.zeros_like(s_ref)
    for t in range(n_ctx):
        for drn, tt in ((0, t), (1, n_ctx - 1 - t)):
            sl = slice(tt * c, (tt + 1) * c)
            step(drn, qc_ref[sl, :], kc_ref[sl, :], vc_ref[sl, :], False)

    half = n_lat // 2

    def first_half(t, carry):
        tb = n_lat - 1 - t
        o_ref[rows(t), :] = lat_step(0, t)
        o_ref[rows(tb), :] = lat_step(1, tb)
        return carry

    def second_half(t, carry):
        tb = n_lat - 1 - t
        finish(t, o_ref[rows(t), :] + lat_step(0, t))
        finish(tb, o_ref[rows(tb), :] + lat_step(1, tb))
        return carry

    lax.fori_loop(0, half, first_half, 0, unroll=4)
    lax.fori_loop(half, n_lat, second_half, 0, unroll=4)


def _retention(r, lg_f, lg_b, gn_g, b_sz, cn, s_len, d):
    h_n = RET_HEADS
    dk = d // h_n
    dv = 2 * dk
    c = RET_CHUNK
    assert cn % c == 0 and s_len % (2 * c) == 0
    c0 = (b_sz * s_len) // cn
    kq, kk, kv, kg = 0, d // dk, (2 * d) // dv, (4 * d) // dv

    def lg_spec():
        return pl.BlockSpec((1, 1, LANES), lambda b, h: (h, 0, 0))

    return pl.pallas_call(
        functools.partial(_ret_kernel, n_ctx=cn // c, n_lat=s_len // c),
        grid=(b_sz, h_n),
        in_specs=[
            lg_spec(), lg_spec(),
            pl.BlockSpec((1, dv), lambda b, h: (0, h)),
            pl.BlockSpec((cn, dk), lambda b, h: (c0 + b, kq + h)),
            pl.BlockSpec((cn, dk), lambda b, h: (c0 + b, kk + h)),
            pl.BlockSpec((cn, dv), lambda b, h: (c0 + b, kv + h)),
            pl.BlockSpec((s_len, dk), lambda b, h: (b, kq + h)),
            pl.BlockSpec((s_len, dk), lambda b, h: (b, kk + h)),
            pl.BlockSpec((s_len, dv), lambda b, h: (b, kv + h)),
            pl.BlockSpec((s_len, dv), lambda b, h: (b, kg + h)),
        ],
        out_specs=pl.BlockSpec((s_len, dv), lambda b, h: (b, h)),
        out_shape=jax.ShapeDtypeStruct((b_sz * s_len, h_n * dv), BF16),
        scratch_shapes=[
            pltpu.VMEM((s_len, dv), F32),
            pltpu.VMEM((2, dk, dv), F32),
            pltpu.VMEM((2, c, c), F32),
            pltpu.VMEM((2, c, LANES), F32),
            pltpu.VMEM((2, c, LANES), F32),
        ],
        name="retention",
        compiler_params=_cparams(("arbitrary", "arbitrary")),
    )(lg_f, lg_b, gn_g, r, r, r, r, r, r, r)


def _rope_angles(s_len, rot_dim):
    rows = s_len // GRID_W
    row = np.repeat(np.arange(rows, dtype=np.float32), GRID_W)
    col = (np.arange(s_len) % GRID_W).astype(np.float32)
    n_freq = rot_dim // 4
    inv = (np.float32(ROPE_BASE) ** (-np.arange(n_freq, dtype=np.float32) / np.float32(n_freq))).astype(np.float32)
    ang = np.concatenate([row[:, None] * inv, col[:, None] * inv], axis=-1).astype(np.float32)
    return np.cos(ang).astype(np.float32), np.sin(ang).astype(np.float32)


def _position_table(lat_table, ctx_row, pad_rows):
    width = lat_table.shape[1]
    return jnp.asarray(np.concatenate([lat_table, np.broadcast_to(ctx_row[None, :], (pad_rows, width))],
                                      axis=0).astype(np.float32))


def kernel(x, c, ctx, c_ctx, mod_w, mod_b, norm_mix_g, norm_ffn_g, ffn_w_gate, ffn_w_up, ffn_w_down, ab_w_in, ab_w_out, swa_sink, mla_q_norm_g, mla_w_q_b, mla_kv_norm_g, mla_w_kv_b, ret_w_in, ret_decay_logit_fwd, ret_decay_logit_bwd, ret_gn_g, ret_w_out, final_norm_g):
    b_sz, s_len, d = x.shape
    cn = ctx.shape[1]
    depth = mod_w.shape[0]
    assert depth == 2 and ab_w_in.shape[0] == 1 and ret_w_in.shape[0] == 1
    assert b_sz + 1 <= 8
    n_lat_rows = b_sz * s_len
    n_ctx_rows = b_sz * cn
    nt = n_lat_rows + n_ctx_rows

    tm = min(1024, n_ctx_rows, s_len)
    tq = min(512, s_len)
    tk = min(1024, s_len)
    assert n_ctx_rows % tm == 0 and s_len % tm == 0 and s_len % cn == 0 and s_len % tq == 0 and s_len % tk == 0

    def make_mrow(t):
        n_lat_tiles = n_lat_rows // t
        per_b = s_len // t
        return lambda i: jnp.where(i < n_lat_tiles, 1 + i // per_b, 0)

    mrow = make_mrow(tm)
    tm_o = min(512, tm)
    mrow_o = make_mrow(tm_o)

    x2d = x.reshape(n_lat_rows, d)
    ctx2d = ctx.reshape(n_ctx_rows, d)

    cond8 = jnp.zeros((8, d), F32).at[0].set(c_ctx).at[1:1 + b_sz].set(c)
    mod_all = _modulation(cond8, mod_w, mod_b).reshape(depth, 8, 6, d)

    cos_a, sin_a = _rope_angles(s_len, SWA_HEAD_DIM)
    cos_b, sin_b = _rope_angles(s_len, MLA_ROPE_DIM)
    cos_r, sin_r = _rope_angles(s_len, d // RET_HEADS)
    ones = np.ones((LANES,), np.float32)
    zeros = np.zeros((LANES,), np.float32)
    half = np.concatenate([np.ones((64,), np.float32), np.zeros((64,), np.float32)])
    z64 = np.zeros((s_len, 64), np.float32)
    table = functools.partial(_position_table, pad_rows=tm)
    per_b = s_len // tm
    trow = lambda i: jnp.where(i < n_lat_rows // tm, i % per_b, per_b)
    t_cos_a = table(np.concatenate([cos_a, cos_a], axis=1), ones)
    t_sin_a = table(np.concatenate([-sin_a, sin_a], axis=1), zeros)
    t_cos_b = table(np.concatenate([cos_b, cos_b, z64], axis=1), half)
    t_sin_b = table(np.concatenate([-sin_b, sin_b, z64], axis=1), zeros)
    t_cos_r = table(cos_r, ones)
    t_sin_r = table(sin_r, zeros)

    bf = lambda w: w.astype(BF16)
    w_ab_in = jnp.pad(ab_w_in[0], ((0, 0), (0, 2560 - ab_w_in.shape[2]))).astype(BF16)
    wq_b = jnp.pad(mla_w_q_b[0].reshape(MLA_Q_RANK, MLA_HEADS, MLA_NOPE_DIM + MLA_ROPE_DIM),
                   ((0, 0), (0, 0), (0, MLA_HEAD_PAD - MLA_NOPE_DIM - MLA_ROPE_DIM))
                   ).reshape(MLA_Q_RANK, MLA_HEADS * MLA_HEAD_PAD).astype(BF16)
    wkv_b = bf(mla_w_kv_b[0])
    tf = 512
    final_g = final_norm_g[None, :]
    f_hidden = ffn_w_gate.shape[2]
    later_weights = [ffn_w_gate.reshape(depth * d, f_hidden), ffn_w_up.reshape(depth * d, f_hidden),
                     ffn_w_down.reshape(depth * f_hidden, d), ab_w_out[0], ret_w_in[0], ret_w_out[0]]

    a_qkv, lat = _ab_in_proj(x2d, ctx2d, norm_mix_g[0:1], mod_all, 0, w_ab_in, t_cos_a, t_sin_a, mrow, trow, tm)
    qm, kcat, vext = _mla_proj(lat, mla_q_norm_g[0:1], mla_kv_norm_g[0:1], wq_b, wkv_b, t_cos_b, t_sin_b, trow, tm)
    oa, oa_c = _swa_attention(a_qkv, swa_sink[0], b_sz, cn, s_len, tq)
    ob, ob_c, (wg, wu, wd, w_ab_out, w_ret_in, w_ret_out) = _mla_attention(
        qm, kcat, vext, later_weights, b_sz, cn, s_len, min(1024, s_len), tk)
    wg = wg.reshape(depth, d, f_hidden)
    wu = wu.reshape(depth, d, f_hidden)
    wd = wd.reshape(depth, f_hidden, d)
    xt = _out_proj([oa, ob], x2d, [oa_c, ob_c], ctx2d, w_ab_out, mod_all, 0, mrow_o, tm_o)
    xt = _ffn(xt, norm_ffn_g[0:1], mod_all, 0, wg, wu, wd, final_g, mrow, nt, tm, tf, final=False)

    r = _ret_in_proj(xt, norm_mix_g[1:2], mod_all, 1, w_ret_in, t_cos_r, t_sin_r, mrow, trow, tm, 2048)
    lg_shape = (RET_HEADS, 1, LANES)
    lg_f = jnp.broadcast_to(ret_decay_logit_fwd[0].astype(F32)[:, None, None], lg_shape)
    lg_b = jnp.broadcast_to(ret_decay_logit_bwd[0].astype(F32)[:, None, None], lg_shape)
    y = _retention(r, lg_f, lg_b, ret_gn_g[0:1], b_sz, cn, s_len, d)
    xl = _out_proj([y], xt, None, None, w_ret_out, mod_all, 1, mrow_o, tm_o)
    out = _ffn(xl, norm_ffn_g[1:2], mod_all, 1, wg, wu, wd, final_g, mrow, n_lat_rows, tm, tf, final=True)
    return out.reshape(b_sz, s_len, d)
```

```python
import functools

import jax
import jax.numpy as jnp
import numpy as np
from jax import lax
from jax.experimental import pallas as pl
from jax.experimental.pallas import tpu as pltpu

GRID_W = 64
ROPE_BASE = 10000.0
NORM_EPS = 1e-6
NEG_INF = -1e30
LOG2_E = 1.4426950408889634

SWA_HEADS = 8
SWA_KV_HEADS = 2
SWA_HEAD_DIM = 128
SWA_WINDOW = 128
SWA_QSCALE = SWA_HEAD_DIM ** -0.5 * LOG2_E

MLA_HEADS = 8
MLA_Q_RANK = 512
MLA_KV_RANK = 256
MLA_NOPE_DIM = 128
MLA_ROPE_DIM = 64
MLA_V_DIM = 128
MLA_HEAD_PAD = 256
MLA_QSCALE = (MLA_NOPE_DIM + MLA_ROPE_DIM) ** -0.5 * LOG2_E
MLA_HEADS_PER_STEP = 2

RET_HEADS = 8
RET_CHUNK = 256

LANES = 128
V7X_VMEM_BYTES = 64 * 1024 * 1024
VMEM_LIMIT = V7X_VMEM_BYTES - 8 * 1024 * 1024

F32 = jnp.float32
BF16 = jnp.bfloat16


def _cparams(sem):
    return pltpu.CompilerParams(dimension_semantics=sem, vmem_limit_bytes=VMEM_LIMIT)


def _dot(a, b):
    return jnp.dot(a, b, preferred_element_type=F32)


def _dot_nt(a, b):
    return lax.dot_general(a, b, (((1,), (1,)), ((), ())), preferred_element_type=F32)


def _dot_tn(a, b):
    return lax.dot_general(a, b, (((0,), (0,)), ((), ())), preferred_element_type=F32)


def _silu(x):
    h = 0.5 * x
    return h + h * jnp.tanh(h)


NORM_ROWS = 32
SUB_N = 256
X_CHUNKS = 4


def _norm_mod_into(h_ref, x_ref, g_ref, mod_ref, shift_row):
    shift = mod_ref[0, 0, shift_row:shift_row + 1, :]
    gain = g_ref[...] * (1.0 + mod_ref[0, 0, shift_row + 1:shift_row + 2, :])

    x_refs = x_ref if isinstance(x_ref, (tuple, list)) else (x_ref,)

    def body(t, carry):
        sl = pl.ds(pl.multiple_of(t * NORM_ROWS, NORM_ROWS), NORM_ROWS)
        x = jnp.concatenate([r[sl, :] for r in x_refs], axis=1) if len(x_refs) > 1 else x_refs[0][sl, :]
        inv = lax.rsqrt(jnp.mean(x * x, axis=-1, keepdims=True) + NORM_EPS)
        h_ref[sl, :] = ((x * inv) * gain + shift).astype(BF16)
        return carry

    lax.fori_loop(0, x_refs[0].shape[0] // NORM_ROWS, body, 0, unroll=4)


def _split_rows(n_lat_tiles):
    lat = lambda i, j: (jnp.minimum(i, n_lat_tiles - 1), 0)
    ctx = lambda i, j: (jnp.maximum(i - n_lat_tiles, 0), 0)
    return lat, ctx


def _mod_kernel(c_ref, w_ref, b_ref, o_ref):
    a = _silu(c_ref[...]).astype(BF16)
    o_ref[0] = _dot(a, w_ref[0].astype(BF16)) + b_ref[0]


def _modulation(cond8, mod_w, mod_b):
    depth, d, n = mod_w.shape
    tn = 1024
    return pl.pallas_call(
        _mod_kernel,
        grid=(depth, n // tn),
        in_specs=[
            pl.BlockSpec((8, d), lambda l, j: (0, 0)),
            pl.BlockSpec((1, d, tn), lambda l, j: (l, 0, j)),
            pl.BlockSpec((1, 1, tn), lambda l, j: (l, 0, j)),
        ],
        out_specs=pl.BlockSpec((1, 8, tn), lambda l, j: (l, 0, j)),
        out_shape=jax.ShapeDtypeStruct((depth, 8, n), F32),
        name="modulation",
        compiler_params=_cparams(("arbitrary", "arbitrary")),
    )(cond8, mod_w, mod_b.reshape(depth, 1, n))


def _rope128(z, cosf, sins):
    return z * cosf + pltpu.roll(z, 64, axis=1) * sins


def _ab_in_kernel(*refs, n_lat_tiles):
    x_refs = refs[:X_CHUNKS]
    c_ref, g_ref, mod_ref, w_ref, cos_ref, sin_ref, a_ref, l_ref, h_ref = refs[X_CHUNKS:]
    i = pl.program_id(0)
    j = pl.program_id(1)

    pl.when((j == 0) & (i < n_lat_tiles))(lambda: _norm_mod_into(h_ref, x_refs, g_ref, mod_ref, 0))
    pl.when((j == 0) & (i >= n_lat_tiles))(lambda: _norm_mod_into(h_ref, c_ref, g_ref, mod_ref, 0))

    def block(kinds):
        h = h_ref[...]
        for half, kind in enumerate(kinds):
            z = _dot(h, w_ref[:, half * SUB_N:(half + 1) * SUB_N])
            for k in range(SUB_N // LANES):
                sl = slice(half * SUB_N + k * LANES, half * SUB_N + (k + 1) * LANES)
                zk = z[:, k * LANES:(k + 1) * LANES]
                if kind == "q":
                    a_ref[:, sl] = (_rope128(zk, cos_ref[...], sin_ref[...]) * SWA_QSCALE).astype(BF16)
                elif kind == "k":
                    a_ref[:, sl] = _rope128(zk, cos_ref[...], sin_ref[...]).astype(BF16)
                elif kind == "v":
                    a_ref[:, sl] = zk.astype(BF16)
                else:
                    l_ref[:, sl] = zk

    pl.when(j <= 1)(lambda: block(("q", "q")))
    pl.when(j == 2)(lambda: block(("k", "v")))
    pl.when(j >= 3)(lambda: block(("f32", "f32")))


def _ab_in_proj(x2d, ctx2d, norm_g, mod_all, layer, w_in, cos_a, sin_a, mrow, trow, tm):
    d = x2d.shape[1]
    n_lat_tiles = x2d.shape[0] // tm
    nt = x2d.shape[0] + ctx2d.shape[0]
    tn = 512
    _, ctx_map = _split_rows(n_lat_tiles)
    xw = d // X_CHUNKS
    x_specs = [pl.BlockSpec((tm, xw), lambda i, j, c=c: (jnp.minimum(jnp.where(j > c, i + 1, i), n_lat_tiles - 1), c))
               for c in range(X_CHUNKS)]
    return pl.pallas_call(
        functools.partial(_ab_in_kernel, n_lat_tiles=n_lat_tiles),
        grid=(nt // tm, 5),
        in_specs=x_specs + [
            pl.BlockSpec((tm, d), ctx_map, pipeline_mode=pl.Buffered(1)),
            pl.BlockSpec((1, d), lambda i, j: (0, 0)),
            pl.BlockSpec((1, 1, 6, d), lambda i, j: (layer, mrow(i), 0, 0)),
            pl.BlockSpec((d, tn), lambda i, j: (0, j)),
            pl.BlockSpec((tm, LANES), lambda i, j: (trow(i), 0)),
            pl.BlockSpec((tm, LANES), lambda i, j: (trow(i), 0)),
        ],
        out_specs=[
            pl.BlockSpec((tm, tn), lambda i, j: (i, jnp.minimum(j, 2))),
            pl.BlockSpec((tm, tn), lambda i, j: (i, jnp.maximum(j - 3, 0))),
        ],
        out_shape=[
            jax.ShapeDtypeStruct((nt, 3 * tn), BF16),
            jax.ShapeDtypeStruct((nt, 2 * tn), F32),
        ],
        scratch_shapes=[pltpu.VMEM((tm, d), BF16)],
        name="ab_in_proj",
        compiler_params=_cparams(("arbitrary", "arbitrary")),
    )(*([x2d] * X_CHUNKS), ctx2d, norm_g, mod_all, w_in, cos_a, sin_a)


def _rope64(r, cosp, sinp):
    lane = lax.broadcasted_iota(jnp.int32, r.shape, 1)
    partner = jnp.where(lane < 32, pltpu.roll(r, 96, axis=1), pltpu.roll(r, 32, axis=1))
    return r * cosp + partner * sinp


def _mla_proj_kernel(l_ref, qg_ref, kvg_ref, wq_ref, wkv_ref, cos_ref, sin_ref, qm_ref, kc_ref, vm_ref):
    cosp = cos_ref[...]
    sinp = sin_ref[...]

    def rms(x, g):
        return x * lax.rsqrt(jnp.mean(x * x, axis=-1, keepdims=True) + NORM_EPS) * g

    qn = rms(l_ref[:, :MLA_Q_RANK], qg_ref[...]).astype(BF16)
    qm = _dot(qn, wq_ref[...])
    kvn = rms(l_ref[:, MLA_Q_RANK:MLA_Q_RANK + MLA_KV_RANK], kvg_ref[...]).astype(BF16)
    kv = _dot(kvn, wkv_ref[...])
    kr = l_ref[:, MLA_Q_RANK + MLA_KV_RANK:MLA_Q_RANK + MLA_KV_RANK + LANES]
    krr = _rope64(kr, cosp, sinp).astype(BF16)
    ones = jnp.ones((l_ref.shape[0], LANES), BF16)
    for h in range(MLA_HEADS):
        o = h * MLA_HEAD_PAD
        qm_ref[:, o:o + LANES] = (qm[:, o:o + LANES] * MLA_QSCALE).astype(BF16)
        qm_ref[:, o + LANES:o + 2 * LANES] = (
            _rope64(qm[:, o + LANES:o + 2 * LANES], cosp, sinp) * MLA_QSCALE).astype(BF16)
        kc_ref[:, o:o + LANES] = kv[:, o:o + LANES].astype(BF16)
        kc_ref[:, o + LANES:o + 2 * LANES] = krr
        vm_ref[:, o:o + LANES] = kv[:, o + LANES:o + 2 * LANES].astype(BF16)
        vm_ref[:, o + LANES:o + 2 * LANES] = ones


def _mla_proj(lat, q_norm_g, kv_norm_g, wq, wkv, cos_b, sin_b, trow, tm):
    nt = lat.shape[0]
    hp = MLA_HEADS * MLA_HEAD_PAD
    return pl.pallas_call(
        _mla_proj_kernel,
        grid=(nt // tm,),
        in_specs=[
            pl.BlockSpec((tm, lat.shape[1]), lambda i: (i, 0)),
            pl.BlockSpec((1, MLA_Q_RANK), lambda i: (0, 0)),
            pl.BlockSpec((1, MLA_KV_RANK), lambda i: (0, 0)),
            pl.BlockSpec(wq.shape, lambda i: (0, 0)),
            pl.BlockSpec(wkv.shape, lambda i: (0, 0)),
            pl.BlockSpec((tm, LANES), lambda i: (trow(i), 0)),
            pl.BlockSpec((tm, LANES), lambda i: (trow(i), 0)),
        ],
        out_specs=[
            pl.BlockSpec((tm, hp), lambda i: (i, 0)),
            pl.BlockSpec((tm, hp), lambda i: (i, 0)),
            pl.BlockSpec((tm, hp), lambda i: (i, 0)),
        ],
        out_shape=[
            jax.ShapeDtypeStruct((nt, hp), BF16),
            jax.ShapeDtypeStruct((nt, hp), BF16),
            jax.ShapeDtypeStruct((nt, hp), BF16),
        ],
        name="mla_proj",
        compiler_params=_cparams(("arbitrary",)),
    )(lat, q_norm_g, kv_norm_g, wq, wkv, cos_b, sin_b)


def _swa_softmax_pv(s, sk, v_ext):
    d = SWA_HEAD_DIM
    m = jnp.maximum(jnp.max(s, axis=-1, keepdims=True), sk)
    o_ext = _dot(jnp.exp2(s - m).astype(BF16), v_ext)
    return o_ext[:, :d] / (o_ext[:, d:] + jnp.exp2(sk - m))


def _swa_ctx_kernel(sink_ref, q_ref, kc, vc, o_ref):
    kvh = pl.program_id(1)
    d = SWA_HEAD_DIM
    g_heads = SWA_HEADS // SWA_KV_HEADS
    k_all = kc[...]
    v_ext = jnp.concatenate([vc[...], jnp.ones_like(vc)], axis=1)
    for g in range(g_heads):
        sk = sink_ref[kvh * g_heads + g] * LOG2_E
        s = _dot_nt(q_ref[:, g * d:(g + 1) * d], k_all)
        o_ref[:, g * d:(g + 1) * d] = _swa_softmax_pv(s, sk, v_ext).astype(BF16)


def _swa_kernel(sink_ref, q_ref, kp, km, kn, kc, vp, vm, vn, vc, o_ref, *, tq, s_len):
    i = pl.program_id(1)
    d = SWA_HEAD_DIM
    g_heads = SWA_HEADS // SWA_KV_HEADS
    nb = tq + 2 * SWA_WINDOW
    shape = (tq, nb + kc.shape[0])
    row = lax.broadcasted_iota(jnp.int32, shape, 0)
    col = lax.broadcasted_iota(jnp.int32, shape, 1)
    kpos = i * tq - SWA_WINDOW + col
    valid = (col >= nb) | ((jnp.abs(col - SWA_WINDOW - row) <= SWA_WINDOW) & (kpos >= 0) & (kpos < s_len))
    for kvh in range(SWA_KV_HEADS):
        kv = slice(kvh * d, (kvh + 1) * d)
        k_all = jnp.concatenate([kp[:, kv], km[:, kv], kn[:, kv], kc[:, kv]], axis=0)
        v_all = jnp.concatenate([vp[:, kv], vm[:, kv], vn[:, kv], vc[:, kv]], axis=0)
        v_ext = jnp.concatenate([v_all, jnp.ones_like(v_all)], axis=1)
        for g in range(g_heads):
            head = kvh * g_heads + g
            sk = sink_ref[head] * LOG2_E
            s = jnp.where(valid, _dot_nt(q_ref[:, head * d:(head + 1) * d], k_all), NEG_INF)
            o_ref[:, head * d:(head + 1) * d] = _swa_softmax_pv(s, sk, v_ext).astype(BF16)


def _swa_attention(a, sink, b_sz, cn, s_len, tq):
    d = SWA_HEAD_DIM
    w = SWA_WINDOW
    gw = (SWA_HEADS // SWA_KV_HEADS) * d
    kcol = SWA_HEADS
    vcol = SWA_HEADS + SWA_KV_HEADS
    nq = s_len // tq
    r = tq // w
    c0 = (b_sz * s_len) // cn
    smem = pl.BlockSpec(memory_space=pltpu.SMEM)

    def main(b, i):
        return b * nq + i

    def prev(b, i):
        return b * (s_len // w) + jnp.maximum(i * r - 1, 0)

    def nxt(b, i):
        return b * (s_len // w) + jnp.minimum((i + 1) * r, s_len // w - 1)

    qw = SWA_HEADS * d
    kw = SWA_KV_HEADS * d
    kb = qw // kw
    o_lat = pl.pallas_call(
        functools.partial(_swa_kernel, tq=tq, s_len=s_len),
        grid=(b_sz, nq),
        in_specs=[
            smem,
            pl.BlockSpec((tq, qw), lambda b, i: (main(b, i), 0)),
            pl.BlockSpec((w, kw), lambda b, i: (prev(b, i), kb)),
            pl.BlockSpec((tq, kw), lambda b, i: (main(b, i), kb)),
            pl.BlockSpec((w, kw), lambda b, i: (nxt(b, i), kb)),
            pl.BlockSpec((cn, kw), lambda b, i: (c0 + b, kb)),
            pl.BlockSpec((w, kw), lambda b, i: (prev(b, i), kb + 1)),
            pl.BlockSpec((tq, kw), lambda b, i: (main(b, i), kb + 1)),
            pl.BlockSpec((w, kw), lambda b, i: (nxt(b, i), kb + 1)),
            pl.BlockSpec((cn, kw), lambda b, i: (c0 + b, kb + 1)),
        ],
        out_specs=pl.BlockSpec((tq, qw), lambda b, i: (main(b, i), 0)),
        out_shape=jax.ShapeDtypeStruct((b_sz * s_len, qw), BF16),
        name="swa_latent",
        compiler_params=_cparams(("arbitrary", "arbitrary")),
    )(sink, a, a, a, a, a, a, a, a, a)

    o_ctx = pl.pallas_call(
        _swa_ctx_kernel,
        grid=(b_sz, SWA_KV_HEADS),
        in_specs=[
            smem,
            pl.BlockSpec((cn, gw), lambda b, h: (c0 + b, h)),
            pl.BlockSpec((cn, d), lambda b, h: (c0 + b, kcol + h)),
            pl.BlockSpec((cn, d), lambda b, h: (c0 + b, vcol + h)),
        ],
        out_specs=pl.BlockSpec((cn, gw), lambda b, h: (b, h)),
        out_shape=jax.ShapeDtypeStruct((b_sz * cn, SWA_HEADS * d), BF16),
        name="swa_context",
        compiler_params=_cparams(("arbitrary", "arbitrary")),
    )(sink, a, a, a)
    return o_lat, o_ctx


def _mla_kernel(q_ref, kc_ref, vc_ref, *refs, tk, latent, n_cast=0):
    if latent:
        kl_ref, vl_ref = refs[:2]
        cast_in = refs[2:2 + n_cast]
        o_ref = refs[2 + n_cast]
        cast_out = refs[3 + n_cast:]
        for src, dst in zip(cast_in, cast_out):
            dst[...] = src[...].astype(BF16)
    else:
        (o_ref,) = refs
    dv = MLA_V_DIM
    hp = MLA_HEAD_PAD
    n_heads = q_ref.shape[1] // hp
    chunks = []
    if latent:
        chunks += [(kl_ref, vl_ref, slice(j * tk, (j + 1) * tk)) for j in range(kl_ref.shape[0] // tk)]
    chunks.append((kc_ref, vc_ref, slice(None)))
    m = [None] * n_heads
    acc = [None] * n_heads
    for k_ref, v_ref, sl in chunks:
        for hh in range(n_heads):
            cols = slice(hh * hp, (hh + 1) * hp)
            s = _dot_nt(q_ref[:, cols], k_ref[sl, cols])
            if m[hh] is None:
                m[hh] = jnp.max(s, axis=-1, keepdims=True)
                acc[hh] = _dot(jnp.exp2(s - m[hh]).astype(BF16), v_ref[sl, cols])
            else:
                m_new = jnp.maximum(m[hh], jnp.max(s, axis=-1, keepdims=True))
                acc[hh] = (jnp.exp2(m[hh] - m_new) * acc[hh]
                           + _dot(jnp.exp2(s - m_new).astype(BF16), v_ref[sl, cols]))
                m[hh] = m_new
    for hh in range(n_heads):
        o_ref[:, hh * dv:(hh + 1) * dv] = (acc[hh][:, :dv] / acc[hh][:, dv:]).astype(BF16)


BF16_SUBLANES = 16


def _cast_block(rows, n_steps):
    share = 1
    while (rows * share) % n_steps or (rows * share // n_steps) % BF16_SUBLANES:
        share *= 2
        assert share <= n_steps
    return rows * share // n_steps, share


def _mla_attention(qm, kcat, vext, cast_weights, b_sz, cn, s_len, tq, tk):
    hp = MLA_HEAD_PAD * MLA_HEADS_PER_STEP
    dv = MLA_V_DIM * MLA_HEADS_PER_STEP
    n_hg = MLA_HEADS // MLA_HEADS_PER_STEP
    nq = s_len // tq
    c0 = (b_sz * s_len) // cn
    n_steps = b_sz * n_hg * nq

    cast_specs = []
    for w in cast_weights:
        rows, share = _cast_block(w.shape[0], n_steps)
        cast_specs.append(pl.BlockSpec(
            (rows, w.shape[1]), lambda b, h, i, share=share: (((b * n_hg + h) * nq + i) // share, 0)))

    o_lat, *w_bf16 = pl.pallas_call(
        functools.partial(_mla_kernel, tk=tk, latent=True, n_cast=len(cast_weights)),
        grid=(b_sz, n_hg, nq),
        in_specs=[
            pl.BlockSpec((tq, hp), lambda b, h, i: (b * nq + i, h)),
            pl.BlockSpec((cn, hp), lambda b, h, i: (c0 + b, h)),
            pl.BlockSpec((cn, hp), lambda b, h, i: (c0 + b, h)),
            pl.BlockSpec((s_len, hp), lambda b, h, i: (b, h)),
            pl.BlockSpec((s_len, hp), lambda b, h, i: (b, h)),
        ] + cast_specs,
        out_specs=[pl.BlockSpec((tq, dv), lambda b, h, i: (b * nq + i, h))] + cast_specs,
        out_shape=[jax.ShapeDtypeStruct((b_sz * s_len, n_hg * dv), BF16)]
        + [jax.ShapeDtypeStruct(w.shape, BF16) for w in cast_weights],
        name="mla_latent",
        compiler_params=_cparams(("arbitrary",) * 3),
    )(qm, kcat, vext, kcat, vext, *cast_weights)

    o_ctx = pl.pallas_call(
        functools.partial(_mla_kernel, tk=0, latent=False),
        grid=(b_sz, n_hg),
        in_specs=[
            pl.BlockSpec((cn, hp), lambda b, h: (c0 + b, h)),
            pl.BlockSpec((cn, hp), lambda b, h: (c0 + b, h)),
            pl.BlockSpec((cn, hp), lambda b, h: (c0 + b, h)),
        ],
        out_specs=pl.BlockSpec((cn, dv), lambda b, h: (b, h)),
        out_shape=jax.ShapeDtypeStruct((b_sz * cn, n_hg * dv), BF16),
        name="mla_context",
        compiler_params=_cparams(("arbitrary", "arbitrary")),
    )(qm, kcat, vext)
    return o_lat, o_ctx, w_bf16


def _out_proj_kernel(w_ref, mod_ref, *refs, n_parts, n_lat_tiles):
    o_ref = refs[-1]
    lat = refs[:n_parts + 1]
    ctx = refs[n_parts + 1:-1]

    def compute(side):
        parts = [p[...] for p in side[:-1]]
        res_ref = side[-1]
        for sub in range(o_ref.shape[1] // SUB_N):
            cols = slice(sub * SUB_N, (sub + 1) * SUB_N)
            acc = None
            k0 = 0
            for p in parts:
                kw = p.shape[1]
                t = _dot(p, w_ref[k0:k0 + kw, cols])
                acc = t if acc is None else acc + t
                k0 += kw
            o_ref[:, cols] = res_ref[:, cols] + mod_ref[0, 0, 2:3, cols] * acc

    if n_lat_tiles is None:
        compute(lat)
    else:
        i = pl.program_id(0)
        pl.when(i < n_lat_tiles)(lambda: compute(lat))
        pl.when(i >= n_lat_tiles)(lambda: compute(ctx))


def _out_proj(lat_parts, lat_res, ctx_parts, ctx_res, w, mod_all, layer, mrow, tm):
    lat_rows = lat_parts[0].shape[0]
    d = lat_res.shape[1]
    n_lat = lat_rows // tm
    if ctx_parts is None:
        n_lat_tiles = None
        rows = lat_rows
        specs = [pl.BlockSpec((tm, p.shape[1]), lambda i: (i, 0)) for p in lat_parts]
        specs.append(pl.BlockSpec((tm, d), lambda i: (i, 0)))
        operands = [*lat_parts, lat_res]
    else:
        n_lat_tiles = n_lat
        rows = lat_rows + ctx_parts[0].shape[0]
        lat_map = lambda i: (jnp.minimum(i, n_lat - 1), 0)
        ctx_map = lambda i: (jnp.maximum(i - n_lat, 0), 0)
        specs = [pl.BlockSpec((tm, p.shape[1]), lat_map) for p in lat_parts]
        specs.append(pl.BlockSpec((tm, d), lat_map))
        specs += [pl.BlockSpec((tm, p.shape[1]), ctx_map) for p in ctx_parts]
        specs.append(pl.BlockSpec((tm, d), ctx_map))
        operands = [*lat_parts, lat_res, *ctx_parts, ctx_res]
    return pl.pallas_call(
        functools.partial(_out_proj_kernel, n_parts=len(lat_parts), n_lat_tiles=n_lat_tiles),
        grid=(rows // tm,),
        in_specs=[
            pl.BlockSpec(w.shape, lambda i: (0, 0), pipeline_mode=pl.Buffered(1)),
            pl.BlockSpec((1, 1, 6, d), lambda i: (layer, mrow(i), 0, 0)),
        ] + specs,
        out_specs=pl.BlockSpec((tm, d), lambda i: (i, 0)),
        out_shape=jax.ShapeDtypeStruct((rows, d), F32),
        name="out_proj",
        compiler_params=_cparams(("arbitrary",)),
    )(w, mod_all, *operands)


def _ffn_kernel(x_ref, g_ref, mod_ref, wg_ref, wu_ref, wd_ref, fg_ref, o_ref, h_ref, *, nf, final):
    j = pl.program_id(1)

    @pl.when(j == 0)
    def _():
        _norm_mod_into(h_ref, x_ref, g_ref, mod_ref, 3)
        o_ref[...] = jnp.zeros_like(o_ref)

    h = h_ref[...]
    a = (_silu(_dot(h, wg_ref[0].astype(BF16))) * _dot(h, wu_ref[0].astype(BF16))).astype(BF16)
    o_ref[...] += _dot(a, wd_ref[0].astype(BF16))

    @pl.when(j == nf - 1)
    def _():
        gate = mod_ref[0, 0, 5:6, :]

        def body(t, carry):
            sl = pl.ds(pl.multiple_of(t * NORM_ROWS, NORM_ROWS), NORM_ROWS)
            y = x_ref[sl, :] + gate * o_ref[sl, :]
            if final:
                y = y * lax.rsqrt(jnp.mean(y * y, axis=-1, keepdims=True) + NORM_EPS) * fg_ref[...]
            o_ref[sl, :] = y
            return carry

        lax.fori_loop(0, x_ref.shape[0] // NORM_ROWS, body, 0, unroll=4)


def _ffn(xt, norm_g, mod_all, layer, wg, wu, wd, final_g, mrow, n_rows, tm, tf, final):
    d = xt.shape[1]
    f = wg.shape[2]
    nf = f // tf
    return pl.pallas_call(
        functools.partial(_ffn_kernel, nf=nf, final=final),
        grid=(n_rows // tm, nf),
        in_specs=[
            pl.BlockSpec((tm, d), lambda i, j: (i, 0)),
            pl.BlockSpec((1, d), lambda i, j: (0, 0)),
            pl.BlockSpec((1, 1, 6, d), lambda i, j: (layer, mrow(i), 0, 0)),
            pl.BlockSpec((1, d, tf), lambda i, j: (layer, 0, j)),
            pl.BlockSpec((1, d, tf), lambda i, j: (layer, 0, j)),
            pl.BlockSpec((1, tf, d), lambda i, j: (layer, j, 0)),
            pl.BlockSpec((1, d), lambda i, j: (0, 0)),
        ],
        out_specs=pl.BlockSpec((tm, d), lambda i, j: (i, 0)),
        out_shape=jax.ShapeDtypeStruct((n_rows, d), F32),
        scratch_shapes=[pltpu.VMEM((tm, d), BF16)],
        name="ffn_final" if final else "ffn",
        compiler_params=_cparams(("arbitrary", "arbitrary")),
    )(xt, norm_g, mod_all, wg, wu, wd, final_g)


def _ret_in_kernel(x_ref, g_ref, mod_ref, w_ref, cos_ref, sin_ref, o_ref, h_ref, *, n_qk, k_scale, tn):
    j = pl.program_id(1)

    pl.when(j == 0)(lambda: _norm_mod_into(h_ref, x_ref, g_ref, mod_ref, 0))

    def block(scale):
        h = h_ref[...]
        for sub in range(tn // SUB_N):
            o = sub * SUB_N
            z = _dot(h, w_ref[:, o:o + SUB_N])
            if scale is None:
                o_ref[:, o:o + SUB_N] = z.astype(BF16)
            elif scale == "silu":
                o_ref[:, o:o + SUB_N] = _silu(z).astype(BF16)
            else:
                c = cos_ref[...]
                s = sin_ref[...]
                x1 = z[:, :LANES] * scale if scale != 1.0 else z[:, :LANES]
                x2 = z[:, LANES:] * scale if scale != 1.0 else z[:, LANES:]
                o_ref[:, o:o + LANES] = (x1 * c - x2 * s).astype(BF16)
                o_ref[:, o + LANES:o + SUB_N] = (x1 * s + x2 * c).astype(BF16)

    pl.when(j < n_qk)(lambda: block(1.0))
    pl.when((j >= n_qk) & (j < 2 * n_qk))(lambda: block(k_scale))
    pl.when((j >= 2 * n_qk) & (j < 4 * n_qk))(lambda: block(None))
    pl.when(j >= 4 * n_qk)(lambda: block("silu"))


def _ret_in_proj(xt, norm_g, mod_all, layer, w_in, cos_r, sin_r, mrow, trow, tm, tn):
    nt, d = xt.shape
    n = w_in.shape[1]
    dk = d // RET_HEADS
    return pl.pallas_call(
        functools.partial(_ret_in_kernel, n_qk=d // tn, k_scale=dk ** -0.5, tn=tn),
        grid=(nt // tm, n // tn),
        in_specs=[
            pl.BlockSpec((tm, d), lambda i, j: (i, 0)),
            pl.BlockSpec((1, d), lambda i, j: (0, 0)),
            pl.BlockSpec((1, 1, 6, d), lambda i, j: (layer, mrow(i), 0, 0)),
            pl.BlockSpec((d, tn), lambda i, j: (0, j)),
            pl.BlockSpec((tm, LANES), lambda i, j: (trow(i), 0)),
            pl.BlockSpec((tm, LANES), lambda i, j: (trow(i), 0)),
        ],
        out_specs=pl.BlockSpec((tm, tn), lambda i, j: (i, j)),
        out_shape=jax.ShapeDtypeStruct((nt, n), BF16),
        scratch_shapes=[pltpu.VMEM((tm, d), BF16)],
        name="ret_in_proj",
        compiler_params=_cparams(("arbitrary", "arbitrary")),
    )(xt, norm_g, mod_all, w_in, cos_r, sin_r)


def _log_sigmoid(x):
    return jnp.minimum(x, 0.0) - jnp.log1p(jnp.exp(-jnp.abs(x)))


def _ret_kernel(lf_ref, lb_ref, gn_ref, qc_ref, kc_ref, vc_ref, q_ref, k_ref, v_ref, g_ref, y_ref,
                o_ref, s_ref, dint_ref, qdec_ref, kdec_ref, *, n_ctx, n_lat):
    c = RET_CHUNK
    dk = q_ref.shape[1]
    dv = v_ref.shape[1]
    row = lax.broadcasted_iota(jnp.int32, (c, c), 0)
    col = lax.broadcasted_iota(jnp.int32, (c, c), 1)
    pos = lax.broadcasted_iota(jnp.int32, (c, LANES), 0).astype(F32)
    c_dec = []
    for drn, l_ref in enumerate((lf_ref, lb_ref)):
        lg = _log_sigmoid(l_ref[0])
        lg_c = jnp.broadcast_to(lg[:, 0:1], (1, c))
        diff = (col - row) if drn else (row - col)
        dint_ref[drn] = jnp.where(diff >= 0, jnp.exp(lg_c * jnp.maximum(diff, 0).astype(F32)), 0.0)
        if drn:
            qdec_ref[drn] = jnp.exp(lg * (c - pos))
            kdec_ref[drn] = jnp.exp(lg * pos)
        else:
            qdec_ref[drn] = jnp.exp(lg * (pos + 1.0))
            kdec_ref[drn] = jnp.exp(lg * (c - 1.0 - pos))
        c_dec.append(jnp.exp(jnp.broadcast_to(lg[:, 0:1], (1, dv)) * c))

    def scale_rows(x, dec):
        return jnp.concatenate([x[:, k * LANES:(k + 1) * LANES].astype(F32) * dec for k in range(dk // LANES)],
                               axis=1).astype(BF16)

    def step(drn, q, k, v, want_out):
        s = s_ref[drn]
        o = None
        if want_out:
            a = (_dot_nt(q, k) * dint_ref[drn]).astype(BF16)
            o = _dot(a, v) + _dot(scale_rows(q, qdec_ref[drn]), s.astype(BF16))
        s_ref[drn] = s * c_dec[drn] + _dot_tn(scale_rows(k, kdec_ref[drn]), v)
        return o

    def rows(t):
        return pl.ds(pl.multiple_of(t * c, c), c)

    def lat_step(drn, t):
        sl = rows(t)
        return step(drn, q_ref[sl, :], k_ref[sl, :], v_ref[sl, :], True)

    def finish(t, o):
        sl = rows(t)
        mu = jnp.mean(o, axis=-1, keepdims=True)
        oc = o - mu
        var = jnp.mean(oc * oc, axis=-1, keepdims=True)
        yn = oc * lax.rsqrt(var + NORM_EPS) * gn_ref[...]
        y_ref[sl, :] = (g_ref[sl, :].astype(F32) * yn).astype(BF16)

    s_ref[...] = jnp.zeros_like(s_ref)
    for t in range(n_ctx):
        for drn, tt in ((0, t), (1, n_ctx - 1 - t)):
            sl = slice(tt * c, (tt + 1) * c)
            step(drn, qc_ref[sl, :], kc_ref[sl, :], vc_ref[sl, :], False)

    half = n_lat // 2

    def first_half(t, carry):
        tb = n_lat - 1 - t
        o_ref[rows(t), :] = lat_step(0, t)
        o_ref[rows(tb), :] = lat_step(1, tb)
        return carry

    def second_half(t, carry):
        tb = n_lat - 1 - t
        finish(t, o_ref[rows(t), :] + lat_step(0, t))
        finish(tb, o_ref[rows(tb), :] + lat_step(1, tb))
        return carry

    lax.fori_loop(0, half, first_half, 0, unroll=4)
    lax.fori_loop(half, n_lat, second_half, 0, unroll=4)


def _retention(r, lg_f, lg_b, gn_g, b_sz, cn, s_len, d):
    h_n = RET_HEADS
    dk = d // h_n
    dv = 2 * dk
    c = RET_CHUNK
    assert cn % c == 0 and s_len % (2 * c) == 0
    c0 = (b_sz * s_len) // cn
    kq, kk, kv, kg = 0, d // dk, (2 * d) // dv, (4 * d) // dv

    def lg_spec():
        return pl.BlockSpec((1, 1, LANES), lambda b, h: (h, 0, 0))

    return pl.pallas_call(
        functools.partial(_ret_kernel, n_ctx=cn // c, n_lat=s_len // c),
        grid=(b_sz, h_n),
        in_specs=[
            lg_spec(), lg_spec(),
            pl.BlockSpec((1, dv), lambda b, h: (0, h)),
            pl.BlockSpec((cn, dk), lambda b, h: (c0 + b, kq + h)),
            pl.BlockSpec((cn, dk), lambda b, h: (c0 + b, kk + h)),
            pl.BlockSpec((cn, dv), lambda b, h: (c0 + b, kv + h)),
            pl.BlockSpec((s_len, dk), lambda b, h: (b, kq + h)),
            pl.BlockSpec((s_len, dk), lambda b, h: (b, kk + h)),
            pl.BlockSpec((s_len, dv), lambda b, h: (b, kv + h)),
            pl.BlockSpec((s_len, dv), lambda b, h: (b, kg + h)),
        ],
        out_specs=pl.BlockSpec((s_len, dv), lambda b, h: (b, h)),
        out_shape=jax.ShapeDtypeStruct((b_sz * s_len, h_n * dv), BF16),
        scratch_shapes=[
            pltpu.VMEM((s_len, dv), F32),
            pltpu.VMEM((2, dk, dv), F32),
            pltpu.VMEM((2, c, c), F32),
            pltpu.VMEM((2, c, LANES), F32),
            pltpu.VMEM((2, c, LANES), F32),
        ],
        name="retention",
        compiler_params=_cparams(("arbitrary", "arbitrary")),
    )(lg_f, lg_b, gn_g, r, r, r, r, r, r, r)


def _rope_angles(s_len, rot_dim):
    rows = s_len // GRID_W
    row = np.repeat(np.arange(rows, dtype=np.float32), GRID_W)
    col = (np.arange(s_len) % GRID_W).astype(np.float32)
    n_freq = rot_dim // 4
    inv = (np.float32(ROPE_BASE) ** (-np.arange(n_freq, dtype=np.float32) / np.float32(n_freq))).astype(np.float32)
    ang = np.concatenate([row[:, None] * inv, col[:, None] * inv], axis=-1).astype(np.float32)
    return np.cos(ang).astype(np.float32), np.sin(ang).astype(np.float32)


def _position_table(lat_table, ctx_row, pad_rows):
    width = lat_table.shape[1]
    return jnp.asarray(np.concatenate([lat_table, np.broadcast_to(ctx_row[None, :], (pad_rows, width))],
                                      axis=0).astype(np.float32))


def kernel(x, c, ctx, c_ctx, mod_w, mod_b, norm_mix_g, norm_ffn_g, ffn_w_gate, ffn_w_up, ffn_w_down, ab_w_in, ab_w_out, swa_sink, mla_q_norm_g, mla_w_q_b, mla_kv_norm_g, mla_w_kv_b, ret_w_in, ret_decay_logit_fwd, ret_decay_logit_bwd, ret_gn_g, ret_w_out, final_norm_g):
    b_sz, s_len, d = x.shape
    cn = ctx.shape[1]
    depth = mod_w.shape[0]
    assert depth == 2 and ab_w_in.shape[0] == 1 and ret_w_in.shape[0] == 1
    assert b_sz + 1 <= 8
    n_lat_rows = b_sz * s_len
    n_ctx_rows = b_sz * cn
    nt = n_lat_rows + n_ctx_rows

    tm = min(1024, n_ctx_rows, s_len)
    tq = min(512, s_len)
    tk = min(1024, s_len)
    assert n_ctx_rows % tm == 0 and s_len % tm == 0 and s_len % cn == 0 and s_len % tq == 0 and s_len % tk == 0

    def make_mrow(t):
        n_lat_tiles = n_lat_rows // t
        per_b = s_len // t
        return lambda i: jnp.where(i < n_lat_tiles, 1 + i // per_b, 0)

    mrow = make_mrow(tm)
    tm_o = min(512, tm)
    mrow_o = make_mrow(tm_o)

    x2d = x.reshape(n_lat_rows, d)
    ctx2d = ctx.reshape(n_ctx_rows, d)

    cond8 = jnp.zeros((8, d), F32).at[0].set(c_ctx).at[1:1 + b_sz].set(c)
    mod_all = _modulation(cond8, mod_w, mod_b).reshape(depth, 8, 6, d)

    cos_a, sin_a = _rope_angles(s_len, SWA_HEAD_DIM)
    cos_b, sin_b = _rope_angles(s_len, MLA_ROPE_DIM)
    cos_r, sin_r = _rope_angles(s_len, d // RET_HEADS)
    ones = np.ones((LANES,), np.float32)
    zeros = np.zeros((LANES,), np.float32)
    half = np.concatenate([np.ones((64,), np.float32), np.zeros((64,), np.float32)])
    z64 = np.zeros((s_len, 64), np.float32)
    table = functools.partial(_position_table, pad_rows=tm)
    per_b = s_len // tm
    trow = lambda i: jnp.where(i < n_lat_rows // tm, i % per_b, per_b)
    t_cos_a = table(np.concatenate([cos_a, cos_a], axis=1), ones)
    t_sin_a = table(np.concatenate([-sin_a, sin_a], axis=1), zeros)
    t_cos_b = table(np.concatenate([cos_b, cos_b, z64], axis=1), half)
    t_sin_b = table(np.concatenate([-sin_b, sin_b, z64], axis=1), zeros)
    t_cos_r = table(cos_r, ones)
    t_sin_r = table(sin_r, zeros)

    bf = lambda w: w.astype(BF16)
    w_ab_in = jnp.pad(ab_w_in[0], ((0, 0), (0, 2560 - ab_w_in.shape[2]))).astype(BF16)
    wq_b = jnp.pad(mla_w_q_b[0].reshape(MLA_Q_RANK, MLA_HEADS, MLA_NOPE_DIM + MLA_ROPE_DIM),
                   ((0, 0), (0, 0), (0, MLA_HEAD_PAD - MLA_NOPE_DIM - MLA_ROPE_DIM))
                   ).reshape(MLA_Q_RANK, MLA_HEADS * MLA_HEAD_PAD).astype(BF16)
    wkv_b = bf(mla_w_kv_b[0])
    tf = 512
    final_g = final_norm_g[None, :]
    f_hidden = ffn_w_gate.shape[2]
    later_weights = [ffn_w_gate.reshape(depth * d, f_hidden), ffn_w_up.reshape(depth * d, f_hidden),
                     ffn_w_down.reshape(depth * f_hidden, d), ab_w_out[0], ret_w_in[0], ret_w_out[0]]

    a_qkv, lat = _ab_in_proj(x2d, ctx2d, norm_mix_g[0:1], mod_all, 0, w_ab_in, t_cos_a, t_sin_a, mrow, trow, tm)
    qm, kcat, vext = _mla_proj(lat, mla_q_norm_g[0:1], mla_kv_norm_g[0:1], wq_b, wkv_b, t_cos_b, t_sin_b, trow, tm)
    oa, oa_c = _swa_attention(a_qkv, swa_sink[0], b_sz, cn, s_len, tq)
    ob, ob_c, (wg, wu, wd, w_ab_out, w_ret_in, w_ret_out) = _mla_attention(
        qm, kcat, vext, later_weights, b_sz, cn, s_len, min(1024, s_len), tk)
    wg = wg.reshape(depth, d, f_hidden)
    wu = wu.reshape(depth, d, f_hidden)
    wd = wd.reshape(depth, f_hidden, d)
    xt = _out_proj([oa, ob], x2d, [oa_c, ob_c], ctx2d, w_ab_out, mod_all, 0, mrow_o, tm_o)
    xt = _ffn(xt, norm_ffn_g[0:1], mod_all, 0, wg, wu, wd, final_g, mrow, nt, tm, tf, final=False)

    r = _ret_in_proj(xt, norm_mix_g[1:2], mod_all, 1, w_ret_in, t_cos_r, t_sin_r, mrow, trow, tm, 2048)
    lg_shape = (RET_HEADS, 1, LANES)
    lg_f = jnp.broadcast_to(ret_decay_logit_fwd[0].astype(F32)[:, None, None], lg_shape)
    lg_b = jnp.broadcast_to(ret_decay_logit_bwd[0].astype(F32)[:, None, None], lg_shape)
    y = _retention(r, lg_f, lg_b, ret_gn_g[0:1], b_sz, cn, s_len, d)
    xl = _out_proj([y], xt, None, None, w_ret_out, mod_all, 1, mrow_o, tm_o)
    out = _ffn(xl, norm_ffn_g[1:2], mod_all, 1, wg, wu, wd, final_g, mrow, n_lat_rows, tm, tf, final=True)
    return out.reshape(b_sz, s_len, d)
```

```python
import functools

import jax
import jax.numpy as jnp
import numpy as np
from jax import lax
from jax.experimental import pallas as pl
from jax.experimental.pallas import tpu as pltpu

GRID_W = 64
ROPE_BASE = 10000.0
NORM_EPS = 1e-6
NEG_INF = -1e30
LOG2_E = 1.4426950408889634

SWA_HEADS = 8
SWA_KV_HEADS = 2
SWA_HEAD_DIM = 128
SWA_WINDOW = 128
SWA_QSCALE = SWA_HEAD_DIM ** -0.5 * LOG2_E

MLA_HEADS = 8
MLA_Q_RANK = 512
MLA_KV_RANK = 256
MLA_NOPE_DIM = 128
MLA_ROPE_DIM = 64
MLA_V_DIM = 128
MLA_HEAD_PAD = 256
MLA_QSCALE = (MLA_NOPE_DIM + MLA_ROPE_DIM) ** -0.5 * LOG2_E
MLA_HEADS_PER_STEP = 2

RET_HEADS = 8
RET_CHUNK = 256

LANES = 128
V7X_VMEM_BYTES = 64 * 1024 * 1024
VMEM_LIMIT = V7X_VMEM_BYTES - 8 * 1024 * 1024

F32 = jnp.float32
BF16 = jnp.bfloat16


def _cparams(sem):
    return pltpu.CompilerParams(dimension_semantics=sem, vmem_limit_bytes=VMEM_LIMIT)


def _dot(a, b):
    return jnp.dot(a, b, preferred_element_type=F32)


def _dot_nt(a, b):
    return lax.dot_general(a, b, (((1,), (1,)), ((), ())), preferred_element_type=F32)


def _dot_tn(a, b):
    return lax.dot_general(a, b, (((0,), (0,)), ((), ())), preferred_element_type=F32)


def _silu(x):
    h = 0.5 * x
    return h + h * jnp.tanh(h)


NORM_ROWS = 32
SUB_N = 256
X_CHUNKS = 4


def _norm_mod_into(h_ref, x_ref, g_ref, mod_ref, shift_row):
    shift = mod_ref[0, 0, shift_row:shift_row + 1, :]
    gain = g_ref[...] * (1.0 + mod_ref[0, 0, shift_row + 1:shift_row + 2, :])

    x_refs = x_ref if isinstance(x_ref, (tuple, list)) else (x_ref,)

    def body(t, carry):
        sl = pl.ds(pl.multiple_of(t * NORM_ROWS, NORM_ROWS), NORM_ROWS)
        x = jnp.concatenate([r[sl, :] for r in x_refs], axis=1) if len(x_refs) > 1 else x_refs[0][sl, :]
        inv = lax.rsqrt(jnp.mean(x * x, axis=-1, keepdims=True) + NORM_EPS)
        h_ref[sl, :] = ((x * inv) * gain + shift).astype(BF16)
        return carry

    lax.fori_loop(0, x_refs[0].shape[0] // NORM_ROWS, body, 0, unroll=4)


def _split_rows(n_lat_tiles):
    lat = lambda i, j: (jnp.minimum(i, n_lat_tiles - 1), 0)
    ctx = lambda i, j: (jnp.maximum(i - n_lat_tiles, 0), 0)
    return lat, ctx


def _mod_kernel(c_ref, w_ref, b_ref, o_ref):
    a = _silu(c_ref[...]).astype(BF16)
    o_ref[0] = _dot(a, w_ref[0].astype(BF16)) + b_ref[0]


def _modulation(cond8, mod_w, mod_b):
    depth, d, n = mod_w.shape
    tn = 1024
    return pl.pallas_call(
        _mod_kernel,
        grid=(depth, n // tn),
        in_specs=[
            pl.BlockSpec((8, d), lambda l, j: (0, 0)),
            pl.BlockSpec((1, d, tn), lambda l, j: (l, 0, j)),
            pl.BlockSpec((1, 1, tn), lambda l, j: (l, 0, j)),
        ],
        out_specs=pl.BlockSpec((1, 8, tn), lambda l, j: (l, 0, j)),
        out_shape=jax.ShapeDtypeStruct((depth, 8, n), F32),
        name="modulation",
        compiler_params=_cparams(("arbitrary", "arbitrary")),
    )(cond8, mod_w, mod_b.reshape(depth, 1, n))


def _rope128(z, cosf, sins):
    return z * cosf + pltpu.roll(z, 64, axis=1) * sins


def _ab_in_kernel(*refs, n_lat_tiles):
    x_refs = refs[:X_CHUNKS]
    c_ref, g_ref, mod_ref, w_ref, cos_ref, sin_ref, a_ref, l_ref, h_ref = refs[X_CHUNKS:]
    i = pl.program_id(0)
    j = pl.program_id(1)

    pl.when((j == 0) & (i < n_lat_tiles))(lambda: _norm_mod_into(h_ref, x_refs, g_ref, mod_ref, 0))
    pl.when((j == 0) & (i >= n_lat_tiles))(lambda: _norm_mod_into(h_ref, c_ref, g_ref, mod_ref, 0))

    def block(kinds):
        h = h_ref[...]
        for half, kind in enumerate(kinds):
            z = _dot(h, w_ref[:, half * SUB_N:(half + 1) * SUB_N])
            for k in range(SUB_N // LANES):
                sl = slice(half * SUB_N + k * LANES, half * SUB_N + (k + 1) * LANES)
                zk = z[:, k * LANES:(k + 1) * LANES]
                if kind == "q":
                    a_ref[:, sl] = (_rope128(zk, cos_ref[...], sin_ref[...]) * SWA_QSCALE).astype(BF16)
                elif kind == "k":
                    a_ref[:, sl] = _rope128(zk, cos_ref[...], sin_ref[...]).astype(BF16)
                elif kind == "v":
                    a_ref[:, sl] = zk.astype(BF16)
                else:
                    l_ref[:, sl] = zk

    pl.when(j <= 1)(lambda: block(("q", "q")))
    pl.when(j == 2)(lambda: block(("k", "v")))
    pl.when(j >= 3)(lambda: block(("f32", "f32")))


def _ab_in_proj(x2d, ctx2d, norm_g, mod_all, layer, w_in, cos_a, sin_a, mrow, trow, tm):
    d = x2d.shape[1]
    n_lat_tiles = x2d.shape[0] // tm
    nt = x2d.shape[0] + ctx2d.shape[0]
    tn = 512
    _, ctx_map = _split_rows(n_lat_tiles)
    xw = d // X_CHUNKS
    x_specs = [pl.BlockSpec((tm, xw), lambda i, j, c=c: (jnp.minimum(jnp.where(j > c, i + 1, i), n_lat_tiles - 1), c))
               for c in range(X_CHUNKS)]
    return pl.pallas_call(
        functools.partial(_ab_in_kernel, n_lat_tiles=n_lat_tiles),
        grid=(nt // tm, 5),
        in_specs=x_specs + [
            pl.BlockSpec((tm, d), ctx_map, pipeline_mode=pl.Buffered(1)),
            pl.BlockSpec((1, d), lambda i, j: (0, 0)),
            pl.BlockSpec((1, 1, 6, d), lambda i, j: (layer, mrow(i), 0, 0)),
            pl.BlockSpec((d, tn), lambda i, j: (0, j)),
            pl.BlockSpec((tm, LANES), lambda i, j: (trow(i), 0)),
            pl.BlockSpec((tm, LANES), lambda i, j: (trow(i), 0)),
        ],
        out_specs=[
            pl.BlockSpec((tm, tn), lambda i, j: (i, jnp.minimum(j, 2))),
            pl.BlockSpec((tm, tn), lambda i, j: (i, jnp.maximum(j - 3, 0))),
        ],
        out_shape=[
            jax.ShapeDtypeStruct((nt, 3 * tn), BF16),
            jax.ShapeDtypeStruct((nt, 2 * tn), F32),
        ],
        scratch_shapes=[pltpu.VMEM((tm, d), BF16)],
        name="ab_in_proj",
        compiler_params=_cparams(("arbitrary", "arbitrary")),
    )(*([x2d] * X_CHUNKS), ctx2d, norm_g, mod_all, w_in, cos_a, sin_a)


def _rope64(r, cosp, sinp):
    lane = lax.broadcasted_iota(jnp.int32, r.shape, 1)
    partner = jnp.where(lane < 32, pltpu.roll(r, 96, axis=1), pltpu.roll(r, 32, axis=1))
    return r * cosp + partner * sinp


def _mla_proj_kernel(l_ref, qg_ref, kvg_ref, wq_ref, wkv_ref, cos_ref, sin_ref, qm_ref, kc_ref, vm_ref):
    cosp = cos_ref[...]
    sinp = sin_ref[...]

    def rms(x, g):
        return x * lax.rsqrt(jnp.mean(x * x, axis=-1, keepdims=True) + NORM_EPS) * g

    qn = rms(l_ref[:, :MLA_Q_RANK], qg_ref[...]).astype(BF16)
    qm = _dot(qn, wq_ref[...])
    kvn = rms(l_ref[:, MLA_Q_RANK:MLA_Q_RANK + MLA_KV_RANK], kvg_ref[...]).astype(BF16)
    kv = _dot(kvn, wkv_ref[...])
    kr = l_ref[:, MLA_Q_RANK + MLA_KV_RANK:MLA_Q_RANK + MLA_KV_RANK + LANES]
    krr = _rope64(kr, cosp, sinp).astype(BF16)
    ones = jnp.ones((l_ref.shape[0], LANES), BF16)
    for h in range(MLA_HEADS):
        o = h * MLA_HEAD_PAD
        qm_ref[:, o:o + LANES] = (qm[:, o:o + LANES] * MLA_QSCALE).astype(BF16)
        qm_ref[:, o + LANES:o + 2 * LANES] = (
            _rope64(qm[:, o + LANES:o + 2 * LANES], cosp, sinp) * MLA_QSCALE).astype(BF16)
        kc_ref[:, o:o + LANES] = kv[:, o:o + LANES].astype(BF16)
        kc_ref[:, o + LANES:o + 2 * LANES] = krr
        vm_ref[:, o:o + LANES] = kv[:, o + LANES:o + 2 * LANES].astype(BF16)
        vm_ref[:, o + LANES:o + 2 * LANES] = ones


def _mla_proj(lat, q_norm_g, kv_norm_g, wq, wkv, cos_b, sin_b, trow, tm):
    nt = lat.shape[0]
    hp = MLA_HEADS * MLA_HEAD_PAD
    return pl.pallas_call(
        _mla_proj_kernel,
        grid=(nt // tm,),
        in_specs=[
            pl.BlockSpec((tm, lat.shape[1]), lambda i: (i, 0)),
            pl.BlockSpec((1, MLA_Q_RANK), lambda i: (0, 0)),
            pl.BlockSpec((1, MLA_KV_RANK), lambda i: (0, 0)),
            pl.BlockSpec(wq.shape, lambda i: (0, 0)),
            pl.BlockSpec(wkv.shape, lambda i: (0, 0)),
            pl.BlockSpec((tm, LANES), lambda i: (trow(i), 0)),
            pl.BlockSpec((tm, LANES), lambda i: (trow(i), 0)),
        ],
        out_specs=[
            pl.BlockSpec((tm, hp), lambda i: (i, 0)),
            pl.BlockSpec((tm, hp), lambda i: (i, 0)),
            pl.BlockSpec((tm, hp), lambda i: (i, 0)),
        ],
        out_shape=[
            jax.ShapeDtypeStruct((nt, hp), BF16),
            jax.ShapeDtypeStruct((nt, hp), BF16),
            jax.ShapeDtypeStruct((nt, hp), BF16),
        ],
        name="mla_proj",
        compiler_params=_cparams(("arbitrary",)),
    )(lat, q_norm_g, kv_norm_g, wq, wkv, cos_b, sin_b)


def _swa_softmax_pv(s, sk, v_ext):
    d = SWA_HEAD_DIM
    m = jnp.maximum(jnp.max(s, axis=-1, keepdims=True), sk)
    o_ext = _dot(jnp.exp2(s - m).astype(BF16), v_ext)
    return o_ext[:, :d] / (o_ext[:, d:] + jnp.exp2(sk - m))


def _swa_ctx_kernel(sink_ref, q_ref, kc, vc, o_ref):
    kvh = pl.program_id(1)
    d = SWA_HEAD_DIM
    g_heads = SWA_HEADS // SWA_KV_HEADS
    k_all = kc[...]
    v_ext = jnp.concatenate([vc[...], jnp.ones_like(vc)], axis=1)
    for g in range(g_heads):
        sk = sink_ref[kvh * g_heads + g] * LOG2_E
        s = _dot_nt(q_ref[:, g * d:(g + 1) * d], k_all)
        o_ref[:, g * d:(g + 1) * d] = _swa_softmax_pv(s, sk, v_ext).astype(BF16)


def _swa_kernel(sink_ref, q_ref, kp, km, kn, kc, vp, vm, vn, vc, o_ref, *, tq, s_len):
    i = pl.program_id(1)
    d = SWA_HEAD_DIM
    g_heads = SWA_HEADS // SWA_KV_HEADS
    nb = tq + 2 * SWA_WINDOW
    shape = (tq, nb + kc.shape[0])
    row = lax.broadcasted_iota(jnp.int32, shape, 0)
    col = lax.broadcasted_iota(jnp.int32, shape, 1)
    kpos = i * tq - SWA_WINDOW + col
    valid = (col >= nb) | ((jnp.abs(col - SWA_WINDOW - row) <= SWA_WINDOW) & (kpos >= 0) & (kpos < s_len))
    for kvh in range(SWA_KV_HEADS):
        kv = slice(kvh * d, (kvh + 1) * d)
        k_all = jnp.concatenate([kp[:, kv], km[:, kv], kn[:, kv], kc[:, kv]], axis=0)
        v_all = jnp.concatenate([vp[:, kv], vm[:, kv], vn[:, kv], vc[:, kv]], axis=0)
        v_ext = jnp.concatenate([v_all, jnp.ones_like(v_all)], axis=1)
        for g in range(g_heads):
            head = kvh * g_heads + g
            sk = sink_ref[head] * LOG2_E
            s = jnp.where(valid, _dot_nt(q_ref[:, head * d:(head + 1) * d], k_all), NEG_INF)
            o_ref[:, head * d:(head + 1) * d] = _swa_softmax_pv(s, sk, v_ext).astype(BF16)


def _swa_attention(a, sink, b_sz, cn, s_len, tq):
    d = SWA_HEAD_DIM
    w = SWA_WINDOW
    gw = (SWA_HEADS // SWA_KV_HEADS) * d
    kcol = SWA_HEADS
    vcol = SWA_HEADS + SWA_KV_HEADS
    nq = s_len // tq
    r = tq // w
    c0 = (b_sz * s_len) // cn
    smem = pl.BlockSpec(memory_space=pltpu.SMEM)

    def main(b, i):
        return b * nq + i

    def prev(b, i):
        return b * (s_len // w) + jnp.maximum(i * r - 1, 0)

    def nxt(b, i):
        return b * (s_len // w) + jnp.minimum((i + 1) * r, s_len // w - 1)

    qw = SWA_HEADS * d
    kw = SWA_KV_HEADS * d
    kb = qw // kw
    o_lat = pl.pallas_call(
        functools.partial(_swa_kernel, tq=tq, s_len=s_len),
        grid=(b_sz, nq),
        in_specs=[
            smem,
            pl.BlockSpec((tq, qw), lambda b, i: (main(b, i), 0)),
            pl.BlockSpec((w, kw), lambda b, i: (prev(b, i), kb)),
            pl.BlockSpec((tq, kw), lambda b, i: (main(b, i), kb)),
            pl.BlockSpec((w, kw), lambda b, i: (nxt(b, i), kb)),
            pl.BlockSpec((cn, kw), lambda b, i: (c0 + b, kb)),
            pl.BlockSpec((w, kw), lambda b, i: (prev(b, i), kb + 1)),
            pl.BlockSpec((tq, kw), lambda b, i: (main(b, i), kb + 1)),
            pl.BlockSpec((w, kw), lambda b, i: (nxt(b, i), kb + 1)),
            pl.BlockSpec((cn, kw), lambda b, i: (c0 + b, kb + 1)),
        ],
        out_specs=pl.BlockSpec((tq, qw), lambda b, i: (main(b, i), 0)),
        out_shape=jax.ShapeDtypeStruct((b_sz * s_len, qw), BF16),
        name="swa_latent",
        compiler_params=_cparams(("arbitrary", "arbitrary")),
    )(sink, a, a, a, a, a, a, a, a, a)

    o_ctx = pl.pallas_call(
        _swa_ctx_kernel,
        grid=(b_sz, SWA_KV_HEADS),
        in_specs=[
            smem,
            pl.BlockSpec((cn, gw), lambda b, h: (c0 + b, h)),
            pl.BlockSpec((cn, d), lambda b, h: (c0 + b, kcol + h)),
            pl.BlockSpec((cn, d), lambda b, h: (c0 + b, vcol + h)),
        ],
        out_specs=pl.BlockSpec((cn, gw), lambda b, h: (b, h)),
        out_shape=jax.ShapeDtypeStruct((b_sz * cn, SWA_HEADS * d), BF16),
        name="swa_context",
        compiler_params=_cparams(("arbitrary", "arbitrary")),
    )(sink, a, a, a)
    return o_lat, o_ctx


def _mla_kernel(q_ref, kc_ref, vc_ref, *refs, tk, latent, n_cast=0):
    if latent:
        kl_ref, vl_ref = refs[:2]
        cast_in = refs[2:2 + n_cast]
        o_ref = refs[2 + n_cast]
        cast_out = refs[3 + n_cast:]
        for src, dst in zip(cast_in, cast_out):
            dst[...] = src[...].astype(BF16)
    else:
        (o_ref,) = refs
    dv = MLA_V_DIM
    hp = MLA_HEAD_PAD
    n_heads = q_ref.shape[1] // hp
    chunks = []
    if latent:
        chunks += [(kl_ref, vl_ref, slice(j * tk, (j + 1) * tk)) for j in range(kl_ref.shape[0] // tk)]
    chunks.append((kc_ref, vc_ref, slice(None)))
    m = [None] * n_heads
    acc = [None] * n_heads
    for k_ref, v_ref, sl in chunks:
        for hh in range(n_heads):
            cols = slice(hh * hp, (hh + 1) * hp)
            s = _dot_nt(q_ref[:, cols], k_ref[sl, cols])
            if m[hh] is None:
                m[hh] = jnp.max(s, axis=-1, keepdims=True)
                acc[hh] = _dot(jnp.exp2(s - m[hh]).astype(BF16), v_ref[sl, cols])
            else:
                m_new = jnp.maximum(m[hh], jnp.max(s, axis=-1, keepdims=True))
                acc[hh] = (jnp.exp2(m[hh] - m_new) * acc[hh]
                           + _dot(jnp.exp2(s - m_new).astype(BF16), v_ref[sl, cols]))
                m[hh] = m_new
    for hh in range(n_heads):
        o_ref[:, hh * dv:(hh + 1) * dv] = (acc[hh][:, :dv] / acc[hh][:, dv:]).astype(BF16)


BF16_SUBLANES = 16


def _cast_block(rows, n_steps):
    share = 1
    while (rows * share) % n_steps or (rows * share // n_steps) % BF16_SUBLANES:
        share *= 2
        assert share <= n_steps
    return rows * share // n_steps, share


def _mla_attention(qm, kcat, vext, cast_weights, b_sz, cn, s_len, tq, tk):
    hp = MLA_HEAD_PAD * MLA_HEADS_PER_STEP
    dv = MLA_V_DIM * MLA_HEADS_PER_STEP
    n_hg = MLA_HEADS // MLA_HEADS_PER_STEP
    nq = s_len // tq
    c0 = (b_sz * s_len) // cn
    n_steps = b_sz * n_hg * nq

    cast_specs = []
    for w in cast_weights:
        rows, share = _cast_block(w.shape[0], n_steps)
        cast_specs.append(pl.BlockSpec(
            (rows, w.shape[1]), lambda b, h, i, share=share: (((b * n_hg + h) * nq + i) // share, 0)))

    o_lat, *w_bf16 = pl.pallas_call(
        functools.partial(_mla_kernel, tk=tk, latent=True, n_cast=len(cast_weights)),
        grid=(b_sz, n_hg, nq),
        in_specs=[
            pl.BlockSpec((tq, hp), lambda b, h, i: (b * nq + i, h)),
            pl.BlockSpec((cn, hp), lambda b, h, i: (c0 + b, h)),
            pl.BlockSpec((cn, hp), lambda b, h, i: (c0 + b, h)),
            pl.BlockSpec((s_len, hp), lambda b, h, i: (b, h)),
            pl.BlockSpec((s_len, hp), lambda b, h, i: (b, h)),
        ] + cast_specs,
        out_specs=[pl.BlockSpec((tq, dv), lambda b, h, i: (b * nq + i, h))] + cast_specs,
        out_shape=[jax.ShapeDtypeStruct((b_sz * s_len, n_hg * dv), BF16)]
        + [jax.ShapeDtypeStruct(w.shape, BF16) for w in cast_weights],
        name="mla_latent",
        compiler_params=_cparams(("arbitrary",) * 3),
    )(qm, kcat, vext, kcat, vext, *cast_weights)

    o_ctx = pl.pallas_call(
        functools.partial(_mla_kernel, tk=0, latent=False),
        grid=(b_sz, n_hg),
        in_specs=[
            pl.BlockSpec((cn, hp), lambda b, h: (c0 + b, h)),
            pl.BlockSpec((cn, hp), lambda b, h: (c0 + b, h)),
            pl.BlockSpec((cn, hp), lambda b, h: (c0 + b, h)),
        ],
        out_specs=pl.BlockSpec((cn, dv), lambda b, h: (b, h)),
        out_shape=jax.ShapeDtypeStruct((b_sz * cn, n_hg * dv), BF16),
        name="mla_context",
        compiler_params=_cparams(("arbitrary", "arbitrary")),
    )(qm, kcat, vext)
    return o_lat, o_ctx, w_bf16


def _out_proj_kernel(w_ref, mod_ref, *refs, n_parts, n_lat_tiles):
    o_ref = refs[-1]
    lat = refs[:n_parts + 1]
    ctx = refs[n_parts + 1:-1]

    def compute(side):
        parts = [p[...] for p in side[:-1]]
        res_ref = side[-1]
        for sub in range(o_ref.shape[1] // SUB_N):
            cols = slice(sub * SUB_N, (sub + 1) * SUB_N)
            acc = None
            k0 = 0
            for p in parts:
                kw = p.shape[1]
                t = _dot(p, w_ref[k0:k0 + kw, cols])
                acc = t if acc is None else acc + t
                k0 += kw
            o_ref[:, cols] = res_ref[:, cols] + mod_ref[0, 0, 2:3, cols] * acc

    if n_lat_tiles is None:
        compute(lat)
    else:
        i = pl.program_id(0)
        pl.when(i < n_lat_tiles)(lambda: compute(lat))
        pl.when(i >= n_lat_tiles)(lambda: compute(ctx))


def _out_proj(lat_parts, lat_res, ctx_parts, ctx_res, w, mod_all, layer, mrow, tm):
    lat_rows = lat_parts[0].shape[0]
    d = lat_res.shape[1]
    n_lat = lat_rows // tm
    if ctx_parts is None:
        n_lat_tiles = None
        rows = lat_rows
        specs = [pl.BlockSpec((tm, p.shape[1]), lambda i: (i, 0)) for p in lat_parts]
        specs.append(pl.BlockSpec((tm, d), lambda i: (i, 0)))
        operands = [*lat_parts, lat_res]
    else:
        n_lat_tiles = n_lat
        rows = lat_rows + ctx_parts[0].shape[0]
        lat_map = lambda i: (jnp.minimum(i, n_lat - 1), 0)
        ctx_map = lambda i: (jnp.maximum(i - n_lat, 0), 0)
        specs = [pl.BlockSpec((tm, p.shape[1]), lat_map) for p in lat_parts]
        specs.append(pl.BlockSpec((tm, d), lat_map))
        specs += [pl.BlockSpec((tm, p.shape[1]), ctx_map) for p in ctx_parts]
        specs.append(pl.BlockSpec((tm, d), ctx_map))
        operands = [*lat_parts, lat_res, *ctx_parts, ctx_res]
    return pl.pallas_call(
        functools.partial(_out_proj_kernel, n_parts=len(lat_parts), n_lat_tiles=n_lat_tiles),
        grid=(rows // tm,),
        in_specs=[
            pl.BlockSpec(w.shape, lambda i: (0, 0), pipeline_mode=pl.Buffered(1)),
            pl.BlockSpec((1, 1, 6, d), lambda i: (layer, mrow(i), 0, 0)),
        ] + specs,
        out_specs=pl.BlockSpec((tm, d), lambda i: (i, 0)),
        out_shape=jax.ShapeDtypeStruct((rows, d), F32),
        name="out_proj",
        compiler_params=_cparams(("arbitrary",)),
    )(w, mod_all, *operands)


def _ffn_kernel(x_ref, g_ref, mod_ref, wg_ref, wu_ref, wd_ref, fg_ref, o_ref, h_ref, *, nf, final):
    j = pl.program_id(1)

    @pl.when(j == 0)
    def _():
        _norm_mod_into(h_ref, x_ref, g_ref, mod_ref, 3)
        o_ref[...] = jnp.zeros_like(o_ref)

    h = h_ref[...]
    a = (_silu(_dot(h, wg_ref[0].astype(BF16))) * _dot(h, wu_ref[0].astype(BF16))).astype(BF16)
    o_ref[...] += _dot(a, wd_ref[0].astype(BF16))

    @pl.when(j == nf - 1)
    def _():
        gate = mod_ref[0, 0, 5:6, :]

        def body(t, carry):
            sl = pl.ds(pl.multiple_of(t * NORM_ROWS, NORM_ROWS), NORM_ROWS)
            y = x_ref[sl, :] + gate * o_ref[sl, :]
            if final:
                y = y * lax.rsqrt(jnp.mean(y * y, axis=-1, keepdims=True) + NORM_EPS) * fg_ref[...]
            o_ref[sl, :] = y
            return carry

        lax.fori_loop(0, x_ref.shape[0] // NORM_ROWS, body, 0, unroll=4)


def _ffn(xt, norm_g, mod_all, layer, wg, wu, wd, final_g, mrow, n_rows, tm, tf, final):
    d = xt.shape[1]
    f = wg.shape[2]
    nf = f // tf
    return pl.pallas_call(
        functools.partial(_ffn_kernel, nf=nf, final=final),
        grid=(n_rows // tm, nf),
        in_specs=[
            pl.BlockSpec((tm, d), lambda i, j: (i, 0)),
            pl.BlockSpec((1, d), lambda i, j: (0, 0)),
            pl.BlockSpec((1, 1, 6, d), lambda i, j: (layer, mrow(i), 0, 0)),
            pl.BlockSpec((1, d, tf), lambda i, j: (layer, 0, j)),
            pl.BlockSpec((1, d, tf), lambda i, j: (layer, 0, j)),
            pl.BlockSpec((1, tf, d), lambda i, j: (layer, j, 0)),
            pl.BlockSpec((1, d), lambda i, j: (0, 0)),
        ],
        out_specs=pl.BlockSpec((tm, d), lambda i, j: (i, 0)),
        out_shape=jax.ShapeDtypeStruct((n_rows, d), F32),
        scratch_shapes=[pltpu.VMEM((tm, d), BF16)],
        name="ffn_final" if final else "ffn",
        compiler_params=_cparams(("arbitrary", "arbitrary")),
    )(xt, norm_g, mod_all, wg, wu, wd, final_g)


def _ret_in_kernel(x_ref, g_ref, mod_ref, w_ref, cos_ref, sin_ref, o_ref, h_ref, *, n_qk, k_scale, tn):
    j = pl.program_id(1)

    pl.when(j == 0)(lambda: _norm_mod_into(h_ref, x_ref, g_ref, mod_ref, 0))

    def block(scale):
        h = h_ref[...]
        for sub in range(tn // SUB_N):
            o = sub * SUB_N
            z = _dot(h, w_ref[:, o:o + SUB_N])
            if scale is None:
                o_ref[:, o:o + SUB_N] = z.astype(BF16)
            elif scale == "silu":
                o_ref[:, o:o + SUB_N] = _silu(z).astype(BF16)
            else:
                c = cos_ref[...]
                s = sin_ref[...]
                x1 = z[:, :LANES] * scale if scale != 1.0 else z[:, :LANES]
                x2 = z[:, LANES:] * scale if scale != 1.0 else z[:, LANES:]
                o_ref[:, o:o + LANES] = (x1 * c - x2 * s).astype(BF16)
                o_ref[:, o + LANES:o + SUB_N] = (x1 * s + x2 * c).astype(BF16)

    pl.when(j < n_qk)(lambda: block(1.0))
    pl.when((j >= n_qk) & (j < 2 * n_qk))(lambda: block(k_scale))
    pl.when((j >= 2 * n_qk) & (j < 4 * n_qk))(lambda: block(None))
    pl.when(j >= 4 * n_qk)(lambda: block("silu"))


def _ret_in_proj(xt, norm_g, mod_all, layer, w_in, cos_r, sin_r, mrow, trow, tm, tn):
    nt, d = xt.shape
    n = w_in.shape[1]
    dk = d // RET_HEADS
    return pl.pallas_call(
        functools.partial(_ret_in_kernel, n_qk=d // tn, k_scale=dk ** -0.5, tn=tn),
        grid=(nt // tm, n // tn),
        in_specs=[
            pl.BlockSpec((tm, d), lambda i, j: (i, 0)),
            pl.BlockSpec((1, d), lambda i, j: (0, 0)),
            pl.BlockSpec((1, 1, 6, d), lambda i, j: (layer, mrow(i), 0, 0)),
            pl.BlockSpec((d, tn), lambda i, j: (0, j)),
            pl.BlockSpec((tm, LANES), lambda i, j: (trow(i), 0)),
            pl.BlockSpec((tm, LANES), lambda i, j: (trow(i), 0)),
        ],
        out_specs=pl.BlockSpec((tm, tn), lambda i, j: (i, j)),
        out_shape=jax.ShapeDtypeStruct((nt, n), BF16),
        scratch_shapes=[pltpu.VMEM((tm, d), BF16)],
        name="ret_in_proj",
        compiler_params=_cparams(("arbitrary", "arbitrary")),
    )(xt, norm_g, mod_all, w_in, cos_r, sin_r)


def _log_sigmoid(x):
    return jnp.minimum(x, 0.0) - jnp.log1p(jnp.exp(-jnp.abs(x)))


def _ret_kernel(lf_ref, lb_ref, gn_ref, qc_ref, kc_ref, vc_ref, q_ref, k_ref, v_ref, g_ref, y_ref,
                o_ref, s_ref, dint_ref, qdec_ref, kdec_ref, *, n_ctx, n_lat):
    c = RET_CHUNK
    dk = q_ref.shape[1]
    dv = v_ref.shape[1]
    row = lax.broadcasted_iota(jnp.int32, (c, c), 0)
    col = lax.broadcasted_iota(jnp.int32, (c, c), 1)
    pos = lax.broadcasted_iota(jnp.int32, (c, LANES), 0).astype(F32)
    c_dec = []
    for drn, l_ref in enumerate((lf_ref, lb_ref)):
        lg = _log_sigmoid(l_ref[0])
        lg_c = jnp.broadcast_to(lg[:, 0:1], (1, c))
        diff = (col - row) if drn else (row - col)
        dint_ref[drn] = jnp.where(diff >= 0, jnp.exp(lg_c * jnp.maximum(diff, 0).astype(F32)), 0.0)
        if drn:
            qdec_ref[drn] = jnp.exp(lg * (c - pos))
            kdec_ref[drn] = jnp.exp(lg * pos)
        else:
            qdec_ref[drn] = jnp.exp(lg * (pos + 1.0))
            kdec_ref[drn] = jnp.exp(lg * (c - 1.0 - pos))
        c_dec.append(jnp.exp(jnp.broadcast_to(lg[:, 0:1], (1, dv)) * c))

    def scale_rows(x, dec):
        return jnp.concatenate([x[:, k * LANES:(k + 1) * LANES].astype(F32) * dec for k in range(dk // LANES)],
                               axis=1).astype(BF16)

    def step(drn, q, k, v, want_out):
        s = s_ref[drn]
        o = None
        if want_out:
            a = (_dot_nt(q, k) * dint_ref[drn]).astype(BF16)
            o = _dot(a, v) + _dot(scale_rows(q, qdec_ref[drn]), s.astype(BF16))
        s_ref[drn] = s * c_dec[drn] + _dot_tn(scale_rows(k, kdec_ref[drn]), v)
        return o

    def rows(t):
        return pl.ds(pl.multiple_of(t * c, c), c)

    def lat_step(drn, t):
        sl = rows(t)
        return step(drn, q_ref[sl, :], k_ref[sl, :], v_ref[sl, :], True)

    def finish(t, o):
        sl = rows(t)
        mu = jnp.mean(o, axis=-1, keepdims=True)
        oc = o - mu
        var = jnp.mean(oc * oc, axis=-1, keepdims=True)
        yn = oc * lax.rsqrt(var + NORM_EPS) * gn_ref[...]
        y_ref[sl, :] = (g_ref[sl, :].astype(F32) * yn).astype(BF16)

    s_ref[...] = jnp.zeros_like(s_ref)
    for t in range(n_ctx):
        for drn, tt in ((0, t), (1, n_ctx - 1 - t)):
            sl = slice(tt * c, (tt + 1) * c)
            step(drn, qc_ref[sl, :], kc_ref[sl, :], vc_ref[sl, :], False)

    half = n_lat // 2

    def first_half(t, carry):
        tb = n_lat - 1 - t
        o_ref[rows(t), :] = lat_step(0, t)
        o_ref[rows(tb), :] = lat_step(1, tb)
        return carry

    def second_half(t, carry):
        tb = n_lat - 1 - t
        finish(t, o_ref[rows(t), :] + lat_step(0, t))
        finish(tb, o_ref[rows(tb), :] + lat_step(1, tb))
        return carry

    lax.fori_loop(0, half, first_half, 0, unroll=4)
    lax.fori_loop(half, n_lat, second_half, 0, unroll=4)


def _retention(r, lg_f, lg_b, gn_g, b_sz, cn, s_len, d):
    h_n = RET_HEADS
    dk = d // h_n
    dv = 2 * dk
    c = RET_CHUNK
    assert cn % c == 0 and s_len % (2 * c) == 0
    c0 = (b_sz * s_len) // cn
    kq, kk, kv, kg = 0, d // dk, (2 * d) // dv, (4 * d) // dv

    def lg_spec():
        return pl.BlockSpec((1, 1, LANES), lambda b, h: (h, 0, 0))

    return pl.pallas_call(
        functools.partial(_ret_kernel, n_ctx=cn // c, n_lat=s_len // c),
        grid=(b_sz, h_n),
        in_specs=[
            lg_spec(), lg_spec(),
            pl.BlockSpec((1, dv), lambda b, h: (0, h)),
            pl.BlockSpec((cn, dk), lambda b, h: (c0 + b, kq + h)),
            pl.BlockSpec((cn, dk), lambda b, h: (c0 + b, kk + h)),
            pl.BlockSpec((cn, dv), lambda b, h: (c0 + b, kv + h)),
            pl.BlockSpec((s_len, dk), lambda b, h: (b, kq + h)),
            pl.BlockSpec((s_len, dk), lambda b, h: (b, kk + h)),
            pl.BlockSpec((s_len, dv), lambda b, h: (b, kv + h)),
            pl.BlockSpec((s_len, dv), lambda b, h: (b, kg + h)),
        ],
        out_specs=pl.BlockSpec((s_len, dv), lambda b, h: (b, h)),
        out_shape=jax.ShapeDtypeStruct((b_sz * s_len, h_n * dv), BF16),
        scratch_shapes=[
            pltpu.VMEM((s_len, dv), F32),
            pltpu.VMEM((2, dk, dv), F32),
            pltpu.VMEM((2, c, c), F32),
            pltpu.VMEM((2, c, LANES), F32),
            pltpu.VMEM((2, c, LANES), F32),
        ],
        name="retention",
        compiler_params=_cparams(("arbitrary", "arbitrary")),
    )(lg_f, lg_b, gn_g, r, r, r, r, r, r, r)


def _rope_angles(s_len, rot_dim):
    rows = s_len // GRID_W
    row = np.repeat(np.arange(rows, dtype=np.float32), GRID_W)
    col = (np.arange(s_len) % GRID_W).astype(np.float32)
    n_freq = rot_dim // 4
    inv = (np.float32(ROPE_BASE) ** (-np.arange(n_freq, dtype=np.float32) / np.float32(n_freq))).astype(np.float32)
    ang = np.concatenate([row[:, None] * inv, col[:, None] * inv], axis=-1).astype(np.float32)
    return np.cos(ang).astype(np.float32), np.sin(ang).astype(np.float32)


def _position_table(lat_table, ctx_row, pad_rows):
    width = lat_table.shape[1]
    return jnp.asarray(np.concatenate([lat_table, np.broadcast_to(ctx_row[None, :], (pad_rows, width))],
                                      axis=0).astype(np.float32))


def kernel(x, c, ctx, c_ctx, mod_w, mod_b, norm_mix_g, norm_ffn_g, ffn_w_gate, ffn_w_up, ffn_w_down, ab_w_in, ab_w_out, swa_sink, mla_q_norm_g, mla_w_q_b, mla_kv_norm_g, mla_w_kv_b, ret_w_in, ret_decay_logit_fwd, ret_decay_logit_bwd, ret_gn_g, ret_w_out, final_norm_g):
    b_sz, s_len, d = x.shape
    cn = ctx.shape[1]
    depth = mod_w.shape[0]
    assert depth == 2 and ab_w_in.shape[0] == 1 and ret_w_in.shape[0] == 1
    assert b_sz + 1 <= 8
    n_lat_rows = b_sz * s_len
    n_ctx_rows = b_sz * cn
    nt = n_lat_rows + n_ctx_rows

    tm = min(1024, n_ctx_rows, s_len)
    tq = min(256, s_len)
    tk = min(1024, s_len)
    assert n_ctx_rows % tm == 0 and s_len % tm == 0 and s_len % cn == 0 and s_len % tq == 0 and s_len % tk == 0

    def make_mrow(t):
        n_lat_tiles = n_lat_rows // t
        per_b = s_len // t
        return lambda i: jnp.where(i < n_lat_tiles, 1 + i // per_b, 0)

    mrow = make_mrow(tm)
    tm_o = min(512, tm)
    mrow_o = make_mrow(tm_o)

    x2d = x.reshape(n_lat_rows, d)
    ctx2d = ctx.reshape(n_ctx_rows, d)

    cond8 = jnp.zeros((8, d), F32).at[0].set(c_ctx).at[1:1 + b_sz].set(c)
    mod_all = _modulation(cond8, mod_w, mod_b).reshape(depth, 8, 6, d)

    cos_a, sin_a = _rope_angles(s_len, SWA_HEAD_DIM)
    cos_b, sin_b = _rope_angles(s_len, MLA_ROPE_DIM)
    cos_r, sin_r = _rope_angles(s_len, d // RET_HEADS)
    ones = np.ones((LANES,), np.float32)
    zeros = np.zeros((LANES,), np.float32)
    half = np.concatenate([np.ones((64,), np.float32), np.zeros((64,), np.float32)])
    z64 = np.zeros((s_len, 64), np.float32)
    table = functools.partial(_position_table, pad_rows=tm)
    per_b = s_len // tm
    trow = lambda i: jnp.where(i < n_lat_rows // tm, i % per_b, per_b)
    t_cos_a = table(np.concatenate([cos_a, cos_a], axis=1), ones)
    t_sin_a = table(np.concatenate([-sin_a, sin_a], axis=1), zeros)
    t_cos_b = table(np.concatenate([cos_b, cos_b, z64], axis=1), half)
    t_sin_b = table(np.concatenate([-sin_b, sin_b, z64], axis=1), zeros)
    t_cos_r = table(cos_r, ones)
    t_sin_r = table(sin_r, zeros)

    bf = lambda w: w.astype(BF16)
    w_ab_in = jnp.pad(ab_w_in[0], ((0, 0), (0, 2560 - ab_w_in.shape[2]))).astype(BF16)
    wq_b = jnp.pad(mla_w_q_b[0].reshape(MLA_Q_RANK, MLA_HEADS, MLA_NOPE_DIM + MLA_ROPE_DIM),
                   ((0, 0), (0, 0), (0, MLA_HEAD_PAD - MLA_NOPE_DIM - MLA_ROPE_DIM))
                   ).reshape(MLA_Q_RANK, MLA_HEADS * MLA_HEAD_PAD).astype(BF16)
    wkv_b = bf(mla_w_kv_b[0])
    tf = 512
    final_g = final_norm_g[None, :]
    f_hidden = ffn_w_gate.shape[2]
    later_weights = [ffn_w_gate.reshape(depth * d, f_hidden), ffn_w_up.reshape(depth * d, f_hidden),
                     ffn_w_down.reshape(depth * f_hidden, d), ab_w_out[0], ret_w_in[0], ret_w_out[0]]

    a_qkv, lat = _ab_in_proj(x2d, ctx2d, norm_mix_g[0:1], mod_all, 0, w_ab_in, t_cos_a, t_sin_a, mrow, trow, tm)
    qm, kcat, vext = _mla_proj(lat, mla_q_norm_g[0:1], mla_kv_norm_g[0:1], wq_b, wkv_b, t_cos_b, t_sin_b, trow, tm)
    oa, oa_c = _swa_attention(a_qkv, swa_sink[0], b_sz, cn, s_len, tq)
    ob, ob_c, (wg, wu, wd, w_ab_out, w_ret_in, w_ret_out) = _mla_attention(
        qm, kcat, vext, later_weights, b_sz, cn, s_len, min(1024, s_len), tk)
    wg = wg.reshape(depth, d, f_hidden)
    wu = wu.reshape(depth, d, f_hidden)
    wd = wd.reshape(depth, f_hidden, d)
    xt = _out_proj([oa, ob], x2d, [oa_c, ob_c], ctx2d, w_ab_out, mod_all, 0, mrow_o, tm_o)
    xt = _ffn(xt, norm_ffn_g[0:1], mod_all, 0, wg, wu, wd, final_g, mrow, nt, tm, tf, final=False)

    r = _ret_in_proj(xt, norm_mix_g[1:2], mod_all, 1, w_ret_in, t_cos_r, t_sin_r, mrow, trow, tm, 2048)
    lg_shape = (RET_HEADS, 1, LANES)
    lg_f = jnp.broadcast_to(ret_decay_logit_fwd[0].astype(F32)[:, None, None], lg_shape)
    lg_b = jnp.broadcast_to(ret_decay_logit_bwd[0].astype(F32)[:, None, None], lg_shape)
    y = _retention(r, lg_f, lg_b, ret_gn_g[0:1], b_sz, cn, s_len, d)
    xl = _out_proj([y], xt, None, None, w_ret_out, mod_all, 1, mrow_o, tm_o)
    out = _ffn(xl, norm_ffn_g[1:2], mod_all, 1, wg, wu, wd, final_g, mrow, n_lat_rows, tm, tf, final=True)
    return out.reshape(b_sz, s_len, d)
```
